```python
import math
import jax, jax.numpy as jnp
from jax import lax
import numpy as np

D_MODEL = 1024
BATCH = 8
SEQ = 2048
DEPTH = 4
DEC_BATCH = 32
DEC_SEQ = 1
PAST_LEN = 8192
PAGE_SIZE = 128

N_MIXERS = 3
N_HEADS = 8
N_KV_HEADS = 4
HEAD_DIM = D_MODEL // N_HEADS
GROUP = N_HEADS // N_KV_HEADS
Q_DIM = N_HEADS * HEAD_DIM
KV_DIM = N_KV_HEADS * HEAD_DIM
QKV_DIM = Q_DIM + 2 * KV_DIM
Q_BLOCK = 128
MOBA_BLOCK = 256
MOBA_TOPK = 3
MOBA_QCHUNK = 16
N_BUCKETS = 32
MAX_DISTANCE = 128
N_EXPERTS = 16
N_GROUPS = 4
EXPERTS_PER_GROUP = N_EXPERTS // N_GROUPS
TOP_K = 2
D_EXPERT = D_MODEL // 4
RMS_EPS = 1e-6
N_FOX_LAYERS = len(range(0, DEPTH, N_MIXERS))

kernel_name = "hybrid_fox_moba_stickbreak_moe_step"

F32 = jnp.float32


def rmsnorm(x, g):
    xf = x.astype(F32)
    y = xf * lax.rsqrt(jnp.mean(xf * xf, axis=-1, keepdims=True) + RMS_EPS)
    return (y * g.astype(F32)).astype(x.dtype)


def sweep_queries(fn, q_args, pos, block):
    tq = pos.shape[0]
    if tq <= block or tq % block:
        return fn(q_args, pos)
    nb = tq // block

    def split(a):
        return jnp.moveaxis(a.reshape(a.shape[0], nb, block, *a.shape[2:]), 1, 0)

    out = lax.map(lambda args: fn(args[0], args[1]),
                  (tuple(split(a) for a in q_args), pos.reshape(nb, block)))
    out = jnp.moveaxis(out, 0, 1)
    return out.reshape(out.shape[0], tq, *out.shape[3:])


def gather_pages(cache, layer, page_table):
    rows = cache[layer, page_table]
    return rows.reshape(rows.shape[0], rows.shape[1] * rows.shape[2], *rows.shape[3:])


def rel_bucket(dist):
    n = jnp.maximum(dist, 0)
    max_exact = N_BUCKETS // 2
    large = max_exact + (jnp.log(jnp.maximum(n, 1).astype(F32) / max_exact)
                         / math.log(MAX_DISTANCE / max_exact)
                         * (N_BUCKETS - max_exact)).astype(jnp.int32)
    large = jnp.minimum(large, N_BUCKETS - 1)
    return jnp.where(n < max_exact, n, large)


def fox_attention(q, k, v, d_q, d_k, q_pos):
    B, Tk = k.shape[:2]
    k_pos = jnp.arange(Tk, dtype=jnp.int32)
    d_k_t = jnp.moveaxis(d_k, 1, 2).reshape(B, N_KV_HEADS, GROUP, 1, Tk)
    scale = HEAD_DIM ** -0.5

    def block(args, pos):
        qb, dqb = args
        qg = qb.reshape(B, -1, N_KV_HEADS, GROUP, HEAD_DIM)
        s = jnp.einsum('bqkgd,bskd->bkgqs', qg, k, preferred_element_type=F32) * scale
        dq_t = jnp.moveaxis(dqb, 1, 2).reshape(B, N_KV_HEADS, GROUP, -1, 1)
        s = s + (dq_t - d_k_t)
        s = jnp.where(k_pos[None, :] <= pos[:, None], s, -jnp.inf)
        p = jax.nn.softmax(s, axis=-1)
        o = jnp.einsum('bkgqs,bskd->bqkgd', p, v)
        return o.reshape(B, -1, N_HEADS, HEAD_DIM).astype(q.dtype)

    return sweep_queries(block, (q, d_q), q_pos, Q_BLOCK)


def stick_breaking_attention(q, k, v, q_pos):
    B, Tk = k.shape[:2]
    k_pos = jnp.arange(Tk, dtype=jnp.int32)
    scale = HEAD_DIM ** -0.5

    def block(args, pos):
        (qb,) = args
        qg = qb.reshape(B, -1, N_KV_HEADS, GROUP, HEAD_DIM)
        z = jnp.einsum('bqkgd,bskd->bkgqs', qg, k, preferred_element_type=F32) * scale
        mask = k_pos[None, :] < pos[:, None]
        log_keep = jnp.where(mask, jax.nn.log_sigmoid(-z), 0.0)
        after = lax.cumsum(log_keep, axis=4, reverse=True) - log_keep
        a = jnp.where(mask, jnp.exp(jax.nn.log_sigmoid(z) + after), 0.0)
        o = jnp.einsum('bkgqs,bskd->bqkgd', a, v)
        return o.reshape(B, -1, N_HEADS, HEAD_DIM).astype(q.dtype)

    return sweep_queries(block, (q,), q_pos, Q_BLOCK)


def moba_attention(q, k, v, q_pos, rel_bias):
    B, Tk = k.shape[:2]
    nb = -(-Tk // MOBA_BLOCK)
    pad = nb * MOBA_BLOCK - Tk

    def to_blocks(a):
        a = jnp.pad(a, ((0, 0), (0, pad), (0, 0), (0, 0)))
        return a.reshape(B, nb, MOBA_BLOCK, N_KV_HEADS, HEAD_DIM).transpose(0, 3, 1, 2, 4)

    kb, vb = to_blocks(k), to_blocks(v)
    k_mean = jnp.mean(kb.astype(F32), axis=3)
    topk = min(MOBA_TOPK, nb)
    bias_tbl = rel_bias.reshape(N_BUCKETS, N_KV_HEADS, GROUP)
    b_ix = jnp.arange(B)[:, None, None, None, None]
    kv_ix = jnp.arange(N_KV_HEADS)[None, None, :, None, None]
    kv_ix6 = kv_ix[..., None]
    g_ix6 = jnp.arange(GROUP)[None, None, None, :, None, None]
    in_blk = jnp.arange(MOBA_BLOCK, dtype=jnp.int32)
    scale = HEAD_DIM ** -0.5

    def block(args, pos):
        (qb,) = args
        qg = qb.reshape(B, -1, N_KV_HEADS, GROUP, HEAD_DIM)
        own = pos // MOBA_BLOCK
        own5 = own[None, :, None, None, None]
        gate = jnp.einsum('bqkgd,bknd->bqkgn', qg, k_mean, preferred_element_type=F32)
        past_blk = jnp.arange(nb)[None, :] < own[:, None]
        gate = jnp.where(past_blk[None, :, None, None, :], gate, -jnp.inf)
        _, sel = lax.top_k(gate, topk)
        own_b = jnp.broadcast_to(own5, sel.shape[:-1] + (1,))
        idx = jnp.concatenate([sel, own_b], axis=-1)
        slot_ok = jnp.concatenate([sel < own5, jnp.ones(own_b.shape, bool)], axis=-1)
        kg = kb[b_ix, kv_ix, idx]
        vg = vb[b_ix, kv_ix, idx]
        s = jnp.einsum('bqkgd,bqkgnpd->bqkgnp', qg, kg, preferred_element_type=F32) * scale
        key_pos = idx[..., None] * MOBA_BLOCK + in_blk
        dist = pos[None, :, None, None, None, None] - key_pos
        bias = bias_tbl[rel_bucket(dist), kv_ix6, g_ix6]
        ok = slot_ok[..., None] & (dist >= 0)
        s = jnp.where(ok, s + bias, -jnp.inf)
        p = jax.nn.softmax(s.reshape(*s.shape[:4], -1), axis=-1).reshape(s.shape)
        o = jnp.einsum('bqkgnp,bqkgnpd->bqkgd', p, vg)
        return o.reshape(B, -1, N_HEADS, HEAD_DIM).astype(q.dtype)

    return sweep_queries(block, (q,), q_pos, MOBA_QCHUNK)


def moe_ffn(h, w_router, b_router, w_gate, w_up, w_down):
    logits = jnp.einsum('btd,de->bte', h, w_router, preferred_element_type=F32) + b_router
    probs = jax.nn.softmax(logits.astype(F32), axis=-1)
    grouped = probs.reshape(*probs.shape[:-1], N_GROUPS, EXPERTS_PER_GROUP)
    group_score = jnp.sum(lax.top_k(grouped, TOP_K)[0], axis=-1)
    g_sel = jnp.argmax(group_score, axis=-1)
    in_group = jnp.sum(grouped * jax.nn.one_hot(g_sel, N_GROUPS, dtype=F32)[..., None], axis=-2)
    top_p, top_i = lax.top_k(in_group, TOP_K)
    weights = top_p / jnp.sum(top_p, axis=-1, keepdims=True)
    expert_id = g_sel[..., None] * EXPERTS_PER_GROUP + top_i
    gates = jnp.sum(jax.nn.one_hot(expert_id, N_EXPERTS, dtype=F32) * weights[..., None], axis=-2)
    y = jnp.zeros(h.shape, F32)
    for e in range(N_EXPERTS):
        a = jax.nn.silu(h @ w_gate[e]) * (h @ w_up[e])
        y = y + gates[..., e:e + 1] * (a @ w_down[e])
    return y.astype(h.dtype)


def trunk(x, c, cache_k, cache_v, cache_logf, page_table, rel_bias, w_ada, b_ada, norm_g,
          final_g, w_in, w_out, w_fgate, b_fgate, w_router, b_router, w_gate, w_up, w_down):
    B, T, _ = x.shape
    paged = page_table is not None
    past_len = page_table.shape[1] * PAGE_SIZE if paged else 0
    q_pos = past_len + jnp.arange(T, dtype=jnp.int32)
    new_k, new_v, new_logf = [], [], []
    for l in range(DEPTH):
        kind = l % N_MIXERS
        mod = jnp.einsum('bd,de->be', jax.nn.silu(c), w_ada[l]) + b_ada[l]
        sh1, sc1, gt1, sh2, sc2, gt2 = [m[:, None, :] for m in jnp.split(mod, 6, axis=-1)]
        h = rmsnorm(x, norm_g[l, 0]) * (1 + sc1) + sh1
        qkv = h @ w_in[l]
        q = qkv[..., :Q_DIM].reshape(B, T, N_HEADS, HEAD_DIM)
        k = qkv[..., Q_DIM:Q_DIM + KV_DIM].reshape(B, T, N_KV_HEADS, HEAD_DIM)
        v = qkv[..., Q_DIM + KV_DIM:].reshape(B, T, N_KV_HEADS, HEAD_DIM)
        new_k.append(k)
        new_v.append(v)
        if paged:
            k_all = jnp.concatenate([gather_pages(cache_k, l, page_table), k], axis=1)
            v_all = jnp.concatenate([gather_pages(cache_v, l, page_table), v], axis=1)
        else:
            k_all, v_all = k, v
        if kind == 0:
            a = l // N_MIXERS
            logf = jax.nn.log_sigmoid((h @ w_fgate[a] + b_fgate[a]).astype(F32)).astype(x.dtype)
            new_logf.append(logf)
            lf = logf.astype(F32)
            if paged:
                lf = jnp.concatenate([gather_pages(cache_logf, a, page_table).astype(F32), lf], axis=1)
            d_all = jnp.cumsum(lf, axis=1)
            o = fox_attention(q, k_all, v_all, d_all[:, past_len:], d_all, q_pos)
        elif kind == 1:
            o = moba_attention(q, k_all, v_all, q_pos, rel_bias)
        else:
            o = stick_breaking_attention(q, k_all, v_all, q_pos)
        x = x + gt1 * (o.reshape(B, T, Q_DIM) @ w_out[l])
        h2 = rmsnorm(x, norm_g[l, 1]) * (1 + sc2) + sh2
        x = x + gt2 * moe_ffn(h2, w_router, b_router, w_gate[l], w_up[l], w_down[l])
    y = rmsnorm(x, final_g)
    return y, jnp.stack(new_k), jnp.stack(new_v), jnp.stack(new_logf)


def setup_inputs(seed: int = 0) -> dict:
    key = jax.random.key(seed)
    ks = jax.random.split(key, 24)
    n_pages = PAST_LEN // PAGE_SIZE
    n_used = DEC_BATCH * n_pages
    n_pool = n_used + -(-n_used // 4)

    def nrm(k, shape, s):
        return jax.random.normal(k, shape, F32) * s

    x_prompt = nrm(ks[0], (BATCH, SEQ, D_MODEL), 1.0)
    x_sample = nrm(ks[1], (DEC_BATCH, DEC_SEQ, D_MODEL), 1.0)
    cache_k = nrm(ks[2], (DEPTH, n_pool, PAGE_SIZE, N_KV_HEADS, HEAD_DIM), 1.0)
    cache_v = nrm(ks[3], (DEPTH, n_pool, PAGE_SIZE, N_KV_HEADS, HEAD_DIM), 1.0)
    cache_logf = jax.nn.log_sigmoid(nrm(ks[4], (N_FOX_LAYERS, n_pool, PAGE_SIZE, N_HEADS), 1.0))
    page_table = jax.random.permutation(ks[5], n_pool)[:n_used].reshape(DEC_BATCH, n_pages).astype(jnp.int32)
    c_prompt = nrm(ks[6], (BATCH, D_MODEL), 1.0)
    c_sample = nrm(ks[7], (DEC_BATCH, D_MODEL), 1.0)
    rel_bias = nrm(ks[8], (N_BUCKETS, N_HEADS), 0.5)
    w_ada = nrm(ks[9], (DEPTH, D_MODEL, 6 * D_MODEL), 0.5 * D_MODEL ** -0.5)
    b_ada = nrm(ks[10], (DEPTH, 6 * D_MODEL), 0.02)
    norm_g = 1.0 + nrm(ks[11], (DEPTH, 2, D_MODEL), 0.02)
    final_g = 1.0 + nrm(ks[12], (D_MODEL,), 0.02)
    w_in = nrm(ks[13], (DEPTH, D_MODEL, QKV_DIM), D_MODEL ** -0.5)
    w_out = nrm(ks[14], (DEPTH, Q_DIM, D_MODEL), Q_DIM ** -0.5)
    w_fgate = nrm(ks[15], (N_FOX_LAYERS, D_MODEL, N_HEADS), D_MODEL ** -0.5)
    b_fgate = nrm(ks[16], (N_FOX_LAYERS, N_HEADS), 0.1)
    w_router = nrm(ks[17], (D_MODEL, N_EXPERTS), D_MODEL ** -0.5)
    b_router = nrm(ks[18], (N_EXPERTS,), 0.01)
    w_gate = nrm(ks[19], (DEPTH, N_EXPERTS, D_MODEL, D_EXPERT), D_MODEL ** -0.5)
    w_up = nrm(ks[20], (DEPTH, N_EXPERTS, D_MODEL, D_EXPERT), D_MODEL ** -0.5)
    w_down = nrm(ks[21], (DEPTH, N_EXPERTS, D_EXPERT, D_MODEL), D_EXPERT ** -0.5)
    return {"x_prompt": x_prompt, "x_sample": x_sample, "cache_k": cache_k, "cache_v": cache_v,
            "cache_logf": cache_logf, "page_table": page_table, "c_prompt": c_prompt,
            "c_sample": c_sample, "rel_bias": rel_bias, "w_ada": w_ada, "b_ada": b_ada,
            "norm_g": norm_g, "final_g": final_g, "w_in": w_in, "w_out": w_out,
            "w_fgate": w_fgate, "b_fgate": b_fgate, "w_router": w_router, "b_router": b_router,
            "w_gate": w_gate, "w_up": w_up, "w_down": w_down}


def reference(x_prompt, x_sample, cache_k, cache_v, cache_logf, page_table, c_prompt, c_sample,
              rel_bias, w_ada, b_ada, norm_g, final_g, w_in, w_out, w_fgate, b_fgate,
              w_router, b_router, w_gate, w_up, w_down):
    y_prompt, k_prompt, v_prompt, logf_prompt = trunk(
        x_prompt, c_prompt, None, None, None, None, rel_bias, w_ada, b_ada, norm_g, final_g,
        w_in, w_out, w_fgate, b_fgate, w_router, b_router, w_gate, w_up, w_down)
    y_sample, k_sample, v_sample, logf_sample = trunk(
        x_sample, c_sample, cache_k, cache_v, cache_logf, page_table, rel_bias, w_ada, b_ada,
        norm_g, final_g, w_in, w_out, w_fgate, b_fgate, w_router, b_router, w_gate, w_up, w_down)
    return (y_prompt, y_sample, k_prompt, v_prompt, logf_prompt, k_sample, v_sample, logf_sample)
```

```python
import functools
import math

import jax
import jax.numpy as jnp
from jax import lax
from jax.experimental import pallas as pl
from jax.experimental.pallas import tpu as pltpu

F32 = jnp.float32
BF16 = jnp.bfloat16
I32 = jnp.int32

N_MIXERS = 3
N_HEADS = 8
N_KV_HEADS = 4
GROUP = N_HEADS // N_KV_HEADS
HEAD_DIM = 128
Q_DIM = N_HEADS * HEAD_DIM
KV_DIM = N_KV_HEADS * HEAD_DIM
PAGE_SIZE = 128
MOBA_BLOCK = 256
MOBA_TOPK = 3
N_BUCKETS = 32
MAX_DISTANCE = 128
N_EXPERTS = 16
N_GROUPS = 4
EXPERTS_PER_GROUP = N_EXPERTS // N_GROUPS
TOP_K = 2
RMS_EPS = 1e-6

PAGE_ROWS = PAGE_SIZE * N_KV_HEADS
LANES = 128
HEAD_ROWS = 16
MASKED = -1e30
VMEM_LIMIT = 56 * 1024 * 1024


def _params(*sem):
    return pltpu.CompilerParams(dimension_semantics=sem, vmem_limit_bytes=VMEM_LIMIT)


def _dot(a, b):
    return jnp.dot(a, b, preferred_element_type=F32)


def _dot_nt(a, b):
    return lax.dot_general(a, b, (((1,), (1,)), ((), ())), preferred_element_type=F32)


def _split2(x):
    hi = x.astype(BF16)
    lo = (x - hi.astype(F32)).astype(BF16)
    return hi, lo


def _split3(x):
    hi = x.astype(BF16)
    r = x - hi.astype(F32)
    mid = r.astype(BF16)
    lo = (r - mid.astype(F32)).astype(BF16)
    return hi, mid, lo


def _neg_softplus(z):
    return -(jnp.maximum(z, 0.0) + jnp.log1p(jnp.exp(-jnp.abs(z))))


def _silu(x):
    return x / (1.0 + jnp.exp(-x))


def _ada_kernel(c_ref, w_ref, b_ref, o_ref):
    s = _silu(c_ref[...])
    o_ref[...] = _dot(s.astype(BF16), w_ref[...].astype(BF16)) + b_ref[...]


def ada_modulation(c_all, w_ada, b_ada):
    depth, d, n = w_ada.shape
    mc = c_all.shape[0]
    tn = 1024
    return pl.pallas_call(
        _ada_kernel,
        grid=(depth, n // tn),
        in_specs=[
            pl.BlockSpec((mc, d), lambda l, j: (0, 0)),
            pl.BlockSpec((None, d, tn), lambda l, j: (l, 0, j)),
            pl.BlockSpec((None, 1, tn), lambda l, j: (l, 0, j)),
        ],
        out_specs=pl.BlockSpec((None, mc, tn), lambda l, j: (l, 0, j)),
        out_shape=jax.ShapeDtypeStruct((depth, mc, n), F32),
        compiler_params=_params("parallel", "parallel"),
        name="ada_modulation",
    )(c_all, w_ada, b_ada.reshape(depth, 1, n))


def _rms_mod(x, g, sc, sh):
    r = lax.rsqrt(jnp.mean(x * x, axis=-1, keepdims=True) + RMS_EPS)
    return (x * r * g) * (1.0 + sc) + sh


def _qkv_kernel(*refs, has_fgate, q_scale):
    if has_fgate:
        (x_ref, g_ref, sc_ref, sh_ref, w_ref, wf_ref, bf_ref,
         q_ref, k_ref, v_ref, kb_ref, vb_ref, lf_ref) = refs
    else:
        x_ref, g_ref, sc_ref, sh_ref, w_ref, q_ref, k_ref, v_ref, kb_ref, vb_ref = refs
    hb = _rms_mod(x_ref[...], g_ref[...], sc_ref[...], sh_ref[...]).astype(BF16)
    qkv = _dot(hb, w_ref[...])
    q_ref[...] = (qkv[:, :Q_DIM] * q_scale).astype(BF16)
    k = qkv[:, Q_DIM:Q_DIM + KV_DIM]
    v = qkv[:, Q_DIM + KV_DIM:]
    k_ref[...] = k
    v_ref[...] = v
    kb_ref[...] = k.astype(BF16)
    vb_ref[...] = v.astype(BF16)
    if has_fgate:
        z = _dot(hb, wf_ref[...]) + bf_ref[...]
        lf_ref[...] = jnp.minimum(z, 0.0) - jnp.log1p(jnp.exp(-jnp.abs(z)))


def _mod_spec(mod, tm, rows_per_seq):
    d = mod.shape[-1]
    if mod.ndim == 3:
        tiles = rows_per_seq // tm
        return pl.BlockSpec((None, 1, d), lambda i: (i // tiles, 0, 0))
    return pl.BlockSpec((tm, d), lambda i: (i, 0))


def norm_qkv(x, g, sc, sh, w_in_b, fgate, *, tm, rows_per_seq):
    m, d = x.shape
    n = w_in_b.shape[1]
    has_fgate = fgate is not None
    row = lambda i: (i, 0)
    const = lambda i: (0, 0)
    in_specs = [pl.BlockSpec((tm, d), row), pl.BlockSpec((1, d), const),
                _mod_spec(sc, tm, rows_per_seq), _mod_spec(sh, tm, rows_per_seq),
                pl.BlockSpec((d, n), const)]
    args = [x, g, sc, sh, w_in_b]
    out_specs = [pl.BlockSpec((tm, Q_DIM), row), pl.BlockSpec((tm, KV_DIM), row),
                 pl.BlockSpec((tm, KV_DIM), row), pl.BlockSpec((tm, KV_DIM), row),
                 pl.BlockSpec((tm, KV_DIM), row)]
    out_shape = [jax.ShapeDtypeStruct((m, Q_DIM), BF16), jax.ShapeDtypeStruct((m, KV_DIM), F32),
                 jax.ShapeDtypeStruct((m, KV_DIM), F32), jax.ShapeDtypeStruct((m, KV_DIM), BF16),
                 jax.ShapeDtypeStruct((m, KV_DIM), BF16)]
    if has_fgate:
        wf, bf = fgate
        in_specs += [pl.BlockSpec((d, LANES), const), pl.BlockSpec((1, LANES), const)]
        args += [wf, bf]
        out_specs.append(pl.BlockSpec((tm, LANES), row))
        out_shape.append(jax.ShapeDtypeStruct((m, LANES), F32))
    return pl.pallas_call(
        functools.partial(_qkv_kernel, has_fgate=has_fgate, q_scale=HEAD_DIM ** -0.5),
        grid=(m // tm,),
        in_specs=in_specs, out_specs=out_specs, out_shape=out_shape,
        compiler_params=_params("parallel"),
        name="norm_qkv",
    )(*args)


def _cumsum_kernel(lf_ref, d_ref, dt_ref, *, chunk):
    t = lf_ref.shape[0]
    row = lax.broadcasted_iota(I32, (chunk, chunk), 0)
    col = lax.broadcasted_iota(I32, (chunk, chunk), 1)
    tri = jnp.where(col <= row, 1.0, 0.0).astype(BF16)
    carry = jnp.zeros((1, LANES), F32)
    for c in range(t // chunk):
        sl = slice(c * chunk, (c + 1) * chunk)
        hi, mid, lo = _split3(lf_ref[sl, :])
        cs = _dot(tri, hi) + _dot(tri, mid) + _dot(tri, lo) + carry
        d_ref[sl, :] = cs
        dt_ref[:, sl] = cs.T[:N_HEADS, :]
        carry = cs[chunk - 1:chunk, :]


def cumsum_time(lf):
    b, t, _ = lf.shape
    chunk = min(256, t)
    return pl.pallas_call(
        functools.partial(_cumsum_kernel, chunk=chunk),
        grid=(b,),
        in_specs=[pl.BlockSpec((None, t, LANES), lambda i: (i, 0, 0))],
        out_specs=[pl.BlockSpec((None, t, LANES), lambda i: (i, 0, 0)),
                   pl.BlockSpec((None, N_HEADS, t), lambda i: (i, 0, 0))],
        out_shape=[jax.ShapeDtypeStruct((b, t, LANES), F32),
                   jax.ShapeDtypeStruct((b, N_HEADS, t), F32)],
        compiler_params=_params("parallel"),
        name="cumsum_time",
    )(lf)


def _lane_pick(x, idx):
    lane = lax.broadcasted_iota(I32, x.shape, 1)
    return jnp.sum(jnp.where(lane == idx, x, 0.0), axis=1, keepdims=True)


def _softmax_step(s, v, m, l, acc):
    m_new = jnp.maximum(m, jnp.max(s, axis=1, keepdims=True))
    alpha = jnp.exp(m - m_new)
    p = jnp.exp(s - m_new)
    l = l * alpha + jnp.sum(p, axis=1, keepdims=True)
    acc = acc * alpha + _dot(p.astype(BF16), v)
    return m_new, l, acc


def _fox_kernel(q_ref, k_ref, v_ref, d_ref, dt_ref, o_ref, *, tq):
    kvh = pl.program_id(1)
    i = pl.program_id(2)
    q0 = pl.multiple_of(i * tq, tq)
    row = lax.broadcasted_iota(I32, (tq, tq), 0)
    col = lax.broadcasted_iota(I32, (tq, tq), 1)
    causal = col <= row
    d_tile = d_ref[...]
    for g in range(GROUP):
        h = kvh * GROUP + g
        qh = q_ref[:, g * HEAD_DIM:(g + 1) * HEAD_DIM]
        dq = _lane_pick(d_tile, h)

        def logits(start):
            dk = dt_ref[pl.ds(h, 1), pl.ds(start, tq)]
            return _dot_nt(qh, k_ref[pl.ds(start, tq), :]) + (dq - dk)

        s = jnp.where(causal, logits(q0), MASKED)
        m = jnp.max(s, axis=1, keepdims=True)
        p = jnp.exp(s - m)
        l = jnp.sum(p, axis=1, keepdims=True)
        acc = _dot(p.astype(BF16), v_ref[pl.ds(q0, tq), :])

        def body(j, carry):
            start = pl.multiple_of(j * tq, tq)
            return _softmax_step(logits(start), v_ref[pl.ds(start, tq), :], *carry)

        m, l, acc = lax.fori_loop(0, i, body, (m, l, acc))
        o_ref[:, g * HEAD_DIM:(g + 1) * HEAD_DIM] = (acc / l).astype(BF16)


def _attn_specs(t, tq):
    qo = pl.BlockSpec((None, tq, GROUP * HEAD_DIM), lambda b, kvh, i: (b, i, kvh))
    kv = pl.BlockSpec((None, t, HEAD_DIM), lambda b, kvh, i: (b, 0, kvh))
    return qo, kv


def fox_attention_prompt(q, kb, vb, d, dt):
    b, t, _ = q.shape
    tq = min(512, t)
    qo, kv = _attn_specs(t, tq)
    return pl.pallas_call(
        functools.partial(_fox_kernel, tq=tq),
        grid=(b, N_KV_HEADS, t // tq),
        in_specs=[qo, kv, kv,
                  pl.BlockSpec((None, tq, LANES), lambda b, kvh, i: (b, i, 0)),
                  pl.BlockSpec((None, N_HEADS, t), lambda b, kvh, i: (b, 0, 0))],
        out_specs=qo,
        out_shape=jax.ShapeDtypeStruct((b, t, Q_DIM), BF16),
        compiler_params=_params("parallel", "parallel", "parallel"),
        name="fox_attention_prompt",
    )(q, kb, vb, d, dt)


def _suffix_matrices(n):
    row = lax.broadcasted_iota(I32, (n, n), 0)
    col = lax.broadcasted_iota(I32, (n, n), 1)
    return jnp.where(row > col, 1.0, 0.0).astype(BF16), jnp.ones((n, n), BF16)


def _sb_kernel(q_ref, k_ref, v_ref, o_ref, *, tq, tk):
    i = pl.program_id(2)
    q0 = pl.multiple_of(i * tq, tq)
    upper, _ = _suffix_matrices(tk)
    row = lax.broadcasted_iota(I32, (tq, tk), 0)
    col = lax.broadcasted_iota(I32, (tq, tk), 1)
    for g in range(GROUP):
        qh = q_ref[:, g * HEAD_DIM:(g + 1) * HEAD_DIM]

        def chunk(start, c, acc, mask):
            z = _dot_nt(qh, k_ref[pl.ds(start, tk), :])
            lk = _neg_softplus(z)
            if mask is not None:
                lk = jnp.where(mask, lk, 0.0)
            hi, lo = _split2(lk)
            after = _dot(hi, upper) + _dot(lo, upper) + c
            a = jnp.exp(z + lk + after)
            if mask is not None:
                a = jnp.where(mask, a, 0.0)
            acc = acc + _dot(a.astype(BF16), v_ref[pl.ds(start, tk), :])
            return c + jnp.sum(lk, axis=1, keepdims=True), acc

        c = jnp.zeros((tq, 1), F32)
        acc = jnp.zeros((tq, HEAD_DIM), F32)
        for mth in reversed(range(tq // tk)):
            start = pl.multiple_of(q0 + mth * tk, tk)
            c, acc = chunk(start, c, acc, (col + mth * tk) < row)

        n_past = i * (tq // tk)

        def body(it, carry):
            start = pl.multiple_of((n_past - 1 - it) * tk, tk)
            return chunk(start, carry[0], carry[1], None)

        c, acc = lax.fori_loop(0, n_past, body, (c, acc))
        o_ref[:, g * HEAD_DIM:(g + 1) * HEAD_DIM] = acc.astype(BF16)


def sb_attention_prompt(q, kb, vb):
    b, t, _ = q.shape
    tq = min(512, t)
    tk = min(256, t)
    qo, kv = _attn_specs(t, tq)
    return pl.pallas_call(
        functools.partial(_sb_kernel, tq=tq, tk=tk),
        grid=(b, N_KV_HEADS, t // tq),
        in_specs=[qo, kv, kv],
        out_specs=qo,
        out_shape=jax.ShapeDtypeStruct((b, t, Q_DIM), BF16),
        compiler_params=_params("parallel", "parallel", "parallel"),
        name="sb_attention_prompt",
    )(q, kb, vb)


def _block_mean_kernel(k_ref, o_ref, *, nb):
    t, w = k_ref.shape
    sums = jnp.sum(k_ref[...].reshape(nb, t // nb, w), axis=1)
    o_ref[...] = jnp.zeros(o_ref.shape, F32)
    o_ref[0:nb, :] = sums * (1.0 / (t // nb))


def block_mean(k):
    b, t, w = k.shape
    nb = t // MOBA_BLOCK
    return pl.pallas_call(
        functools.partial(_block_mean_kernel, nb=nb),
        grid=(b,),
        in_specs=[pl.BlockSpec((None, t, w), lambda i: (i, 0, 0))],
        out_specs=pl.BlockSpec((None, LANES, w), lambda i: (i, 0, 0)),
        out_shape=jax.ShapeDtypeStruct((b, LANES, w), F32),
        compiler_params=_params("parallel"),
        name="block_mean",
    )(k)


def _top_mask(gate, n_valid, n_cand, axis):
    idx = lax.broadcasted_iota(I32, gate.shape, axis)
    cnt = jnp.zeros(gate.shape, I32)
    for jp in range(n_cand):
        other = gate[:, jp:jp + 1] if axis == 1 else gate[jp:jp + 1, :]
        beats = (other > gate) | ((other == gate) & (jp < idx))
        cnt = cnt + jnp.where(beats, jnp.where(jp < n_valid, 1, 0), 0)
    return (idx < n_valid) & (cnt < MOBA_TOPK)


def _moba_kernel(q_ref, k_ref, v_ref, km_ref, b0_ref, b1_ref, far_ref, o_ref, *, tq, nb):
    i = pl.program_id(2)
    q0 = pl.multiple_of(i * tq, tq)
    row = lax.broadcasted_iota(I32, (tq, tq), 0)
    col = lax.broadcasted_iota(I32, (tq, tq), 1)
    causal = col <= row
    km_hi, km_lo = _split2(km_ref[...])
    for g in range(GROUP):
        qh = q_ref[:, g * HEAD_DIM:(g + 1) * HEAD_DIM]
        gate = _dot_nt(qh, km_hi) + _dot_nt(qh, km_lo)
        sel = jnp.where(_top_mask(gate, i, nb, 1), 1.0, 0.0)

        s = jnp.where(causal, _dot_nt(qh, k_ref[pl.ds(q0, tq), :]) + b0_ref[g], MASKED)
        m = jnp.max(s, axis=1, keepdims=True)
        p = jnp.exp(s - m)
        l = jnp.sum(p, axis=1, keepdims=True)
        acc = _dot(p.astype(BF16), v_ref[pl.ds(q0, tq), :])

        def body(j, carry):
            start = pl.multiple_of(j * tq, tq)
            bias = jnp.where(j == i - 1, b1_ref[g], far_ref[g])
            s = _dot_nt(qh, k_ref[pl.ds(start, tq), :]) + bias
            s = jnp.where(_lane_pick(sel, j) > 0.0, s, MASKED)
            return _softmax_step(s, v_ref[pl.ds(start, tq), :], *carry)

        m, l, acc = lax.fori_loop(0, i, body, (m, l, acc))
        o_ref[:, g * HEAD_DIM:(g + 1) * HEAD_DIM] = (acc / l).astype(BF16)


def moba_attention_prompt(q, kb, vb, kmean, bias0, bias1, far):
    b, t, _ = q.shape
    tq = MOBA_BLOCK
    nb = t // tq
    qo, kv = _attn_specs(t, tq)
    head_pair = lambda b, kvh, i: (kvh, 0, 0)
    return pl.pallas_call(
        functools.partial(_moba_kernel, tq=tq, nb=nb),
        grid=(b, N_KV_HEADS, nb),
        in_specs=[qo, kv, kv,
                  pl.BlockSpec((None, LANES, HEAD_DIM), lambda b, kvh, i: (b, 0, kvh)),
                  pl.BlockSpec((GROUP, tq, tq), head_pair),
                  pl.BlockSpec((GROUP, tq, tq), head_pair),
                  pl.BlockSpec((GROUP, 1, tq), head_pair)],
        out_specs=qo,
        out_shape=jax.ShapeDtypeStruct((b, t, Q_DIM), BF16),
        compiler_params=_params("parallel", "parallel", "parallel"),
        name="moba_attention_prompt",
    )(q, kb, vb, kmean, bias0, bias1, far)


def _route(logits):
    idx = lax.broadcasted_iota(I32, logits.shape, 1)
    valid = idx < N_EXPERTS
    lg = jnp.where(valid, logits, MASKED)
    e = jnp.where(valid, jnp.exp(lg - jnp.max(lg, axis=1, keepdims=True)), 0.0)
    grp = idx >> 2

    def peers(x, shifts):
        for s in shifts:
            for sh in (s, LANES - s):
                oi = pltpu.roll(idx, sh, 1)
                yield pltpu.roll(x, sh, 1), oi, oi < N_EXPERTS

    cnt = jnp.zeros(logits.shape, I32)
    for oe, oi, ok in peers(e, (1, 2, 3)):
        beats = ok & ((oi >> 2) == grp) & ((oe > e) | ((oe == e) & (oi < idx)))
        cnt = cnt + jnp.where(beats, 1, 0)
    top2 = valid & (cnt < TOP_K)
    t2e = jnp.where(top2, e, 0.0)
    score = t2e
    for ot, oi, ok in peers(t2e, (1, 2, 3)):
        score = score + jnp.where(ok & ((oi >> 2) == grp), ot, 0.0)
    lost = jnp.zeros(logits.shape, I32)
    for osc, oi, ok in peers(score, (4, 8, 12)):
        beats = ok & ((osc > score) | ((osc == score) & ((oi >> 2) < grp)))
        lost = lost + jnp.where(beats, 1, 0)
    return jnp.where(top2 & (lost == 0), e / score, 0.0)


def _oproj_kernel(o_ref, x_ref, w_ref, gt_ref, g_ref, sc_ref, sh_ref, wrh_ref, wrl_ref, br_ref,
                  x1_ref, h2_ref, gates_ref):
    x1 = x_ref[...] + gt_ref[...] * _dot(o_ref[...], w_ref[...])
    x1_ref[...] = x1
    h2 = _rms_mod(x1, g_ref[...], sc_ref[...], sh_ref[...])
    hi, lo = _split2(h2)
    h2_ref[...] = hi
    logits = _dot(hi, wrh_ref[...]) + _dot(lo, wrh_ref[...]) + _dot(hi, wrl_ref[...]) + br_ref[...]
    gates_ref[...] = _route(logits)


def out_proj_router(o, x, w_out_b, gt, g2, sc, sh, wr_hi, wr_lo, br, *, tm, rows_per_seq):
    m, d = x.shape
    row = lambda i: (i, 0)
    const = lambda i: (0, 0)
    ms = lambda a: _mod_spec(a, tm, rows_per_seq)
    return pl.pallas_call(
        _oproj_kernel,
        grid=(m // tm,),
        in_specs=[pl.BlockSpec((tm, Q_DIM), row), pl.BlockSpec((tm, d), row),
                  pl.BlockSpec((Q_DIM, d), const), ms(gt), pl.BlockSpec((1, d), const),
                  ms(sc), ms(sh), pl.BlockSpec((d, LANES), const),
                  pl.BlockSpec((d, LANES), const), pl.BlockSpec((1, LANES), const)],
        out_specs=[pl.BlockSpec((tm, d), row), pl.BlockSpec((tm, d), row),
                   pl.BlockSpec((tm, LANES), row)],
        out_shape=[jax.ShapeDtypeStruct((m, d), F32), jax.ShapeDtypeStruct((m, d), BF16),
                   jax.ShapeDtypeStruct((m, LANES), F32)],
        compiler_params=_params("parallel"),
        name="out_proj_router",
    )(o, x, w_out_b, gt, g2, sc, sh, wr_hi, wr_lo, br)


def _moe_kernel(h_ref, gates_ref, x_ref, gt_ref, wg_ref, wu_ref, wd_ref, o_ref, acc_ref, *, epc):
    c = pl.program_id(1)

    @pl.when(c == 0)
    def _():
        acc_ref[...] = jnp.zeros(acc_ref.shape, F32)

    h = h_ref[...]
    gates = gates_ref[...]
    for j in range(epc):
        gcol = _lane_pick(gates, c * epc + j)
        act = _silu(_dot(h, wg_ref[j])) * _dot(h, wu_ref[j]) * gcol
        acc_ref[...] += _dot(act.astype(BF16), wd_ref[j])

    @pl.when(c == pl.num_programs(1) - 1)
    def _():
        o_ref[...] = x_ref[...] + gt_ref[...] * acc_ref[...]


def moe_ffn(h2, gates, x1, gt, wg_b, wu_b, wd_b, *, tm, rows_per_seq):
    m, d = x1.shape
    n_e, _, de = wg_b.shape
    epc = 2
    row = lambda i, c: (i, 0)
    gt_spec = _mod_spec(gt, tm, rows_per_seq)
    gt_spec = pl.BlockSpec(gt_spec.block_shape, lambda i, c, f=gt_spec.index_map: f(i))
    return pl.pallas_call(
        functools.partial(_moe_kernel, epc=epc),
        grid=(m // tm, n_e // epc),
        in_specs=[pl.BlockSpec((tm, d), row), pl.BlockSpec((tm, LANES), row),
                  pl.BlockSpec((tm, d), row), gt_spec,
                  pl.BlockSpec((epc, d, de), lambda i, c: (c, 0, 0)),
                  pl.BlockSpec((epc, d, de), lambda i, c: (c, 0, 0)),
                  pl.BlockSpec((epc, de, d), lambda i, c: (c, 0, 0))],
        out_specs=pl.BlockSpec((tm, d), row),
        out_shape=jax.ShapeDtypeStruct((m, d), F32),
        scratch_shapes=[pltpu.VMEM((tm, d), F32)],
        compiler_params=_params("parallel", "arbitrary"),
        name="moe_ffn",
    )(h2, gates, x1, gt, wg_b, wu_b, wd_b)


def _final_kernel(x_ref, g_ref, o_ref):
    x = x_ref[...]
    o_ref[...] = x * lax.rsqrt(jnp.mean(x * x, axis=-1, keepdims=True) + RMS_EPS) * g_ref[...]


def final_norm(x, g, *, tm):
    m, d = x.shape
    return pl.pallas_call(
        _final_kernel,
        grid=(m // tm,),
        in_specs=[pl.BlockSpec((tm, d), lambda i: (i, 0)), pl.BlockSpec((1, d), lambda i: (0, 0))],
        out_specs=pl.BlockSpec((tm, d), lambda i: (i, 0)),
        out_shape=jax.ShapeDtypeStruct((m, d), F32),
        compiler_params=_params("parallel"),
        name="final_norm",
    )(x, g)


def _head_rows(mats):
    rowi = lax.broadcasted_iota(I32, mats[0].shape, 0)
    out = mats[0]
    for kvh in range(1, N_KV_HEADS):
        out = jnp.where((rowi >> 1) == kvh, mats[kvh], out)
    return out


def _kv_slices(x):
    return [x[:, kvh * HEAD_DIM:(kvh + 1) * HEAD_DIM] for kvh in range(N_KV_HEADS)]


def _kv_rows(page_ref):
    return [page_ref[pl.ds(kvh, PAGE_SIZE, stride=N_KV_HEADS), :] for kvh in range(N_KV_HEADS)]


def _new_token_logits(q, kn):
    qf = q.astype(F32)
    prods = [qf * ks.astype(BF16).astype(F32) for ks in _kv_slices(kn)]
    return jnp.sum(_head_rows(prods), axis=1, keepdims=True)


def _new_token_values(vn):
    return _head_rows([jnp.broadcast_to(vs.astype(BF16).astype(F32), (HEAD_ROWS, HEAD_DIM))
                       for vs in _kv_slices(vn)])


def _page_spec(layer, n_pages, page_of):
    return pl.BlockSpec((None, None, PAGE_ROWS, HEAD_DIM),
                        lambda b, c, pt: (layer, pt[b * n_pages + page_of(c)], 0, 0))


def _seq_spec(rows, width):
    return pl.BlockSpec((None, rows, width), lambda b, c, pt: (b, 0, 0))


def _suffix_and_total(x, upper, ones):
    hi, lo = _split2(x)
    return _dot(hi, upper) + _dot(lo, upper), _dot(hi, ones) + _dot(lo, ones)


def _fox_dec_kernel(pt_ref, q_ref, kn_ref, vn_ref, lfn_ref, *refs, pages):
    k_refs = refs[:pages]
    v_refs = refs[pages:2 * pages]
    lf_refs = refs[2 * pages:3 * pages]
    o_ref, m_s, l_s, acc_s, car_s = refs[3 * pages:]
    c = pl.program_id(1)
    q = q_ref[...]
    upper, ones = _suffix_matrices(PAGE_SIZE)

    @pl.when(c == 0)
    def _():
        m_s[...] = jnp.broadcast_to(_new_token_logits(q, kn_ref[...]), m_s.shape)
        l_s[...] = jnp.ones(l_s.shape, F32)
        acc_s[...] = _new_token_values(vn_ref[...])
        car_s[...] = lfn_ref[...]

    car = car_s[...]
    s_pages = [None] * pages
    for r in reversed(range(pages)):
        lfp = jnp.concatenate([lf_refs[r][...], jnp.zeros((HEAD_ROWS - N_HEADS, PAGE_SIZE), F32)], axis=0)
        suf, tot = _suffix_and_total(lfp, upper, ones)
        s_pages[r] = _head_rows([_dot_nt(q, ks.astype(BF16)) for ks in _kv_rows(k_refs[r])]) + (car + suf)
        car = car + tot
    car_s[...] = car

    m_old = m_s[...]
    mx = functools.reduce(jnp.maximum, s_pages)
    m_new = jnp.maximum(m_old, jnp.max(mx, axis=1, keepdims=True))
    alpha = jnp.exp(m_old - m_new)
    p_pages = [jnp.exp(s - m_new) for s in s_pages]
    l_s[...] = l_s[...] * alpha + jnp.sum(functools.reduce(jnp.add, p_pages), axis=1, keepdims=True)
    pv = [jnp.zeros((HEAD_ROWS, HEAD_DIM), F32)] * N_KV_HEADS
    for r in range(pages):
        pb = p_pages[r].astype(BF16)
        pv = [a + _dot(pb, vs.astype(BF16)) for a, vs in zip(pv, _kv_rows(v_refs[r]))]
    acc_s[...] = acc_s[...] * alpha + _head_rows(pv)
    m_s[...] = m_new

    @pl.when(c == pl.num_programs(1) - 1)
    def _():
        o_ref[...] = acc_s[...] / l_s[...]


def fox_attention_decode(q16, kn, vn, lfn, cache_k, cache_v, cache_lft, page_table, layer, fox_layer):
    b = q16.shape[0]
    n_pages = page_table.shape[1]
    pages = min(8, n_pages)
    nc = n_pages // pages
    page = lambda r: (lambda c: (nc - 1 - c) * pages + r)
    kv_specs = [_page_spec(layer, n_pages, page(r)) for r in range(pages)]
    lf_specs = [pl.BlockSpec((None, None, N_HEADS, PAGE_SIZE),
                             lambda b, c, pt, f=page(r): (fox_layer, pt[b * n_pages + f(c)], 0, 0))
                for r in range(pages)]
    grid_spec = pltpu.PrefetchScalarGridSpec(
        num_scalar_prefetch=1,
        grid=(b, nc),
        in_specs=[_seq_spec(HEAD_ROWS, HEAD_DIM), _seq_spec(1, KV_DIM), _seq_spec(1, KV_DIM),
                  _seq_spec(HEAD_ROWS, PAGE_SIZE)] + kv_specs + kv_specs + lf_specs,
        out_specs=_seq_spec(HEAD_ROWS, HEAD_DIM),
        scratch_shapes=[pltpu.VMEM((HEAD_ROWS, LANES), F32)] * 4,
    )
    return pl.pallas_call(
        functools.partial(_fox_dec_kernel, pages=pages),
        grid_spec=grid_spec,
        out_shape=jax.ShapeDtypeStruct((b, HEAD_ROWS, HEAD_DIM), F32),
        compiler_params=_params("parallel", "arbitrary"),
        name="fox_attention_decode",
    )(page_table.reshape(-1), q16, kn, vn, lfn, *([cache_k] * pages), *([cache_v] * pages),
      *([cache_lft] * pages))


def _sb_dec_kernel(pt_ref, q_ref, *refs, pages):
    k_refs = refs[:pages]
    v_refs = refs[pages:2 * pages]
    o_ref, acc_s, car_s = refs[2 * pages:]
    c = pl.program_id(1)
    q = q_ref[...]
    upper, ones = _suffix_matrices(PAGE_SIZE)

    @pl.when(c == 0)
    def _():
        acc_s[...] = jnp.zeros(acc_s.shape, F32)
        car_s[...] = jnp.zeros(car_s.shape, F32)

    car = car_s[...]
    pv = [jnp.zeros((HEAD_ROWS, HEAD_DIM), F32)] * N_KV_HEADS
    for r in reversed(range(pages)):
        z = _head_rows([_dot_nt(q, ks.astype(BF16)) for ks in _kv_rows(k_refs[r])])
        lk = _neg_softplus(z)
        suf, tot = _suffix_and_total(lk, upper, ones)
        a = jnp.exp(z + lk + (car + suf)).astype(BF16)
        car = car + tot
        pv = [x + _dot(a, vs.astype(BF16)) for x, vs in zip(pv, _kv_rows(v_refs[r]))]
    car_s[...] = car
    acc_s[...] += _head_rows(pv)

    @pl.when(c == pl.num_programs(1) - 1)
    def _():
        o_ref[...] = acc_s[...]


def sb_attention_decode(q16, cache_k, cache_v, page_table, layer):
    b = q16.shape[0]
    n_pages = page_table.shape[1]
    pages = min(8, n_pages)
    nc = n_pages // pages
    kv_specs = [_page_spec(layer, n_pages, lambda c, r=r: (nc - 1 - c) * pages + r)
                for r in range(pages)]
    grid_spec = pltpu.PrefetchScalarGridSpec(
        num_scalar_prefetch=1,
        grid=(b, nc),
        in_specs=[_seq_spec(HEAD_ROWS, HEAD_DIM)] + kv_specs + kv_specs,
        out_specs=_seq_spec(HEAD_ROWS, HEAD_DIM),
        scratch_shapes=[pltpu.VMEM((HEAD_ROWS, LANES), F32)] * 2,
    )
    return pl.pallas_call(
        functools.partial(_sb_dec_kernel, pages=pages),
        grid_spec=grid_spec,
        out_shape=jax.ShapeDtypeStruct((b, HEAD_ROWS, HEAD_DIM), F32),
        compiler_params=_params("parallel", "arbitrary"),
        name="sb_attention_decode",
    )(page_table.reshape(-1), q16, *([cache_k] * pages), *([cache_v] * pages))


def _page_head_sums(page_ref):
    x = page_ref[...]
    s8 = jnp.sum(x.reshape(PAGE_ROWS // 8, 8, HEAD_DIM), axis=0)
    return s8[:N_KV_HEADS] + s8[N_KV_HEADS:]


def _moba_sel_kernel(pt_ref, q_ref, *refs, pages, nb):
    k_refs = refs[:pages]
    km_ref, sel_ref = refs[pages:]
    c = pl.program_id(1)
    bps = pages // 2
    sums = [_page_head_sums(k_refs[r]) for r in range(pages)]
    means = [(sums[2 * j] + sums[2 * j + 1]) * (1.0 / MOBA_BLOCK) for j in range(bps)]
    row0 = pl.multiple_of(c * bps, bps)
    for kvh in range(N_KV_HEADS):
        km_ref[kvh, pl.ds(row0, bps), :] = jnp.concatenate([mj[kvh:kvh + 1] for mj in means], axis=0)

    @pl.when(c == pl.num_programs(1) - 1)
    def _():
        q = q_ref[...]
        lane = lax.broadcasted_iota(I32, (nb, LANES), 1)
        gate = jnp.zeros((nb, LANES), F32)
        for kvh in range(N_KV_HEADS):
            hi, lo = _split2(km_ref[kvh])
            gate = jnp.where((lane >> 1) == kvh, _dot_nt(hi, q) + _dot_nt(lo, q), gate)
        sel = _top_mask(gate, nb, nb, 0)
        blk = lax.broadcasted_iota(I32, (nb, LANES), 0).astype(F32)
        rows = []
        for _ in range(MOBA_TOPK):
            first = jnp.min(jnp.where(sel, blk, float(nb)), axis=0, keepdims=True)
            rows.append(first.astype(I32))
            sel = sel & (blk != first)
        rows.append(jnp.zeros((8 - MOBA_TOPK, LANES), I32))
        sel_ref[...] = jnp.concatenate(rows, axis=0)


def moba_select_decode(q128, cache_k, page_table, layer):
    b = q128.shape[0]
    n_pages = page_table.shape[1]
    nb = n_pages * PAGE_SIZE // MOBA_BLOCK
    pages = min(16, n_pages)
    nc = n_pages // pages
    k_specs = [_page_spec(layer, n_pages, lambda c, r=r: c * pages + r) for r in range(pages)]
    grid_spec = pltpu.PrefetchScalarGridSpec(
        num_scalar_prefetch=1,
        grid=(b, nc),
        in_specs=[_seq_spec(LANES, HEAD_DIM)] + k_specs,
        out_specs=[pl.BlockSpec((None, N_KV_HEADS, nb, HEAD_DIM), lambda b, c, pt: (b, 0, 0, 0)),
                   _seq_spec(8, LANES)],
    )
    _, sel = pl.pallas_call(
        functools.partial(_moba_sel_kernel, pages=pages, nb=nb),
        grid_spec=grid_spec,
        out_shape=[jax.ShapeDtypeStruct((b, N_KV_HEADS, nb, HEAD_DIM), F32),
                   jax.ShapeDtypeStruct((b, 8, LANES), I32)],
        compiler_params=_params("parallel", "arbitrary"),
        name="moba_select_decode",
    )(page_table.reshape(-1), q128, *([cache_k] * pages))
    return sel


def _moba_dec_kernel(pt_ref, sel_ref, q_ref, kn_ref, vn_ref, bt_ref, *refs, nb):
    n = MOBA_TOPK * 2
    k_refs = refs[:n]
    v_refs = refs[n:2 * n]
    o_ref = refs[2 * n]
    b = pl.program_id(0)
    h = pl.program_id(1)
    head_rows = pl.ds(h // GROUP, PAGE_SIZE, stride=N_KV_HEADS)
    q = q_ref[...]
    bt = bt_ref[...]
    far = bt[:, 2 * PAGE_SIZE:3 * PAGE_SIZE]
    s_new = _new_token_logits(q, kn_ref[...]) + bt[:, 3 * PAGE_SIZE:3 * PAGE_SIZE + 1]

    s_piece = []
    for slot in range(MOBA_TOPK):
        last = sel_ref[(b * MOBA_TOPK + slot) * N_HEADS + h] == nb - 1
        for r in range(2):
            kp = k_refs[slot * 2 + r][head_rows, :].astype(BF16)
            bias = jnp.where(last, bt[:, r * PAGE_SIZE:(r + 1) * PAGE_SIZE], far)
            s_piece.append(_dot_nt(q, kp) + bias)
    mx = functools.reduce(jnp.maximum, s_piece)
    m = jnp.maximum(jnp.max(mx, axis=1, keepdims=True), s_new)
    p_piece = [jnp.exp(s - m) for s in s_piece]
    p_new = jnp.exp(s_new - m)
    l = jnp.sum(functools.reduce(jnp.add, p_piece), axis=1, keepdims=True) + p_new
    acc = p_new.astype(BF16).astype(F32) * _new_token_values(vn_ref[...])
    for i in range(n):
        acc = acc + _dot(p_piece[i].astype(BF16), v_refs[i][head_rows, :].astype(BF16))
    o = acc / l

    @pl.when(h == 0)
    def _():
        o_ref[...] = jnp.zeros(o_ref.shape, F32)

    rowi = lax.broadcasted_iota(I32, o.shape, 0)
    o_ref[pl.ds(h, 1), :] = jnp.sum(jnp.where(rowi == h, o, 0.0), axis=0, keepdims=True)


def moba_attention_decode(q16, kn, vn, bias_tab, sel, cache_k, cache_v, page_table, layer):
    b = q16.shape[0]
    n_pages = page_table.shape[1]
    nb = n_pages * PAGE_SIZE // MOBA_BLOCK

    def page_spec(slot, r):
        def index(b, h, pt, sel):
            blk = sel[(b * MOBA_TOPK + slot) * N_HEADS + h]
            return (layer, pt[b * n_pages + 2 * blk + r], 0, 0)
        return pl.BlockSpec((None, None, PAGE_ROWS, HEAD_DIM), index)

    kv_specs = [page_spec(slot, r) for slot in range(MOBA_TOPK) for r in range(2)]
    seq = lambda rows, width: pl.BlockSpec((None, rows, width), lambda b, h, pt, sel: (b, 0, 0))
    grid_spec = pltpu.PrefetchScalarGridSpec(
        num_scalar_prefetch=2,
        grid=(b, N_HEADS),
        in_specs=[seq(HEAD_ROWS, HEAD_DIM), seq(1, KV_DIM), seq(1, KV_DIM),
                  pl.BlockSpec((HEAD_ROWS, 4 * PAGE_SIZE), lambda b, h, pt, sel: (0, 0))] + kv_specs + kv_specs,
        out_specs=seq(HEAD_ROWS, HEAD_DIM),
    )
    n = len(kv_specs)
    return pl.pallas_call(
        functools.partial(_moba_dec_kernel, nb=nb),
        grid_spec=grid_spec,
        out_shape=jax.ShapeDtypeStruct((b, HEAD_ROWS, HEAD_DIM), F32),
        compiler_params=_params("parallel", "arbitrary"),
        name="moba_attention_decode",
    )(page_table.reshape(-1), sel.reshape(-1), q16, kn, vn, bias_tab, *([cache_k] * n), *([cache_v] * n))


def _rel_bias_by_distance(rel_bias, n):
    dist = jnp.arange(n, dtype=I32)
    max_exact = N_BUCKETS // 2
    large = max_exact + (jnp.log(jnp.maximum(dist, 1).astype(F32) / max_exact)
                         / math.log(MAX_DISTANCE / max_exact)
                         * (N_BUCKETS - max_exact)).astype(I32)
    large = jnp.minimum(large, N_BUCKETS - 1)
    return rel_bias[jnp.where(dist < max_exact, dist, large)]


def _pad_lanes(w, b):
    n = w.shape[1]
    return jnp.pad(w, ((0, 0), (0, LANES - n))), jnp.pad(b, (0, LANES - n)).reshape(1, LANES)


def _trunk(x, mod, cache, rel_bias, weights):
    (norm_g, final_g, w_in_b, w_out_b, fgates, wr_hi, wr_lo, br, wg_b, wu_b, wd_b) = weights
    b, t, d = x.shape
    m = b * t
    depth = w_in_b.shape[0]
    paged = cache is not None
    if paged:
        assert t == 1
        cache_k, cache_v, cache_lft, page_table = cache
        n_pages = page_table.shape[1]
        nb_past = n_pages * PAGE_SIZE // MOBA_BLOCK
        assert (n_pages * PAGE_SIZE) % MOBA_BLOCK == 0 and nb_past >= MOBA_TOPK
        tm_tok = tm_moe = m
        bias_d = _rel_bias_by_distance(rel_bias, 2 * MOBA_BLOCK + 1)
        assert MAX_DISTANCE <= MOBA_BLOCK
        near = bias_d[MOBA_BLOCK - jnp.arange(MOBA_BLOCK)].T
        tab = jnp.concatenate([near, jnp.broadcast_to(bias_d[2 * MOBA_BLOCK][:, None], (N_HEADS, PAGE_SIZE)),
                               jnp.broadcast_to(bias_d[0][:, None], (N_HEADS, PAGE_SIZE))], axis=1)
        bias_tab = jnp.pad(tab, ((0, HEAD_ROWS - N_HEADS), (0, 0)))
    else:
        assert t % MOBA_BLOCK == 0 and t // MOBA_BLOCK <= LANES
        tm_tok = min(512, t)
        tm_moe = min(1024, t)
        bias_d = _rel_bias_by_distance(rel_bias, 2 * MOBA_BLOCK + 1)
        assert MAX_DISTANCE <= MOBA_BLOCK
        delta = jnp.arange(MOBA_BLOCK)[:, None] - jnp.arange(MOBA_BLOCK)[None, :]
        bias0 = jnp.moveaxis(bias_d[jnp.maximum(delta, 0)], 2, 0)
        bias1 = jnp.moveaxis(bias_d[delta + MOBA_BLOCK], 2, 0)
        far = jnp.broadcast_to(bias_d[2 * MOBA_BLOCK][:, None, None], (N_HEADS, 1, MOBA_BLOCK))

    def mod_part(l, j):
        part = mod[l, :, j * d:(j + 1) * d]
        return part if paged else part.reshape(b, 1, d)

    xf = x.reshape(m, d)
    new_k, new_v, new_logf = [], [], []
    for l in range(depth):
        kind = l % N_MIXERS
        sh1, sc1, gt1, sh2, sc2, gt2 = [mod_part(l, j) for j in range(6)]
        fg = fgates[l // N_MIXERS] if kind == 0 else None
        outs = norm_qkv(xf, norm_g[l, 0].reshape(1, d), sc1, sh1, w_in_b[l], fg,
                        tm=tm_tok, rows_per_seq=t)
        q, k, v, kb, vb = outs[:5]
        new_k.append(k.reshape(b, t, N_KV_HEADS, HEAD_DIM))
        new_v.append(v.reshape(b, t, N_KV_HEADS, HEAD_DIM))
        if kind == 0:
            lf = outs[5]
            new_logf.append(lf[:, :N_HEADS].reshape(b, t, N_HEADS))
        if paged:
            q16 = jnp.pad(q.reshape(b, N_HEADS, HEAD_DIM), ((0, 0), (0, HEAD_ROWS - N_HEADS), (0, 0)))
            kn = k.reshape(b, 1, KV_DIM)
            vn = v.reshape(b, 1, KV_DIM)
            if kind == 0:
                lfn = jnp.broadcast_to(
                    jnp.pad(lf[:, :N_HEADS], ((0, 0), (0, HEAD_ROWS - N_HEADS)))[:, :, None],
                    (b, HEAD_ROWS, PAGE_SIZE))
                o16 = fox_attention_decode(q16, kn, vn, lfn, cache_k, cache_v, cache_lft, page_table,
                                           l, l // N_MIXERS)
            elif kind == 1:
                q128 = jnp.pad(q.reshape(b, N_HEADS, HEAD_DIM), ((0, 0), (0, LANES - N_HEADS), (0, 0)))
                sel = moba_select_decode(q128, cache_k, page_table, l)
                sel = sel[:, :MOBA_TOPK, :N_HEADS]
                o16 = moba_attention_decode(q16, kn, vn, bias_tab, sel, cache_k, cache_v, page_table, l)
            else:
                o16 = sb_attention_decode(q16, cache_k, cache_v, page_table, l)
            o = o16[:, :N_HEADS, :].reshape(m, Q_DIM).astype(BF16)
        else:
            q3 = q.reshape(b, t, Q_DIM)
            kb3 = kb.reshape(b, t, KV_DIM)
            vb3 = vb.reshape(b, t, KV_DIM)
            if kind == 0:
                dsum, dsum_t = cumsum_time(lf.reshape(b, t, LANES))
                o = fox_attention_prompt(q3, kb3, vb3, dsum, dsum_t)
            elif kind == 1:
                kmean = block_mean(k.reshape(b, t, KV_DIM))
                o = moba_attention_prompt(q3, kb3, vb3, kmean, bias0, bias1, far)
            else:
                o = sb_attention_prompt(q3, kb3, vb3)
            o = o.reshape(m, Q_DIM)
        x1, h2, gates = out_proj_router(o, xf, w_out_b[l], gt1, norm_g[l, 1].reshape(1, d), sc2, sh2,
                                        wr_hi, wr_lo, br, tm=tm_tok, rows_per_seq=t)
        xf = moe_ffn(h2, gates, x1, gt2, wg_b[l], wu_b[l], wd_b[l], tm=tm_moe, rows_per_seq=t)
    y = final_norm(xf, final_g.reshape(1, d), tm=tm_tok).reshape(b, t, d)
    return y, jnp.stack(new_k), jnp.stack(new_v), jnp.stack(new_logf)


def kernel(x_prompt, x_sample, cache_k, cache_v, cache_logf, page_table, c_prompt, c_sample, rel_bias,
           w_ada, b_ada, norm_g, final_g, w_in, w_out, w_fgate, b_fgate, w_router, b_router,
           w_gate, w_up, w_down):
    n_prompt = c_prompt.shape[0]
    mod = ada_modulation(jnp.concatenate([c_prompt, c_sample], axis=0), w_ada, b_ada)

    fgates = []
    for a in range(w_fgate.shape[0]):
        wf, bf = _pad_lanes(w_fgate[a], b_fgate[a])
        fgates.append((wf.astype(BF16), bf))
    wr, br = _pad_lanes(w_router, b_router)
    wr_hi, wr_lo = _split2(wr)
    weights = (norm_g, final_g, w_in.astype(BF16), w_out.astype(BF16), fgates, wr_hi, wr_lo, br,
               w_gate.astype(BF16), w_up.astype(BF16), w_down.astype(BF16))

    depth, n_pool = cache_k.shape[:2]
    cache = (cache_k.reshape(depth, n_pool, PAGE_ROWS, HEAD_DIM),
             cache_v.reshape(depth, n_pool, PAGE_ROWS, HEAD_DIM),
             jnp.swapaxes(cache_logf, 2, 3), page_table)

    y_p, k_p, v_p, lf_p = _trunk(x_prompt, mod[:, :n_prompt], None, rel_bias, weights)
    y_s, k_s, v_s, lf_s = _trunk(x_sample, mod[:, n_prompt:], cache, rel_bias, weights)
    return (y_p, y_s, k_p, v_p, lf_p, k_s, v_s, lf_s)
```

```python
import functools
import math

import jax
import jax.numpy as jnp
from jax import lax
from jax.experimental import pallas as pl
from jax.experimental.pallas import tpu as pltpu

F32 = jnp.float32
BF16 = jnp.bfloat16
I32 = jnp.int32

N_MIXERS = 3
N_HEADS = 8
N_KV_HEADS = 4
GROUP = N_HEADS // N_KV_HEADS
HEAD_DIM = 128
Q_DIM = N_HEADS * HEAD_DIM
KV_DIM = N_KV_HEADS * HEAD_DIM
PAGE_SIZE = 128
MOBA_BLOCK = 256
MOBA_TOPK = 3
N_BUCKETS = 32
MAX_DISTANCE = 128
N_EXPERTS = 16
N_GROUPS = 4
EXPERTS_PER_GROUP = N_EXPERTS // N_GROUPS
TOP_K = 2
RMS_EPS = 1e-6

PAGE_ROWS = PAGE_SIZE * N_KV_HEADS
LANES = 128
HEAD_ROWS = 16
MASKED = -1e30
LOG2E = math.log2(math.e)
VMEM_LIMIT = 56 * 1024 * 1024


def _params(*sem):
    return pltpu.CompilerParams(dimension_semantics=sem, vmem_limit_bytes=VMEM_LIMIT)


def _dot(a, b):
    return jnp.dot(a, b, preferred_element_type=F32)


def _dot_nt(a, b):
    return lax.dot_general(a, b, (((1,), (1,)), ((), ())), preferred_element_type=F32)


def _split2(x):
    hi = x.astype(BF16)
    lo = (x - hi.astype(F32)).astype(BF16)
    return hi, lo


def _split3(x):
    hi = x.astype(BF16)
    r = x - hi.astype(F32)
    mid = r.astype(BF16)
    lo = (r - mid.astype(F32)).astype(BF16)
    return hi, mid, lo


def _neg_softplus(z):
    return -(jnp.maximum(z, 0.0) + jnp.log1p(jnp.exp(-jnp.abs(z))))


def _silu(x):
    return x / (1.0 + jnp.exp(-x))


def _ada_kernel(c_ref, w_ref, b_ref, o_ref):
    s = _silu(c_ref[...])
    o_ref[...] = _dot(s.astype(BF16), w_ref[...].astype(BF16)) + b_ref[...]


def ada_modulation(c_all, w_ada, b_ada):
    depth, d, n = w_ada.shape
    mc = c_all.shape[0]
    tn = 1024
    return pl.pallas_call(
        _ada_kernel,
        grid=(depth, n // tn),
        in_specs=[
            pl.BlockSpec((mc, d), lambda l, j: (0, 0)),
            pl.BlockSpec((None, d, tn), lambda l, j: (l, 0, j)),
            pl.BlockSpec((None, 1, tn), lambda l, j: (l, 0, j)),
        ],
        out_specs=pl.BlockSpec((None, mc, tn), lambda l, j: (l, 0, j)),
        out_shape=jax.ShapeDtypeStruct((depth, mc, n), F32),
        compiler_params=_params("parallel", "parallel"),
        name="ada_modulation",
    )(c_all, w_ada, b_ada.reshape(depth, 1, n))


def _rms_mod(x, g, sc, sh):
    r = lax.rsqrt(jnp.mean(x * x, axis=-1, keepdims=True) + RMS_EPS)
    return (x * r * g) * (1.0 + sc) + sh


def _qkv_kernel(*refs, has_fgate, q_scale):
    if has_fgate:
        (x_ref, g_ref, sc_ref, sh_ref, w_ref, wf_ref, bf_ref,
         q_ref, k_ref, v_ref, kb_ref, vb_ref, lf_ref) = refs
    else:
        x_ref, g_ref, sc_ref, sh_ref, w_ref, q_ref, k_ref, v_ref, kb_ref, vb_ref = refs
    hb = _rms_mod(x_ref[...], g_ref[...], sc_ref[...], sh_ref[...]).astype(BF16)
    qkv = _dot(hb, w_ref[...])
    q_ref[...] = (qkv[:, :Q_DIM] * q_scale).astype(BF16)
    k = qkv[:, Q_DIM:Q_DIM + KV_DIM]
    v = qkv[:, Q_DIM + KV_DIM:]
    tm = k.shape[0]
    for kvh in range(N_KV_HEADS):
        rows = pl.ds(kvh, tm, stride=N_KV_HEADS)
        k_ref[rows, :] = k[:, kvh * HEAD_DIM:(kvh + 1) * HEAD_DIM]
        v_ref[rows, :] = v[:, kvh * HEAD_DIM:(kvh + 1) * HEAD_DIM]
    kb_ref[...] = k.astype(BF16)
    vb_ref[...] = v.astype(BF16)
    if has_fgate:
        z = _dot(hb, wf_ref[...]) + bf_ref[...]
        lf_ref[...] = jnp.minimum(z, 0.0) - jnp.log1p(jnp.exp(-jnp.abs(z)))


def _mod_spec(mod, tm, rows_per_seq):
    d = mod.shape[-1]
    if mod.ndim == 3:
        tiles = rows_per_seq // tm
        return pl.BlockSpec((None, 1, d), lambda i: (i // tiles, 0, 0))
    return pl.BlockSpec((tm, d), lambda i: (i, 0))


def norm_qkv(x, g, sc, sh, w_in_b, fgate, *, tm, rows_per_seq, q_scale):
    m, d = x.shape
    n = w_in_b.shape[1]
    has_fgate = fgate is not None
    row = lambda i: (i, 0)
    const = lambda i: (0, 0)
    in_specs = [pl.BlockSpec((tm, d), row), pl.BlockSpec((1, d), const),
                _mod_spec(sc, tm, rows_per_seq), _mod_spec(sh, tm, rows_per_seq),
                pl.BlockSpec((d, n), const)]
    args = [x, g, sc, sh, w_in_b]
    out_specs = [pl.BlockSpec((tm, Q_DIM), row), pl.BlockSpec((tm * N_KV_HEADS, HEAD_DIM), row),
                 pl.BlockSpec((tm * N_KV_HEADS, HEAD_DIM), row), pl.BlockSpec((tm, KV_DIM), row),
                 pl.BlockSpec((tm, KV_DIM), row)]
    out_shape = [jax.ShapeDtypeStruct((m, Q_DIM), BF16),
                 jax.ShapeDtypeStruct((m * N_KV_HEADS, HEAD_DIM), F32),
                 jax.ShapeDtypeStruct((m * N_KV_HEADS, HEAD_DIM), F32),
                 jax.ShapeDtypeStruct((m, KV_DIM), BF16), jax.ShapeDtypeStruct((m, KV_DIM), BF16)]
    if has_fgate:
        wf, bf = fgate
        in_specs += [pl.BlockSpec((d, LANES), const), pl.BlockSpec((1, LANES), const)]
        args += [wf, bf]
        out_specs.append(pl.BlockSpec((tm, LANES), row))
        out_shape.append(jax.ShapeDtypeStruct((m, LANES), F32))
    return pl.pallas_call(
        functools.partial(_qkv_kernel, has_fgate=has_fgate, q_scale=q_scale),
        grid=(m // tm,),
        in_specs=in_specs, out_specs=out_specs, out_shape=out_shape,
        compiler_params=_params("parallel"),
        name="norm_qkv",
    )(*args)


def _cumsum_kernel(lf_ref, dt_ref, *, chunk):
    t = lf_ref.shape[0]
    row = lax.broadcasted_iota(I32, (chunk, chunk), 0)
    col = lax.broadcasted_iota(I32, (chunk, chunk), 1)
    tri = jnp.where(col <= row, 1.0, 0.0).astype(BF16)
    carry = jnp.zeros((1, LANES), F32)
    for c in range(t // chunk):
        sl = slice(c * chunk, (c + 1) * chunk)
        hi, mid, lo = _split3(lf_ref[sl, :])
        cs = _dot(tri, hi) + _dot(tri, mid) + _dot(tri, lo) + carry
        dt_ref[:, sl] = cs.T[:N_HEADS, :]
        carry = cs[chunk - 1:chunk, :]


def cumsum_time(lf):
    b, t, _ = lf.shape
    chunk = min(256, t)
    return pl.pallas_call(
        functools.partial(_cumsum_kernel, chunk=chunk),
        grid=(b,),
        in_specs=[pl.BlockSpec((None, t, LANES), lambda i: (i, 0, 0))],
        out_specs=pl.BlockSpec((None, N_HEADS, t), lambda i: (i, 0, 0)),
        out_shape=jax.ShapeDtypeStruct((b, N_HEADS, t), F32),
        compiler_params=_params("parallel"),
        name="cumsum_time",
    )(lf)


def _lane_pick(x, idx):
    lane = lax.broadcasted_iota(I32, x.shape, 1)
    return jnp.sum(jnp.where(lane == idx, x, 0.0), axis=1, keepdims=True)


def _with_ones(v):
    return jnp.concatenate([v, jnp.ones(v.shape, v.dtype)], axis=1)


def _softmax2_first(s, v1):
    m = jnp.max(s, axis=1, keepdims=True)
    return m, _dot(jnp.exp2(s - m).astype(BF16), v1)


def _softmax2_step(s, v1, m, accl, keep=None):
    m_new = jnp.maximum(m, jnp.max(s, axis=1, keepdims=True))
    new = accl * jnp.exp2(m - m_new) + _dot(jnp.exp2(s - m_new).astype(BF16), v1)
    if keep is None:
        return m_new, new
    return jnp.where(keep, m_new, m), jnp.where(keep, new, accl)


def _softmax2_out(accl):
    return (accl[:, :HEAD_DIM] / accl[:, HEAD_DIM:HEAD_DIM + 1]).astype(BF16)


def _fox_kernel(q_ref, k_ref, v_ref, dt_ref, o_ref, *, tq):
    kvh = pl.program_id(1)
    i = pl.program_id(2)
    q0 = pl.multiple_of(i * tq, tq)
    row = lax.broadcasted_iota(I32, (tq, tq), 0)
    col = lax.broadcasted_iota(I32, (tq, tq), 1)
    causal = col <= row
    heads = [kvh * GROUP + g for g in range(GROUP)]
    qs = [q_ref[:, g * HEAD_DIM:(g + 1) * HEAD_DIM] for g in range(GROUP)]
    d0s = [dt_ref[pl.ds(h, 1), pl.ds(q0, tq)][:, 0:1] for h in heads]

    def logits(g, start, kt):
        dk = dt_ref[pl.ds(heads[g], 1), pl.ds(start, tq)]
        return _dot_nt(qs[g], kt) + (d0s[g] - dk) * LOG2E

    def step(s, vt, m, l, acc):
        m_new = jnp.maximum(m, jnp.max(s, axis=1, keepdims=True))
        alpha = jnp.exp2(m - m_new)
        p = jnp.exp2(s - m_new)
        return m_new, l * alpha + jnp.sum(p, axis=1, keepdims=True), acc * alpha + _dot(p.astype(BF16), vt)

    kt, vt = k_ref[pl.ds(q0, tq), :], v_ref[pl.ds(q0, tq), :]
    zero = (jnp.full((tq, 1), MASKED, F32), jnp.zeros((tq, 1), F32), jnp.zeros((tq, HEAD_DIM), F32))
    state = tuple(step(jnp.where(causal, logits(g, q0, kt), MASKED), vt, *zero) for g in range(GROUP))

    def body(j, state):
        start = pl.multiple_of(j * tq, tq)
        kt, vt = k_ref[pl.ds(start, tq), :], v_ref[pl.ds(start, tq), :]
        return tuple(step(logits(g, start, kt), vt, *state[g]) for g in range(GROUP))

    state = lax.fori_loop(0, i, body, state)
    for g, (_, l, acc) in enumerate(state):
        o_ref[:, g * HEAD_DIM:(g + 1) * HEAD_DIM] = (acc / l).astype(BF16)


def _attn_specs(t, tq):
    qo = pl.BlockSpec((None, tq, GROUP * HEAD_DIM), lambda b, kvh, i: (b, i, kvh))
    kv = pl.BlockSpec((None, t, HEAD_DIM), lambda b, kvh, i: (b, 0, kvh))
    return qo, kv


def fox_attention_prompt(q, kb, vb, dt):
    b, t, _ = q.shape
    tq = min(512, t)
    qo, kv = _attn_specs(t, tq)
    return pl.pallas_call(
        functools.partial(_fox_kernel, tq=tq),
        grid=(b, N_KV_HEADS, t // tq),
        in_specs=[qo, kv, kv, pl.BlockSpec((None, N_HEADS, t), lambda b, kvh, i: (b, 0, 0))],
        out_specs=qo,
        out_shape=jax.ShapeDtypeStruct((b, t, Q_DIM), BF16),
        compiler_params=_params("parallel", "parallel", "parallel"),
        name="fox_attention_prompt",
    )(q, kb, vb, dt)


def _suffix_matrices(n):
    row = lax.broadcasted_iota(I32, (n, n), 0)
    col = lax.broadcasted_iota(I32, (n, n), 1)
    return jnp.where(row > col, 1.0, 0.0).astype(BF16), jnp.ones((n, n), BF16)


def _sb_kernel(q_ref, k_ref, v_ref, o_ref, *, tq, tk):
    i = pl.program_id(2)
    q0 = pl.multiple_of(i * tq, tq)
    upper, _ = _suffix_matrices(tk)
    row = lax.broadcasted_iota(I32, (tq, tk), 0)
    col = lax.broadcasted_iota(I32, (tq, tk), 1)
    qs = [q_ref[:, g * HEAD_DIM:(g + 1) * HEAD_DIM] for g in range(GROUP)]

    def chunk(start, state, mask):
        kt, vt = k_ref[pl.ds(start, tk), :], v_ref[pl.ds(start, tk), :]
        out = []
        for g in range(GROUP):
            c, acc = state[g]
            z = _dot_nt(qs[g], kt)
            ls = jnp.minimum(z, 0.0) - jnp.log2(1.0 + jnp.exp2(jnp.minimum(z, -z)))
            lk = ls - z
            if mask is not None:
                lk = jnp.where(mask, lk, 0.0)
            hi, lo = _split2(lk)
            after = _dot(hi, upper) + _dot(lo, upper) + c
            a = jnp.exp2(ls + after)
            if mask is not None:
                a = jnp.where(mask, a, 0.0)
            out.append((c + jnp.sum(lk, axis=1, keepdims=True), acc + _dot(a.astype(BF16), vt)))
        return tuple(out)

    state = tuple((jnp.zeros((tq, 1), F32), jnp.zeros((tq, HEAD_DIM), F32)) for _ in range(GROUP))
    for mth in reversed(range(tq // tk)):
        start = pl.multiple_of(q0 + mth * tk, tk)
        state = chunk(start, state, (col + mth * tk) < row)

    n_past = i * (tq // tk)

    def body(it, state):
        return chunk(pl.multiple_of((n_past - 1 - it) * tk, tk), state, None)

    state = lax.fori_loop(0, n_past, body, state)
    for g, (_, acc) in enumerate(state):
        o_ref[:, g * HEAD_DIM:(g + 1) * HEAD_DIM] = acc.astype(BF16)


def sb_attention_prompt(q, kb, vb):
    b, t, _ = q.shape
    tq = min(512, t)
    tk = min(256, t)
    qo, kv = _attn_specs(t, tq)
    return pl.pallas_call(
        functools.partial(_sb_kernel, tq=tq, tk=tk),
        grid=(b, N_KV_HEADS, t // tq),
        in_specs=[qo, kv, kv],
        out_specs=qo,
        out_shape=jax.ShapeDtypeStruct((b, t, Q_DIM), BF16),
        compiler_params=_params("parallel", "parallel", "parallel"),
        name="sb_attention_prompt",
    )(q, kb, vb)


def _block_mean_kernel(k_ref, o_ref, *, nb):
    t = k_ref.shape[0] // N_KV_HEADS
    o_ref[...] = jnp.zeros(o_ref.shape, F32)
    for kvh in range(N_KV_HEADS):
        x = k_ref[pl.ds(kvh, t, stride=N_KV_HEADS), :]
        o_ref[kvh, 0:nb, :] = jnp.sum(x.reshape(nb, t // nb, HEAD_DIM), axis=1) * (1.0 / (t // nb))


def block_mean(k4):
    b, rows, _ = k4.shape
    nb = rows // N_KV_HEADS // MOBA_BLOCK
    return pl.pallas_call(
        functools.partial(_block_mean_kernel, nb=nb),
        grid=(b,),
        in_specs=[pl.BlockSpec((None, rows, HEAD_DIM), lambda i: (i, 0, 0))],
        out_specs=pl.BlockSpec((None, N_KV_HEADS, LANES, HEAD_DIM), lambda i: (i, 0, 0, 0)),
        out_shape=jax.ShapeDtypeStruct((b, N_KV_HEADS, LANES, HEAD_DIM), F32),
        compiler_params=_params("parallel"),
        name="block_mean",
    )(k4)


def _top_mask(gate, n_valid, n_cand, axis):
    idx = lax.broadcasted_iota(I32, gate.shape, axis)
    cnt = jnp.zeros(gate.shape, I32)
    for jp in range(n_cand):
        other = gate[:, jp:jp + 1] if axis == 1 else gate[jp:jp + 1, :]
        beats = (other > gate) | ((other == gate) & (jp < idx))
        cnt = cnt + jnp.where(beats, jnp.where(jp < n_valid, 1, 0), 0)
    return (idx < n_valid) & (cnt < MOBA_TOPK)


def _moba_kernel(q_ref, k_ref, v_ref, km_ref, b0_ref, b1_ref, o_ref, *, tq, nb):
    i = pl.program_id(2)
    q0 = pl.multiple_of(i * tq, tq)
    row = lax.broadcasted_iota(I32, (tq, tq), 0)
    col = lax.broadcasted_iota(I32, (tq, tq), 1)
    causal = col <= row
    km_hi, km_lo = _split2(km_ref[...])
    qs = [q_ref[:, g * HEAD_DIM:(g + 1) * HEAD_DIM] for g in range(GROUP)]
    sels = [jnp.where(_top_mask(_dot_nt(qh, km_hi) + _dot_nt(qh, km_lo), i, nb, 1), 1.0, 0.0) for qh in qs]

    def tile(j):
        start = pl.multiple_of(j * tq, tq)
        return k_ref[pl.ds(start, tq), :], _with_ones(v_ref[pl.ds(start, tq), :])

    def picked(g, j):
        return _lane_pick(sels[g], j) > 0.0

    kt, v1 = tile(i)
    state = tuple(_softmax2_first(jnp.where(causal, _dot_nt(qs[g], kt) + b0_ref[g], MASKED), v1)
                  for g in range(GROUP))
    prev = jnp.maximum(i - 1, 0)
    kt, v1 = tile(prev)
    state = tuple(_softmax2_step(_dot_nt(qs[g], kt) + b1_ref[g], v1, *state[g], keep=picked(g, i - 1))
                  for g in range(GROUP))

    def body(j, state):
        kt, v1 = tile(j)
        return tuple(_softmax2_step(_dot_nt(qs[g], kt), v1, *state[g], keep=picked(g, j))
                     for g in range(GROUP))

    state = lax.fori_loop(0, i - 1, body, state)
    for g, (_, accl) in enumerate(state):
        o_ref[:, g * HEAD_DIM:(g + 1) * HEAD_DIM] = _softmax2_out(accl)


def _bias_tile_kernel(rb_ref, b0_ref, b1_ref, *, n):
    h = pl.program_id(0)
    row = lax.broadcasted_iota(I32, (n, n), 0)
    col = lax.broadcasted_iota(I32, (n, n), 1)
    max_exact = N_BUCKETS // 2
    for out_ref, shift in ((b0_ref, 0), (b1_ref, n)):
        dist = jnp.maximum(row - col + shift, 0)
        large = max_exact + (jnp.log(jnp.maximum(dist, 1).astype(F32) / max_exact)
                             / math.log(MAX_DISTANCE / max_exact) * (N_BUCKETS - max_exact)).astype(I32)
        bucket = jnp.where(dist < max_exact, dist, jnp.minimum(large, N_BUCKETS - 1))
        acc = jnp.zeros((n, n), F32)
        for bk in range(N_BUCKETS):
            acc = jnp.where(bucket == bk, rb_ref[bk, h], acc)
        out_ref[...] = (acc - rb_ref[N_BUCKETS - 1, h]) * LOG2E


def rel_bias_tiles(rel_bias):
    n = MOBA_BLOCK
    shape = jax.ShapeDtypeStruct((N_HEADS, n, n), F32)
    spec = pl.BlockSpec((None, n, n), lambda h: (h, 0, 0))
    return pl.pallas_call(
        functools.partial(_bias_tile_kernel, n=n),
        grid=(N_HEADS,),
        in_specs=[pl.BlockSpec(memory_space=pltpu.SMEM)],
        out_specs=[spec, spec],
        out_shape=[shape, shape],
        compiler_params=_params("parallel"),
        name="rel_bias_tiles",
    )(rel_bias)


def moba_attention_prompt(q, kb, vb, kmean, bias0, bias1):
    b, t, _ = q.shape
    tq = MOBA_BLOCK
    nb = t // tq
    qo, kv = _attn_specs(t, tq)
    head_pair = lambda b, kvh, i: (kvh, 0, 0)
    return pl.pallas_call(
        functools.partial(_moba_kernel, tq=tq, nb=nb),
        grid=(b, N_KV_HEADS, nb),
        in_specs=[qo, kv, kv,
                  pl.BlockSpec((None, None, LANES, HEAD_DIM), lambda b, kvh, i: (b, kvh, 0, 0)),
                  pl.BlockSpec((GROUP, tq, tq), head_pair),
                  pl.BlockSpec((GROUP, tq, tq), head_pair)],
        out_specs=qo,
        out_shape=jax.ShapeDtypeStruct((b, t, Q_DIM), BF16),
        compiler_params=_params("parallel", "parallel", "parallel"),
        name="moba_attention_prompt",
    )(q, kb, vb, kmean, bias0, bias1)


def _route(logits):
    idx = lax.broadcasted_iota(I32, logits.shape, 1)
    valid = idx < N_EXPERTS
    lg = jnp.where(valid, logits, MASKED)
    e = jnp.where(valid, jnp.exp(lg - jnp.max(lg, axis=1, keepdims=True)), 0.0)
    grp = idx >> 2

    def peers(x, shifts):
        for s in shifts:
            for sh in (s, LANES - s):
                oi = pltpu.roll(idx, sh, 1)
                yield pltpu.roll(x, sh, 1), oi, oi < N_EXPERTS

    cnt = jnp.zeros(logits.shape, I32)
    for oe, oi, ok in peers(e, (1, 2, 3)):
        beats = ok & ((oi >> 2) == grp) & ((oe > e) | ((oe == e) & (oi < idx)))
        cnt = cnt + jnp.where(beats, 1, 0)
    top2 = valid & (cnt < TOP_K)
    t2e = jnp.where(top2, e, 0.0)
    score = t2e
    for ot, oi, ok in peers(t2e, (1, 2, 3)):
        score = score + jnp.where(ok & ((oi >> 2) == grp), ot, 0.0)
    lost = jnp.zeros(logits.shape, I32)
    for osc, oi, ok in peers(score, (4, 8, 12)):
        beats = ok & ((osc > score) | ((osc == score) & ((oi >> 2) < grp)))
        lost = lost + jnp.where(beats, 1, 0)
    return jnp.where(top2 & (lost == 0), e / score, 0.0)


def _oproj_kernel(o_ref, x_ref, w_ref, gt_ref, g_ref, sc_ref, sh_ref, wrh_ref, wrl_ref, br_ref,
                  x1_ref, h2_ref, gates_ref):
    x1 = x_ref[...] + gt_ref[...] * _dot(o_ref[...], w_ref[...])
    x1_ref[...] = x1
    h2 = _rms_mod(x1, g_ref[...], sc_ref[...], sh_ref[...])
    hi, lo = _split2(h2)
    h2_ref[...] = hi
    logits = _dot(hi, wrh_ref[...]) + _dot(lo, wrh_ref[...]) + _dot(hi, wrl_ref[...]) + br_ref[...]
    gates_ref[...] = _route(logits)


def out_proj_router(o, x, w_out_b, gt, g2, sc, sh, wr_hi, wr_lo, br, *, tm, rows_per_seq):
    m, d = x.shape
    row = lambda i: (i, 0)
    const = lambda i: (0, 0)
    ms = lambda a: _mod_spec(a, tm, rows_per_seq)
    return pl.pallas_call(
        _oproj_kernel,
        grid=(m // tm,),
        in_specs=[pl.BlockSpec((tm, Q_DIM), row), pl.BlockSpec((tm, d), row),
                  pl.BlockSpec((Q_DIM, d), const), ms(gt), pl.BlockSpec((1, d), const),
                  ms(sc), ms(sh), pl.BlockSpec((d, LANES), const),
                  pl.BlockSpec((d, LANES), const), pl.BlockSpec((1, LANES), const)],
        out_specs=[pl.BlockSpec((tm, d), row), pl.BlockSpec((tm, d), row),
                   pl.BlockSpec((tm, LANES), row)],
        out_shape=[jax.ShapeDtypeStruct((m, d), F32), jax.ShapeDtypeStruct((m, d), BF16),
                   jax.ShapeDtypeStruct((m, LANES), F32)],
        compiler_params=_params("parallel"),
        name="out_proj_router",
    )(o, x, w_out_b, gt, g2, sc, sh, wr_hi, wr_lo, br)


def _moe_kernel(h_ref, gates_ref, x_ref, gt_ref, wg_ref, wu_ref, wd_ref, *rest, epc, final):
    fg_ref, o_ref, acc_ref = rest if final else (None, *rest)
    c = pl.program_id(1)

    @pl.when(c == 0)
    def _():
        acc_ref[...] = jnp.zeros(acc_ref.shape, F32)

    h = h_ref[...]
    gates = gates_ref[...]
    for j in range(epc):
        gcol = _lane_pick(gates, c * epc + j)
        act = _silu(_dot(h, wg_ref[j])) * _dot(h, wu_ref[j]) * gcol
        acc_ref[...] += _dot(act.astype(BF16), wd_ref[j])

    @pl.when(c == pl.num_programs(1) - 1)
    def _():
        x = x_ref[...] + gt_ref[...] * acc_ref[...]
        if fg_ref is not None:
            x = x * lax.rsqrt(jnp.mean(x * x, axis=-1, keepdims=True) + RMS_EPS) * fg_ref[...]
        o_ref[...] = x


def moe_ffn(h2, gates, x1, gt, wg_b, wu_b, wd_b, final_g, *, tm, rows_per_seq):
    m, d = x1.shape
    n_e, _, de = wg_b.shape
    epc = 2
    row = lambda i, c: (i, 0)
    gt_spec = _mod_spec(gt, tm, rows_per_seq)
    gt_spec = pl.BlockSpec(gt_spec.block_shape, lambda i, c, f=gt_spec.index_map: f(i))
    in_specs = [pl.BlockSpec((tm, d), row), pl.BlockSpec((tm, LANES), row),
                pl.BlockSpec((tm, d), row), gt_spec,
                pl.BlockSpec((epc, d, de), lambda i, c: (c, 0, 0)),
                pl.BlockSpec((epc, d, de), lambda i, c: (c, 0, 0)),
                pl.BlockSpec((epc, de, d), lambda i, c: (c, 0, 0))]
    args = [h2, gates, x1, gt, wg_b, wu_b, wd_b]
    if final_g is not None:
        in_specs.append(pl.BlockSpec((1, d), lambda i, c: (0, 0)))
        args.append(final_g)
    return pl.pallas_call(
        functools.partial(_moe_kernel, epc=epc, final=final_g is not None),
        grid=(m // tm, n_e // epc),
        in_specs=in_specs,
        out_specs=pl.BlockSpec((tm, d), row),
        out_shape=jax.ShapeDtypeStruct((m, d), F32),
        scratch_shapes=[pltpu.VMEM((tm, d), F32)],
        compiler_params=_params("parallel", "arbitrary"),
        name="moe_ffn",
    )(*args)


def _head_rows(mats):
    rowi = lax.broadcasted_iota(I32, mats[0].shape, 0)
    out = mats[0]
    for kvh in range(1, N_KV_HEADS):
        out = jnp.where((rowi >> 1) == kvh, mats[kvh], out)
    return out


def _kv_rows(page_ref):
    return [page_ref[pl.ds(kvh, PAGE_SIZE, stride=N_KV_HEADS), :] for kvh in range(N_KV_HEADS)]


def _new_token_logits(q, kn):
    qf = q.astype(F32)
    prods = [qf * kn[kvh:kvh + 1, :].astype(BF16).astype(F32) for kvh in range(N_KV_HEADS)]
    return jnp.sum(_head_rows(prods), axis=1, keepdims=True)


def _new_token_values(vn):
    return _head_rows([jnp.broadcast_to(vn[kvh:kvh + 1, :].astype(BF16).astype(F32), (HEAD_ROWS, HEAD_DIM))
                       for kvh in range(N_KV_HEADS)])


def _page_spec(layer, n_pages, page_of):
    return pl.BlockSpec((None, None, PAGE_ROWS, HEAD_DIM),
                        lambda b, c, pt: (layer, pt[b * n_pages + page_of(c)], 0, 0))


def _seq_spec(rows, width):
    return pl.BlockSpec((None, rows, width), lambda b, c, pt: (b, 0, 0))


def _suffix_total_matrix(n):
    upper, ones = _suffix_matrices(n)
    return jnp.concatenate([upper, ones], axis=1)


def _suffix_and_total(x, suffix_total):
    hi, lo = _split2(x)
    both = _dot(hi, suffix_total) + _dot(lo, suffix_total)
    return both[:, :PAGE_SIZE], both[:, PAGE_SIZE:]


def _page_logits(q, k_refs):
    return [_head_rows([_dot_nt(q, ks.astype(BF16)) for ks in _kv_rows(ref)]) for ref in k_refs]


def _page_values(w_pages, v_refs):
    pv = [jnp.zeros((HEAD_ROWS, HEAD_DIM), F32)] * N_KV_HEADS
    for w, ref in zip(w_pages, v_refs):
        wb = w.astype(BF16)
        pv = [a + _dot(wb, vs.astype(BF16)) for a, vs in zip(pv, _kv_rows(ref))]
    return _head_rows(pv)


def _carry_back(car, sufs, tots):
    after = [None] * len(sufs)
    for r in reversed(range(len(sufs))):
        after[r] = car + sufs[r]
        car = car + tots[r]
    return after, car


def _fox_dec_kernel(pt_ref, q_ref, kn_ref, vn_ref, lfn_ref, *refs, pages):
    k_refs = refs[:pages]
    v_refs = refs[pages:2 * pages]
    lf_refs = refs[2 * pages:3 * pages]
    o_ref, m_s, l_s, acc_s, car_s = refs[3 * pages:]
    c = pl.program_id(1)
    q = q_ref[...]
    suffix_total = _suffix_total_matrix(PAGE_SIZE)

    @pl.when(c == 0)
    def _():
        m_s[...] = jnp.broadcast_to(_new_token_logits(q, kn_ref[...]), m_s.shape)
        l_s[...] = jnp.ones(l_s.shape, F32)
        acc_s[...] = _new_token_values(vn_ref[...])
        car_s[...] = lfn_ref[...]

    pad = jnp.zeros((HEAD_ROWS - N_HEADS, PAGE_SIZE), F32)
    st = [_suffix_and_total(jnp.concatenate([ref[...], pad], axis=0), suffix_total) for ref in lf_refs]
    qk = _page_logits(q, k_refs)
    decay, car_s[...] = _carry_back(car_s[...], [x[0] for x in st], [x[1] for x in st])
    s_pages = [a + b for a, b in zip(qk, decay)]

    m_old = m_s[...]
    mx = functools.reduce(jnp.maximum, s_pages)
    m_new = jnp.maximum(m_old, jnp.max(mx, axis=1, keepdims=True))
    alpha = jnp.exp(m_old - m_new)
    p_pages = [jnp.exp(s - m_new) for s in s_pages]
    l_s[...] = l_s[...] * alpha + jnp.sum(functools.reduce(jnp.add, p_pages), axis=1, keepdims=True)
    acc_s[...] = acc_s[...] * alpha + _page_values(p_pages, v_refs)
    m_s[...] = m_new

    @pl.when(c == pl.num_programs(1) - 1)
    def _():
        o_ref[...] = acc_s[...] / l_s[...]


def fox_attention_decode(q16, kn, vn, lfn, cache_k, cache_v, cache_lft, page_table, layer, fox_layer):
    b = q16.shape[0]
    n_pages = page_table.shape[1]
    pages = min(8, n_pages)
    nc = n_pages // pages
    page = lambda r: (lambda c: (nc - 1 - c) * pages + r)
    kv_specs = [_page_spec(layer, n_pages, page(r)) for r in range(pages)]
    lf_specs = [pl.BlockSpec((None, None, N_HEADS, PAGE_SIZE),
                             lambda b, c, pt, f=page(r): (fox_layer, pt[b * n_pages + f(c)], 0, 0))
                for r in range(pages)]
    grid_spec = pltpu.PrefetchScalarGridSpec(
        num_scalar_prefetch=1,
        grid=(b, nc),
        in_specs=[_seq_spec(HEAD_ROWS, HEAD_DIM), _seq_spec(N_KV_HEADS, HEAD_DIM),
                  _seq_spec(N_KV_HEADS, HEAD_DIM), _seq_spec(HEAD_ROWS, PAGE_SIZE)]
        + kv_specs + kv_specs + lf_specs,
        out_specs=_seq_spec(HEAD_ROWS, HEAD_DIM),
        scratch_shapes=[pltpu.VMEM((HEAD_ROWS, LANES), F32)] * 4,
    )
    return pl.pallas_call(
        functools.partial(_fox_dec_kernel, pages=pages),
        grid_spec=grid_spec,
        out_shape=jax.ShapeDtypeStruct((b, HEAD_ROWS, HEAD_DIM), F32),
        compiler_params=_params("parallel", "arbitrary"),
        name="fox_attention_decode",
    )(page_table.reshape(-1), q16, kn, vn, lfn, *([cache_k] * pages), *([cache_v] * pages),
      *([cache_lft] * pages))


def _sb_dec_kernel(pt_ref, q_ref, *refs, pages):
    k_refs = refs[:pages]
    v_refs = refs[pages:2 * pages]
    o_ref, acc_s, car_s = refs[2 * pages:]
    c = pl.program_id(1)
    q = q_ref[...]
    suffix_total = _suffix_total_matrix(PAGE_SIZE)

    @pl.when(c == 0)
    def _():
        acc_s[...] = jnp.zeros(acc_s.shape, F32)
        car_s[...] = jnp.zeros(car_s.shape, F32)

    zs = _page_logits(q, k_refs)
    lks = [_neg_softplus(z) for z in zs]
    st = [_suffix_and_total(lk, suffix_total) for lk in lks]
    after, car_s[...] = _carry_back(car_s[...], [x[0] for x in st], [x[1] for x in st])
    a_pages = [jnp.exp(z + lk + af) for z, lk, af in zip(zs, lks, after)]
    acc_s[...] += _page_values(a_pages, v_refs)

    @pl.when(c == pl.num_programs(1) - 1)
    def _():
        o_ref[...] = acc_s[...]


def sb_attention_decode(q16, cache_k, cache_v, page_table, layer):
    b = q16.shape[0]
    n_pages = page_table.shape[1]
    pages = min(8, n_pages)
    nc = n_pages // pages
    kv_specs = [_page_spec(layer, n_pages, lambda c, r=r: (nc - 1 - c) * pages + r)
                for r in range(pages)]
    grid_spec = pltpu.PrefetchScalarGridSpec(
        num_scalar_prefetch=1,
        grid=(b, nc),
        in_specs=[_seq_spec(HEAD_ROWS, HEAD_DIM)] + kv_specs + kv_specs,
        out_specs=_seq_spec(HEAD_ROWS, HEAD_DIM),
        scratch_shapes=[pltpu.VMEM((HEAD_ROWS, LANES), F32)] * 2,
    )
    return pl.pallas_call(
        functools.partial(_sb_dec_kernel, pages=pages),
        grid_spec=grid_spec,
        out_shape=jax.ShapeDtypeStruct((b, HEAD_ROWS, HEAD_DIM), F32),
        compiler_params=_params("parallel", "arbitrary"),
        name="sb_attention_decode",
    )(page_table.reshape(-1), q16, *([cache_k] * pages), *([cache_v] * pages))


def _page_head_sums(page_ref):
    x = page_ref[...]
    s8 = jnp.sum(x.reshape(PAGE_ROWS // 8, 8, HEAD_DIM), axis=0)
    return s8[:N_KV_HEADS] + s8[N_KV_HEADS:]


def _moba_sel_kernel(pt_ref, q_ref, *refs, pages, nb):
    k_refs = refs[:pages]
    km_ref, sel_ref = refs[pages:]
    c = pl.program_id(1)
    bps = pages // 2
    sums = [_page_head_sums(k_refs[r]) for r in range(pages)]
    means = [(sums[2 * j] + sums[2 * j + 1]) * (1.0 / MOBA_BLOCK) for j in range(bps)]
    row0 = pl.multiple_of(c * bps, bps)
    for kvh in range(N_KV_HEADS):
        km_ref[kvh, pl.ds(row0, bps), :] = jnp.concatenate([mj[kvh:kvh + 1] for mj in means], axis=0)

    @pl.when(c == pl.num_programs(1) - 1)
    def _():
        q = q_ref[...]
        lane = lax.broadcasted_iota(I32, (nb, LANES), 1)
        gate = jnp.zeros((nb, LANES), F32)
        for kvh in range(N_KV_HEADS):
            hi, lo = _split2(km_ref[kvh])
            gate = jnp.where((lane >> 1) == kvh, _dot_nt(hi, q) + _dot_nt(lo, q), gate)
        sel = _top_mask(gate, nb, nb, 0)
        blk = lax.broadcasted_iota(I32, (nb, LANES), 0).astype(F32)
        rows = []
        for _ in range(MOBA_TOPK):
            first = jnp.min(jnp.where(sel, blk, float(nb)), axis=0, keepdims=True)
            rows.append(first.astype(I32))
            sel = sel & (blk != first)
        rows.append(jnp.zeros((8 - MOBA_TOPK, LANES), I32))
        sel_ref[...] = jnp.concatenate(rows, axis=0)


def moba_select_decode(q128, cache_k, page_table, layer):
    b = q128.shape[0]
    n_pages = page_table.shape[1]
    nb = n_pages * PAGE_SIZE // MOBA_BLOCK
    pages = min(16, n_pages)
    nc = n_pages // pages
    k_specs = [_page_spec(layer, n_pages, lambda c, r=r: c * pages + r) for r in range(pages)]
    grid_spec = pltpu.PrefetchScalarGridSpec(
        num_scalar_prefetch=1,
        grid=(b, nc),
        in_specs=[_seq_spec(LANES, HEAD_DIM)] + k_specs,
        out_specs=[pl.BlockSpec((None, N_KV_HEADS, nb, HEAD_DIM), lambda b, c, pt: (b, 0, 0, 0)),
                   _seq_spec(8, LANES)],
    )
    _, sel = pl.pallas_call(
        functools.partial(_moba_sel_kernel, pages=pages, nb=nb),
        grid_spec=grid_spec,
        out_shape=[jax.ShapeDtypeStruct((b, N_KV_HEADS, nb, HEAD_DIM), F32),
                   jax.ShapeDtypeStruct((b, 8, LANES), I32)],
        compiler_params=_params("parallel", "arbitrary"),
        name="moba_select_decode",
    )(page_table.reshape(-1), q128, *([cache_k] * pages))
    return sel


def _moba_dec_kernel(pt_ref, sel_ref, q_ref, kn_ref, vn_ref, bt_ref, *refs, nb):
    n = MOBA_TOPK * 2
    k_refs = refs[:n]
    v_refs = refs[n:2 * n]
    o_ref = refs[2 * n]
    b = pl.program_id(0)
    h = pl.program_id(1)
    head_rows = pl.ds(h // GROUP, PAGE_SIZE, stride=N_KV_HEADS)
    q = q_ref[...]
    bt = bt_ref[...]
    far = bt[:, 2 * PAGE_SIZE:3 * PAGE_SIZE]
    s_new = _new_token_logits(q, kn_ref[...]) + bt[:, 3 * PAGE_SIZE:3 * PAGE_SIZE + 1]

    s_piece = []
    for slot in range(MOBA_TOPK):
        last = sel_ref[(b * MOBA_TOPK + slot) * N_HEADS + h] == nb - 1
        for r in range(2):
            kp = k_refs[slot * 2 + r][head_rows, :].astype(BF16)
            bias = jnp.where(last, bt[:, r * PAGE_SIZE:(r + 1) * PAGE_SIZE], far)
            s_piece.append(_dot_nt(q, kp) + bias)
    mx = functools.reduce(jnp.maximum, s_piece)
    m = jnp.maximum(jnp.max(mx, axis=1, keepdims=True), s_new)
    p_piece = [jnp.exp(s - m) for s in s_piece]
    p_new = jnp.exp(s_new - m)
    l = jnp.sum(functools.reduce(jnp.add, p_piece), axis=1, keepdims=True) + p_new
    acc = p_new.astype(BF16).astype(F32) * _new_token_values(vn_ref[...])
    for i in range(n):
        acc = acc + _dot(p_piece[i].astype(BF16), v_refs[i][head_rows, :].astype(BF16))
    o = acc / l

    @pl.when(h == 0)
    def _():
        o_ref[...] = jnp.zeros(o_ref.shape, F32)

    rowi = lax.broadcasted_iota(I32, o.shape, 0)
    o_ref[pl.ds(h, 1), :] = jnp.sum(jnp.where(rowi == h, o, 0.0), axis=0, keepdims=True)


def moba_attention_decode(q16, kn, vn, bias_tab, sel, cache_k, cache_v, page_table, layer):
    b = q16.shape[0]
    n_pages = page_table.shape[1]
    nb = n_pages * PAGE_SIZE // MOBA_BLOCK

    def page_spec(slot, r):
        def index(b, h, pt, sel):
            blk = sel[(b * MOBA_TOPK + slot) * N_HEADS + h]
            return (layer, pt[b * n_pages + 2 * blk + r], 0, 0)
        return pl.BlockSpec((None, None, PAGE_ROWS, HEAD_DIM), index)

    kv_specs = [page_spec(slot, r) for slot in range(MOBA_TOPK) for r in range(2)]
    seq = lambda rows, width: pl.BlockSpec((None, rows, width), lambda b, h, pt, sel: (b, 0, 0))
    grid_spec = pltpu.PrefetchScalarGridSpec(
        num_scalar_prefetch=2,
        grid=(b, N_HEADS),
        in_specs=[seq(HEAD_ROWS, HEAD_DIM), seq(N_KV_HEADS, HEAD_DIM), seq(N_KV_HEADS, HEAD_DIM),
                  pl.BlockSpec((HEAD_ROWS, 4 * PAGE_SIZE), lambda b, h, pt, sel: (0, 0))] + kv_specs + kv_specs,
        out_specs=seq(HEAD_ROWS, HEAD_DIM),
    )
    n = len(kv_specs)
    return pl.pallas_call(
        functools.partial(_moba_dec_kernel, nb=nb),
        grid_spec=grid_spec,
        out_shape=jax.ShapeDtypeStruct((b, HEAD_ROWS, HEAD_DIM), F32),
        compiler_params=_params("parallel", "arbitrary"),
        name="moba_attention_decode",
    )(page_table.reshape(-1), sel.reshape(-1), q16, kn, vn, bias_tab, *([cache_k] * n), *([cache_v] * n))


def _rel_bias_by_distance(rel_bias, n):
    dist = jnp.arange(n, dtype=I32)
    max_exact = N_BUCKETS // 2
    large = max_exact + (jnp.log(jnp.maximum(dist, 1).astype(F32) / max_exact)
                         / math.log(MAX_DISTANCE / max_exact)
                         * (N_BUCKETS - max_exact)).astype(I32)
    large = jnp.minimum(large, N_BUCKETS - 1)
    return rel_bias[jnp.where(dist < max_exact, dist, large)]


def _pad_lanes(w, b):
    n = w.shape[1]
    return jnp.pad(w, ((0, 0), (0, LANES - n))), jnp.pad(b, (0, LANES - n)).reshape(1, LANES)


def _trunk(x, mod, cache, rel_bias, weights):
    (norm_g, final_g, w_in_b, w_out_b, fgates, wr_hi, wr_lo, br, wg_b, wu_b, wd_b) = weights
    b, t, d = x.shape
    m = b * t
    depth = w_in_b.shape[0]
    paged = cache is not None
    if paged:
        assert t == 1
        cache_k, cache_v, cache_lft, page_table = cache
        n_pages = page_table.shape[1]
        nb_past = n_pages * PAGE_SIZE // MOBA_BLOCK
        assert (n_pages * PAGE_SIZE) % MOBA_BLOCK == 0 and nb_past >= MOBA_TOPK
        tm_tok = tm_moe = m
        bias_d = _rel_bias_by_distance(rel_bias, 2 * MOBA_BLOCK + 1)
        assert MAX_DISTANCE <= MOBA_BLOCK
        near = bias_d[MOBA_BLOCK - jnp.arange(MOBA_BLOCK)].T
        tab = jnp.concatenate([near, jnp.broadcast_to(bias_d[2 * MOBA_BLOCK][:, None], (N_HEADS, PAGE_SIZE)),
                               jnp.broadcast_to(bias_d[0][:, None], (N_HEADS, PAGE_SIZE))], axis=1)
        bias_tab = jnp.pad(tab, ((0, HEAD_ROWS - N_HEADS), (0, 0)))
    else:
        assert t % MOBA_BLOCK == 0 and t // MOBA_BLOCK <= LANES
        tm_tok = min(512, t)
        tm_moe = min(1024, t)
        assert MAX_DISTANCE <= MOBA_BLOCK
        bias0, bias1 = rel_bias_tiles(rel_bias)

    q_scale = HEAD_DIM ** -0.5 * (1.0 if paged else LOG2E)

    def mod_part(l, j):
        part = mod[l, :, j * d:(j + 1) * d]
        return part if paged else part.reshape(b, 1, d)

    xf = x.reshape(m, d)
    new_k, new_v, new_logf = [], [], []
    for l in range(depth):
        kind = l % N_MIXERS
        sh1, sc1, gt1, sh2, sc2, gt2 = [mod_part(l, j) for j in range(6)]
        fg = fgates[l // N_MIXERS] if kind == 0 else None
        outs = norm_qkv(xf, norm_g[l, 0].reshape(1, d), sc1, sh1, w_in_b[l], fg,
                        tm=tm_tok, rows_per_seq=t, q_scale=q_scale)
        q, k, v, kb, vb = outs[:5]
        new_k.append(k.reshape(b, t, N_KV_HEADS, HEAD_DIM))
        new_v.append(v.reshape(b, t, N_KV_HEADS, HEAD_DIM))
        if kind == 0:
            lf = outs[5]
            new_logf.append(lf[:, :N_HEADS].reshape(b, t, N_HEADS))
        if paged:
            q16 = jnp.pad(q.reshape(b, N_HEADS, HEAD_DIM), ((0, 0), (0, HEAD_ROWS - N_HEADS), (0, 0)))
            kn = k.reshape(b, N_KV_HEADS, HEAD_DIM)
            vn = v.reshape(b, N_KV_HEADS, HEAD_DIM)
            if kind == 0:
                lfn = jnp.broadcast_to(
                    jnp.pad(lf[:, :N_HEADS], ((0, 0), (0, HEAD_ROWS - N_HEADS)))[:, :, None],
                    (b, HEAD_ROWS, PAGE_SIZE))
                o16 = fox_attention_decode(q16, kn, vn, lfn, cache_k, cache_v, cache_lft, page_table,
                                           l, l // N_MIXERS)
            elif kind == 1:
                q128 = jnp.pad(q.reshape(b, N_HEADS, HEAD_DIM), ((0, 0), (0, LANES - N_HEADS), (0, 0)))
                sel = moba_select_decode(q128, cache_k, page_table, l)
                sel = sel[:, :MOBA_TOPK, :N_HEADS]
                o16 = moba_attention_decode(q16, kn, vn, bias_tab, sel, cache_k, cache_v, page_table, l)
            else:
                o16 = sb_attention_decode(q16, cache_k, cache_v, page_table, l)
            o = o16[:, :N_HEADS, :].reshape(m, Q_DIM).astype(BF16)
        else:
            q3 = q.reshape(b, t, Q_DIM)
            kb3 = kb.reshape(b, t, KV_DIM)
            vb3 = vb.reshape(b, t, KV_DIM)
            if kind == 0:
                o = fox_attention_prompt(q3, kb3, vb3, cumsum_time(lf.reshape(b, t, LANES)))
            elif kind == 1:
                kmean = block_mean(k.reshape(b, t * N_KV_HEADS, HEAD_DIM))
                o = moba_attention_prompt(q3, kb3, vb3, kmean, bias0, bias1)
            else:
                o = sb_attention_prompt(q3, kb3, vb3)
            o = o.reshape(m, Q_DIM)
        x1, h2, gates = out_proj_router(o, xf, w_out_b[l], gt1, norm_g[l, 1].reshape(1, d), sc2, sh2,
                                        wr_hi, wr_lo, br, tm=tm_tok, rows_per_seq=t)
        fin = final_g.reshape(1, d) if l == depth - 1 else None
        xf = moe_ffn(h2, gates, x1, gt2, wg_b[l], wu_b[l], wd_b[l], fin, tm=tm_moe, rows_per_seq=t)
    return xf.reshape(b, t, d), jnp.stack(new_k), jnp.stack(new_v), jnp.stack(new_logf)


def kernel(x_prompt, x_sample, cache_k, cache_v, cache_logf, page_table, c_prompt, c_sample, rel_bias,
           w_ada, b_ada, norm_g, final_g, w_in, w_out, w_fgate, b_fgate, w_router, b_router,
           w_gate, w_up, w_down):
    n_prompt = c_prompt.shape[0]
    mod = ada_modulation(jnp.concatenate([c_prompt, c_sample], axis=0), w_ada, b_ada)

    fgates = []
    for a in range(w_fgate.shape[0]):
        wf, bf = _pad_lanes(w_fgate[a], b_fgate[a])
        fgates.append((wf.astype(BF16), bf))
    wr, br = _pad_lanes(w_router, b_router)
    wr_hi, wr_lo = _split2(wr)
    weights = (norm_g, final_g, w_in.astype(BF16), w_out.astype(BF16), fgates, wr_hi, wr_lo, br,
               w_gate.astype(BF16), w_up.astype(BF16), w_down.astype(BF16))

    depth, n_pool = cache_k.shape[:2]
    cache = (cache_k.reshape(depth, n_pool, PAGE_ROWS, HEAD_DIM),
             cache_v.reshape(depth, n_pool, PAGE_ROWS, HEAD_DIM),
             jnp.swapaxes(cache_logf, 2, 3), page_table)

    y_p, k_p, v_p, lf_p = _trunk(x_prompt, mod[:, :n_prompt], None, rel_bias, weights)
    y_s, k_s, v_s, lf_s = _trunk(x_sample, mod[:, n_prompt:], cache, rel_bias, weights)
    return (y_p, y_s, k_p, v_p, lf_p, k_s, v_s, lf_s)
```

```python
import functools
import math

import jax
import jax.numpy as jnp
from jax import lax
from jax.experimental import pallas as pl
from jax.experimental.pallas import tpu as pltpu

F32 = jnp.float32
BF16 = jnp.bfloat16
I32 = jnp.int32

N_MIXERS = 3
N_HEADS = 8
N_KV_HEADS = 4
GROUP = N_HEADS // N_KV_HEADS
HEAD_DIM = 128
Q_DIM = N_HEADS * HEAD_DIM
KV_DIM = N_KV_HEADS * HEAD_DIM
PAGE_SIZE = 128
MOBA_BLOCK = 256
MOBA_TOPK = 3
N_BUCKETS = 32
MAX_DISTANCE = 128
N_EXPERTS = 16
N_GROUPS = 4
EXPERTS_PER_GROUP = N_EXPERTS // N_GROUPS
TOP_K = 2
RMS_EPS = 1e-6

PAGE_ROWS = PAGE_SIZE * N_KV_HEADS
LANES = 128
HEAD_ROWS = 16
MASKED = -1e30
LOG2E = math.log2(math.e)
VMEM_LIMIT = 56 * 1024 * 1024


def _params(*sem):
    return pltpu.CompilerParams(dimension_semantics=sem, vmem_limit_bytes=VMEM_LIMIT)


def _dot(a, b):
    return jnp.dot(a, b, preferred_element_type=F32)


def _dot_nt(a, b):
    return lax.dot_general(a, b, (((1,), (1,)), ((), ())), preferred_element_type=F32)


def _split2(x):
    hi = x.astype(BF16)
    lo = (x - hi.astype(F32)).astype(BF16)
    return hi, lo


def _split3(x):
    hi = x.astype(BF16)
    r = x - hi.astype(F32)
    mid = r.astype(BF16)
    lo = (r - mid.astype(F32)).astype(BF16)
    return hi, mid, lo


def _neg_softplus(z):
    return -(jnp.maximum(z, 0.0) + jnp.log1p(jnp.exp(-jnp.abs(z))))


def _silu(x):
    return x / (1.0 + jnp.exp(-x))


def _ada_kernel(c_ref, w_ref, b_ref, o_ref):
    s = _silu(c_ref[...])
    o_ref[...] = _dot(s.astype(BF16), w_ref[...].astype(BF16)) + b_ref[...]


def ada_modulation(c_all, w_ada, b_ada):
    depth, d, n = w_ada.shape
    mc = c_all.shape[0]
    tn = 1024
    return pl.pallas_call(
        _ada_kernel,
        grid=(depth, n // tn),
        in_specs=[
            pl.BlockSpec((mc, d), lambda l, j: (0, 0)),
            pl.BlockSpec((None, d, tn), lambda l, j: (l, 0, j)),
            pl.BlockSpec((None, 1, tn), lambda l, j: (l, 0, j)),
        ],
        out_specs=pl.BlockSpec((None, mc, tn), lambda l, j: (l, 0, j)),
        out_shape=jax.ShapeDtypeStruct((depth, mc, n), F32),
        compiler_params=_params("parallel", "parallel"),
        name="ada_modulation",
    )(c_all, w_ada, b_ada.reshape(depth, 1, n))


def _rms_mod(x, g, sc, sh):
    r = lax.rsqrt(jnp.mean(x * x, axis=-1, keepdims=True) + RMS_EPS)
    return (x * r * g) * (1.0 + sc) + sh


def _qkv_kernel(*refs, has_fgate, q_scale):
    if has_fgate:
        (x_ref, g_ref, sc_ref, sh_ref, w_ref, wf_ref, bf_ref,
         q_ref, k_ref, v_ref, kb_ref, vb_ref, lf_ref) = refs
    else:
        x_ref, g_ref, sc_ref, sh_ref, w_ref, q_ref, k_ref, v_ref, kb_ref, vb_ref = refs
    hb = _rms_mod(x_ref[...], g_ref[...], sc_ref[...], sh_ref[...]).astype(BF16)
    qkv = _dot(hb, w_ref[...])
    q_ref[...] = (qkv[:, :Q_DIM] * q_scale).astype(BF16)
    k = qkv[:, Q_DIM:Q_DIM + KV_DIM]
    v = qkv[:, Q_DIM + KV_DIM:]
    tm = k.shape[0]
    for kvh in range(N_KV_HEADS):
        rows = pl.ds(kvh, tm, stride=N_KV_HEADS)
        k_ref[rows, :] = k[:, kvh * HEAD_DIM:(kvh + 1) * HEAD_DIM]
        v_ref[rows, :] = v[:, kvh * HEAD_DIM:(kvh + 1) * HEAD_DIM]
    kb_ref[...] = k.astype(BF16)
    vb_ref[...] = v.astype(BF16)
    if has_fgate:
        z = _dot(hb, wf_ref[...]) + bf_ref[...]
        lf_ref[...] = jnp.minimum(z, 0.0) - jnp.log1p(jnp.exp(-jnp.abs(z)))


def _mod_spec(mod, tm, rows_per_seq):
    d = mod.shape[-1]
    if mod.ndim == 3:
        tiles = rows_per_seq // tm
        return pl.BlockSpec((None, 1, d), lambda i: (i // tiles, 0, 0))
    return pl.BlockSpec((tm, d), lambda i: (i, 0))


def norm_qkv(x, g, sc, sh, w_in_b, fgate, *, tm, rows_per_seq, q_scale):
    m, d = x.shape
    n = w_in_b.shape[1]
    has_fgate = fgate is not None
    row = lambda i: (i, 0)
    const = lambda i: (0, 0)
    in_specs = [pl.BlockSpec((tm, d), row), pl.BlockSpec((1, d), const),
                _mod_spec(sc, tm, rows_per_seq), _mod_spec(sh, tm, rows_per_seq),
                pl.BlockSpec((d, n), const)]
    args = [x, g, sc, sh, w_in_b]
    out_specs = [pl.BlockSpec((tm, Q_DIM), row), pl.BlockSpec((tm * N_KV_HEADS, HEAD_DIM), row),
                 pl.BlockSpec((tm * N_KV_HEADS, HEAD_DIM), row), pl.BlockSpec((tm, KV_DIM), row),
                 pl.BlockSpec((tm, KV_DIM), row)]
    out_shape = [jax.ShapeDtypeStruct((m, Q_DIM), BF16),
                 jax.ShapeDtypeStruct((m * N_KV_HEADS, HEAD_DIM), F32),
                 jax.ShapeDtypeStruct((m * N_KV_HEADS, HEAD_DIM), F32),
                 jax.ShapeDtypeStruct((m, KV_DIM), BF16), jax.ShapeDtypeStruct((m, KV_DIM), BF16)]
    if has_fgate:
        wf, bf = fgate
        in_specs += [pl.BlockSpec((d, LANES), const), pl.BlockSpec((1, LANES), const)]
        args += [wf, bf]
        out_specs.append(pl.BlockSpec((tm, LANES), row))
        out_shape.append(jax.ShapeDtypeStruct((m, LANES), F32))
    return pl.pallas_call(
        functools.partial(_qkv_kernel, has_fgate=has_fgate, q_scale=q_scale),
        grid=(m // tm,),
        in_specs=in_specs, out_specs=out_specs, out_shape=out_shape,
        compiler_params=_params("parallel"),
        name="norm_qkv",
    )(*args)


def _cumsum_kernel(lf_ref, dt_ref, *, chunk):
    t = lf_ref.shape[0]
    row = lax.broadcasted_iota(I32, (chunk, chunk), 0)
    col = lax.broadcasted_iota(I32, (chunk, chunk), 1)
    tri = jnp.where(col <= row, 1.0, 0.0).astype(BF16)
    carry = jnp.zeros((1, LANES), F32)
    for c in range(t // chunk):
        sl = slice(c * chunk, (c + 1) * chunk)
        hi, mid, lo = _split3(lf_ref[sl, :])
        cs = _dot(tri, hi) + _dot(tri, mid) + _dot(tri, lo) + carry
        dt_ref[:, sl] = cs.T[:N_HEADS, :]
        carry = cs[chunk - 1:chunk, :]


def cumsum_time(lf):
    b, t, _ = lf.shape
    chunk = min(256, t)
    return pl.pallas_call(
        functools.partial(_cumsum_kernel, chunk=chunk),
        grid=(b,),
        in_specs=[pl.BlockSpec((None, t, LANES), lambda i: (i, 0, 0))],
        out_specs=pl.BlockSpec((None, N_HEADS, t), lambda i: (i, 0, 0)),
        out_shape=jax.ShapeDtypeStruct((b, N_HEADS, t), F32),
        compiler_params=_params("parallel"),
        name="cumsum_time",
    )(lf)


def _lane_pick(x, idx):
    lane = lax.broadcasted_iota(I32, x.shape, 1)
    return jnp.sum(jnp.where(lane == idx, x, 0.0), axis=1, keepdims=True)


def _with_ones(v):
    return jnp.concatenate([v, jnp.ones(v.shape, v.dtype)], axis=1)


def _softmax2_first(s, v1):
    m = jnp.max(s, axis=1, keepdims=True)
    return m, _dot(jnp.exp2(s - m).astype(BF16), v1)


def _softmax2_step(s, v1, m, accl, keep=None):
    m_new = jnp.maximum(m, jnp.max(s, axis=1, keepdims=True))
    new = accl * jnp.exp2(m - m_new) + _dot(jnp.exp2(s - m_new).astype(BF16), v1)
    if keep is None:
        return m_new, new
    return jnp.where(keep, m_new, m), jnp.where(keep, new, accl)


def _softmax2_out(accl):
    return (accl[:, :HEAD_DIM] / accl[:, HEAD_DIM:HEAD_DIM + 1]).astype(BF16)


def _fox_kernel(q_ref, k_ref, v_ref, dt_ref, o_ref, *, tq):
    kvh = pl.program_id(1)
    i = pl.program_id(2)
    q0 = pl.multiple_of(i * tq, tq)
    row = lax.broadcasted_iota(I32, (tq, tq), 0)
    col = lax.broadcasted_iota(I32, (tq, tq), 1)
    causal = col <= row
    heads = [kvh * GROUP + g for g in range(GROUP)]
    qs = [q_ref[:, g * HEAD_DIM:(g + 1) * HEAD_DIM] for g in range(GROUP)]
    d0s = [dt_ref[pl.ds(h, 1), pl.ds(q0, tq)][:, 0:1] for h in heads]

    def logits(g, start, kt):
        dk = dt_ref[pl.ds(heads[g], 1), pl.ds(start, tq)]
        return _dot_nt(qs[g], kt) + (d0s[g] - dk) * LOG2E

    def step(s, vt, m, l, acc):
        m_new = jnp.maximum(m, jnp.max(s, axis=1, keepdims=True))
        alpha = jnp.exp2(m - m_new)
        p = jnp.exp2(s - m_new)
        return m_new, l * alpha + jnp.sum(p, axis=1, keepdims=True), acc * alpha + _dot(p.astype(BF16), vt)

    kt, vt = k_ref[pl.ds(q0, tq), :], v_ref[pl.ds(q0, tq), :]
    zero = (jnp.full((tq, 1), MASKED, F32), jnp.zeros((tq, 1), F32), jnp.zeros((tq, HEAD_DIM), F32))
    state = tuple(step(jnp.where(causal, logits(g, q0, kt), MASKED), vt, *zero) for g in range(GROUP))

    def body(j, state):
        start = pl.multiple_of(j * tq, tq)
        kt, vt = k_ref[pl.ds(start, tq), :], v_ref[pl.ds(start, tq), :]
        return tuple(step(logits(g, start, kt), vt, *state[g]) for g in range(GROUP))

    state = lax.fori_loop(0, i, body, state)
    for g, (_, l, acc) in enumerate(state):
        o_ref[:, g * HEAD_DIM:(g + 1) * HEAD_DIM] = (acc / l).astype(BF16)


def _attn_specs(t, tq):
    qo = pl.BlockSpec((None, tq, GROUP * HEAD_DIM), lambda b, kvh, i: (b, i, kvh))
    kv = pl.BlockSpec((None, t, HEAD_DIM), lambda b, kvh, i: (b, 0, kvh))
    return qo, kv


def fox_attention_prompt(q, kb, vb, dt):
    b, t, _ = q.shape
    tq = min(512, t)
    qo, kv = _attn_specs(t, tq)
    return pl.pallas_call(
        functools.partial(_fox_kernel, tq=tq),
        grid=(b, N_KV_HEADS, t // tq),
        in_specs=[qo, kv, kv, pl.BlockSpec((None, N_HEADS, t), lambda b, kvh, i: (b, 0, 0))],
        out_specs=qo,
        out_shape=jax.ShapeDtypeStruct((b, t, Q_DIM), BF16),
        compiler_params=_params("parallel", "parallel", "parallel"),
        name="fox_attention_prompt",
    )(q, kb, vb, dt)


def _suffix_matrices(n):
    row = lax.broadcasted_iota(I32, (n, n), 0)
    col = lax.broadcasted_iota(I32, (n, n), 1)
    return jnp.where(row > col, 1.0, 0.0).astype(BF16), jnp.ones((n, n), BF16)


def _sb_kernel(q_ref, k_ref, v_ref, o_ref, *, tq, tk):
    i = pl.program_id(2)
    q0 = pl.multiple_of(i * tq, tq)
    upper, _ = _suffix_matrices(tk)
    row = lax.broadcasted_iota(I32, (tq, tk), 0)
    col = lax.broadcasted_iota(I32, (tq, tk), 1)
    qs = [q_ref[:, g * HEAD_DIM:(g + 1) * HEAD_DIM] for g in range(GROUP)]

    def chunk(start, state, mask):
        kt, vt = k_ref[pl.ds(start, tk), :], v_ref[pl.ds(start, tk), :]
        out = []
        for g in range(GROUP):
            c, acc = state[g]
            z = _dot_nt(qs[g], kt)
            ls = jnp.minimum(z, 0.0) - jnp.log2(1.0 + jnp.exp2(jnp.minimum(z, -z)))
            lk = ls - z
            if mask is not None:
                lk = jnp.where(mask, lk, 0.0)
            hi, lo = _split2(lk)
            after = _dot(hi, upper) + _dot(lo, upper) + c
            a = jnp.exp2(ls + after)
            if mask is not None:
                a = jnp.where(mask, a, 0.0)
            out.append((c + jnp.sum(lk, axis=1, keepdims=True), acc + _dot(a.astype(BF16), vt)))
        return tuple(out)

    state = tuple((jnp.zeros((tq, 1), F32), jnp.zeros((tq, HEAD_DIM), F32)) for _ in range(GROUP))
    for mth in reversed(range(tq // tk)):
        start = pl.multiple_of(q0 + mth * tk, tk)
        state = chunk(start, state, (col + mth * tk) < row)

    n_past = i * (tq // tk)

    def body(it, state):
        return chunk(pl.multiple_of((n_past - 1 - it) * tk, tk), state, None)

    state = lax.fori_loop(0, n_past, body, state)
    for g, (_, acc) in enumerate(state):
        o_ref[:, g * HEAD_DIM:(g + 1) * HEAD_DIM] = acc.astype(BF16)


def sb_attention_prompt(q, kb, vb):
    b, t, _ = q.shape
    tq = min(512, t)
    tk = min(256, t)
    qo, kv = _attn_specs(t, tq)
    return pl.pallas_call(
        functools.partial(_sb_kernel, tq=tq, tk=tk),
        grid=(b, N_KV_HEADS, t // tq),
        in_specs=[qo, kv, kv],
        out_specs=qo,
        out_shape=jax.ShapeDtypeStruct((b, t, Q_DIM), BF16),
        compiler_params=_params("parallel", "parallel", "parallel"),
        name="sb_attention_prompt",
    )(q, kb, vb)


def _block_mean_kernel(k_ref, o_ref, *, nb):
    t = k_ref.shape[0] // N_KV_HEADS
    o_ref[...] = jnp.zeros(o_ref.shape, F32)
    for kvh in range(N_KV_HEADS):
        x = k_ref[pl.ds(kvh, t, stride=N_KV_HEADS), :]
        o_ref[kvh, 0:nb, :] = jnp.sum(x.reshape(nb, t // nb, HEAD_DIM), axis=1) * (1.0 / (t // nb))


def block_mean(k4):
    b, rows, _ = k4.shape
    nb = rows // N_KV_HEADS // MOBA_BLOCK
    return pl.pallas_call(
        functools.partial(_block_mean_kernel, nb=nb),
        grid=(b,),
        in_specs=[pl.BlockSpec((None, rows, HEAD_DIM), lambda i: (i, 0, 0))],
        out_specs=pl.BlockSpec((None, N_KV_HEADS, LANES, HEAD_DIM), lambda i: (i, 0, 0, 0)),
        out_shape=jax.ShapeDtypeStruct((b, N_KV_HEADS, LANES, HEAD_DIM), F32),
        compiler_params=_params("parallel"),
        name="block_mean",
    )(k4)


def _top_mask(gate, n_valid, n_cand, axis):
    idx = lax.broadcasted_iota(I32, gate.shape, axis)
    cnt = jnp.zeros(gate.shape, I32)
    for jp in range(n_cand):
        other = gate[:, jp:jp + 1] if axis == 1 else gate[jp:jp + 1, :]
        beats = (other > gate) | ((other == gate) & (jp < idx))
        cnt = cnt + jnp.where(beats, jnp.where(jp < n_valid, 1, 0), 0)
    return (idx < n_valid) & (cnt < MOBA_TOPK)


def _moba_kernel(q_ref, k_ref, v_ref, km_ref, b0_ref, b1_ref, o_ref, *, tq, nb):
    i = pl.program_id(2)
    q0 = pl.multiple_of(i * tq, tq)
    row = lax.broadcasted_iota(I32, (tq, tq), 0)
    col = lax.broadcasted_iota(I32, (tq, tq), 1)
    causal = col <= row
    km_hi, km_lo = _split2(km_ref[...])
    qs = [q_ref[:, g * HEAD_DIM:(g + 1) * HEAD_DIM] for g in range(GROUP)]
    sels = [jnp.where(_top_mask(_dot_nt(qh, km_hi) + _dot_nt(qh, km_lo), i, nb, 1), 1.0, 0.0) for qh in qs]

    def tile(j):
        start = pl.multiple_of(j * tq, tq)
        return k_ref[pl.ds(start, tq), :], _with_ones(v_ref[pl.ds(start, tq), :])

    def picked(g, j):
        return _lane_pick(sels[g], j) > 0.0

    kt, v1 = tile(i)
    state = tuple(_softmax2_first(jnp.where(causal, _dot_nt(qs[g], kt) + b0_ref[g], MASKED), v1)
                  for g in range(GROUP))
    prev = jnp.maximum(i - 1, 0)
    kt, v1 = tile(prev)
    state = tuple(_softmax2_step(_dot_nt(qs[g], kt) + b1_ref[g], v1, *state[g], keep=picked(g, i - 1))
                  for g in range(GROUP))

    def body(j, state):
        kt, v1 = tile(j)
        return tuple(_softmax2_step(_dot_nt(qs[g], kt), v1, *state[g], keep=picked(g, j))
                     for g in range(GROUP))

    state = lax.fori_loop(0, i - 1, body, state)
    for g, (_, accl) in enumerate(state):
        o_ref[:, g * HEAD_DIM:(g + 1) * HEAD_DIM] = _softmax2_out(accl)


def _bias_tile_kernel(rb_ref, b0_ref, b1_ref, *, n):
    h = pl.program_id(0)
    row = lax.broadcasted_iota(I32, (n, n), 0)
    col = lax.broadcasted_iota(I32, (n, n), 1)
    max_exact = N_BUCKETS // 2
    for out_ref, shift in ((b0_ref, 0), (b1_ref, n)):
        dist = jnp.maximum(row - col + shift, 0)
        large = max_exact + (jnp.log(jnp.maximum(dist, 1).astype(F32) / max_exact)
                             / math.log(MAX_DISTANCE / max_exact) * (N_BUCKETS - max_exact)).astype(I32)
        bucket = jnp.where(dist < max_exact, dist, jnp.minimum(large, N_BUCKETS - 1))
        acc = jnp.zeros((n, n), F32)
        for bk in range(N_BUCKETS):
            acc = jnp.where(bucket == bk, rb_ref[bk, h], acc)
        out_ref[...] = (acc - rb_ref[N_BUCKETS - 1, h]) * LOG2E


def rel_bias_tiles(rel_bias):
    n = MOBA_BLOCK
    shape = jax.ShapeDtypeStruct((N_HEADS, n, n), F32)
    spec = pl.BlockSpec((None, n, n), lambda h: (h, 0, 0))
    return pl.pallas_call(
        functools.partial(_bias_tile_kernel, n=n),
        grid=(N_HEADS,),
        in_specs=[pl.BlockSpec(memory_space=pltpu.SMEM)],
        out_specs=[spec, spec],
        out_shape=[shape, shape],
        compiler_params=_params("parallel"),
        name="rel_bias_tiles",
    )(rel_bias)


def moba_attention_prompt(q, kb, vb, kmean, bias0, bias1):
    b, t, _ = q.shape
    tq = MOBA_BLOCK
    nb = t // tq
    qo, kv = _attn_specs(t, tq)
    head_pair = lambda b, kvh, i: (kvh, 0, 0)
    return pl.pallas_call(
        functools.partial(_moba_kernel, tq=tq, nb=nb),
        grid=(b, N_KV_HEADS, nb),
        in_specs=[qo, kv, kv,
                  pl.BlockSpec((None, None, LANES, HEAD_DIM), lambda b, kvh, i: (b, kvh, 0, 0)),
                  pl.BlockSpec((GROUP, tq, tq), head_pair),
                  pl.BlockSpec((GROUP, tq, tq), head_pair)],
        out_specs=qo,
        out_shape=jax.ShapeDtypeStruct((b, t, Q_DIM), BF16),
        compiler_params=_params("parallel", "parallel", "parallel"),
        name="moba_attention_prompt",
    )(q, kb, vb, kmean, bias0, bias1)


def _route(logits):
    idx = lax.broadcasted_iota(I32, logits.shape, 1)
    valid = idx < N_EXPERTS
    lg = jnp.where(valid, logits, MASKED)
    e = jnp.where(valid, jnp.exp(lg - jnp.max(lg, axis=1, keepdims=True)), 0.0)
    grp = idx >> 2

    def peers(x, shifts):
        for s in shifts:
            for sh in (s, LANES - s):
                oi = pltpu.roll(idx, sh, 1)
                yield pltpu.roll(x, sh, 1), oi, oi < N_EXPERTS

    cnt = jnp.zeros(logits.shape, I32)
    for oe, oi, ok in peers(e, (1, 2, 3)):
        beats = ok & ((oi >> 2) == grp) & ((oe > e) | ((oe == e) & (oi < idx)))
        cnt = cnt + jnp.where(beats, 1, 0)
    top2 = valid & (cnt < TOP_K)
    t2e = jnp.where(top2, e, 0.0)
    score = t2e
    for ot, oi, ok in peers(t2e, (1, 2, 3)):
        score = score + jnp.where(ok & ((oi >> 2) == grp), ot, 0.0)
    lost = jnp.zeros(logits.shape, I32)
    for osc, oi, ok in peers(score, (4, 8, 12)):
        beats = ok & ((osc > score) | ((osc == score) & ((oi >> 2) < grp)))
        lost = lost + jnp.where(beats, 1, 0)
    return jnp.where(top2 & valid & (lost == 0), e / score, 0.0)


def _route_t(lg):
    npos, ngrp = EXPERTS_PER_GROUP, N_GROUPS
    e = jnp.exp(lg - jnp.max(lg, axis=0, keepdims=True))
    pos = [jnp.concatenate([e[g * npos + a:g * npos + a + 1, :] for g in range(ngrp)], axis=0)
           for a in range(npos)]
    top2 = []
    for a in range(npos):
        cnt = jnp.zeros(pos[a].shape, I32)
        for b in range(npos):
            if b != a:
                cnt = cnt + jnp.where((pos[b] >= pos[a]) if b < a else (pos[b] > pos[a]), 1, 0)
        top2.append(cnt < TOP_K)
    score = functools.reduce(jnp.add, [jnp.where(t, p, 0.0) for t, p in zip(top2, pos)])
    rows = [score[g:g + 1, :] for g in range(ngrp)]
    won = []
    for g in range(ngrp):
        lost = jnp.zeros(rows[g].shape, I32)
        for o in range(ngrp):
            if o != g:
                lost = lost + jnp.where((rows[o] >= rows[g]) if o < g else (rows[o] > rows[g]), 1, 0)
        won.append(jnp.where(lost == 0, 1.0, 0.0))
    won = jnp.concatenate(won, axis=0) > 0.0
    gate = [jnp.where(t & won, p / score, 0.0) for t, p in zip(top2, pos)]
    return jnp.concatenate([gate[a][g:g + 1, :] for g in range(ngrp) for a in range(npos)], axis=0)


def _oproj_kernel(o_ref, x_ref, w_ref, gt_ref, g_ref, sc_ref, sh_ref, wr_ref, br_ref,
                  x1_ref, h2_ref, gates_ref, *, transposed):
    x1 = x_ref[...] + gt_ref[...] * _dot(o_ref[...], w_ref[...])
    x1_ref[...] = x1
    h2 = _rms_mod(x1, g_ref[...], sc_ref[...], sh_ref[...])
    hi, lo = _split2(h2)
    h2_ref[...] = hi
    if transposed:
        both = _dot_nt(wr_ref[...], hi)
        lg = both[:N_EXPERTS] + both[N_EXPERTS:] + _dot_nt(wr_ref[0:N_EXPERTS, :], lo) + br_ref[:, 0:1]
        gates = _route_t(lg)
        pad = jnp.zeros((LANES - N_EXPERTS, gates.shape[1]), F32)
        gates_ref[...] = jnp.concatenate([gates, pad], axis=0).T
    else:
        both = _dot(hi, wr_ref[...])
        logits = both[:, :LANES] + both[:, LANES:] + _dot(lo, wr_ref[:, 0:LANES]) + br_ref[...]
        gates_ref[...] = _route(logits)


def router_operands(w_router, b_router, transposed):
    if transposed:
        hi, lo = _split2(w_router.T)
        return jnp.concatenate([hi, lo], axis=0), jnp.broadcast_to(b_router[:, None], (N_EXPERTS, LANES))
    wr, br = _pad_lanes(w_router, b_router)
    hi, lo = _split2(wr)
    return jnp.concatenate([hi, lo], axis=1), br


def out_proj_router(o, x, w_out_b, gt, g2, sc, sh, wr, br, *, tm, rows_per_seq):
    m, d = x.shape
    transposed = wr.shape[1] == d
    row = lambda i: (i, 0)
    const = lambda i: (0, 0)
    ms = lambda a: _mod_spec(a, tm, rows_per_seq)
    return pl.pallas_call(
        functools.partial(_oproj_kernel, transposed=transposed),
        grid=(m // tm,),
        in_specs=[pl.BlockSpec((tm, Q_DIM), row), pl.BlockSpec((tm, d), row),
                  pl.BlockSpec((Q_DIM, d), const), ms(gt), pl.BlockSpec((1, d), const),
                  ms(sc), ms(sh), pl.BlockSpec(wr.shape, const), pl.BlockSpec(br.shape, const)],
        out_specs=[pl.BlockSpec((tm, d), row), pl.BlockSpec((tm, d), row),
                   pl.BlockSpec((tm, LANES), row)],
        out_shape=[jax.ShapeDtypeStruct((m, d), F32), jax.ShapeDtypeStruct((m, d), BF16),
                   jax.ShapeDtypeStruct((m, LANES), F32)],
        compiler_params=_params("parallel"),
        name="out_proj_router",
    )(o, x, w_out_b, gt, g2, sc, sh, wr, br)


def _moe_kernel(h_ref, gates_ref, x_ref, gt_ref, wg_ref, wu_ref, wd_ref, *rest, epc, final):
    fg_ref, o_ref, acc_ref = rest if final else (None, *rest)
    c = pl.program_id(1)

    @pl.when(c == 0)
    def _():
        acc_ref[...] = jnp.zeros(acc_ref.shape, F32)

    h = h_ref[...]
    gates = gates_ref[...]
    acts = []
    for j in range(epc):
        gcol = _lane_pick(gates, c * epc + j)
        acts.append((_silu(_dot(h, wg_ref[j])) * _dot(h, wu_ref[j]) * gcol).astype(BF16))
    wd = wd_ref[...]
    acc_ref[...] += _dot(jnp.concatenate(acts, axis=1), wd.reshape(wd.shape[0] * wd.shape[1], wd.shape[2]))

    @pl.when(c == pl.num_programs(1) - 1)
    def _():
        x = x_ref[...] + gt_ref[...] * acc_ref[...]
        if fg_ref is not None:
            x = x * lax.rsqrt(jnp.mean(x * x, axis=-1, keepdims=True) + RMS_EPS) * fg_ref[...]
        o_ref[...] = x


def moe_ffn(h2, gates, x1, gt, wg_b, wu_b, wd_b, final_g, *, tm, rows_per_seq):
    m, d = x1.shape
    n_e, _, de = wg_b.shape
    epc = 4
    row = lambda i, c: (i, 0)
    gt_spec = _mod_spec(gt, tm, rows_per_seq)
    gt_spec = pl.BlockSpec(gt_spec.block_shape, lambda i, c, f=gt_spec.index_map: f(i))
    in_specs = [pl.BlockSpec((tm, d), row), pl.BlockSpec((tm, LANES), row),
                pl.BlockSpec((tm, d), row), gt_spec,
                pl.BlockSpec((epc, d, de), lambda i, c: (c, 0, 0)),
                pl.BlockSpec((epc, d, de), lambda i, c: (c, 0, 0)),
                pl.BlockSpec((epc, de, d), lambda i, c: (c, 0, 0))]
    args = [h2, gates, x1, gt, wg_b, wu_b, wd_b]
    if final_g is not None:
        in_specs.append(pl.BlockSpec((1, d), lambda i, c: (0, 0)))
        args.append(final_g)
    return pl.pallas_call(
        functools.partial(_moe_kernel, epc=epc, final=final_g is not None),
        grid=(m // tm, n_e // epc),
        in_specs=in_specs,
        out_specs=pl.BlockSpec((tm, d), row),
        out_shape=jax.ShapeDtypeStruct((m, d), F32),
        scratch_shapes=[pltpu.VMEM((tm, d), F32)],
        compiler_params=_params("parallel", "arbitrary"),
        name="moe_ffn",
    )(*args)


def _head_rows(mats):
    rowi = lax.broadcasted_iota(I32, mats[0].shape, 0)
    out = mats[0]
    for kvh in range(1, N_KV_HEADS):
        out = jnp.where((rowi >> 1) == kvh, mats[kvh], out)
    return out


def _kv_rows(page_ref):
    return [page_ref[pl.ds(kvh, PAGE_SIZE, stride=N_KV_HEADS), :] for kvh in range(N_KV_HEADS)]


def _new_token_logits(q, kn):
    qf = q.astype(F32)
    prods = [qf * kn[kvh:kvh + 1, :].astype(BF16).astype(F32) for kvh in range(N_KV_HEADS)]
    return jnp.sum(_head_rows(prods), axis=1, keepdims=True)


def _new_token_values(vn):
    return _head_rows([jnp.broadcast_to(vn[kvh:kvh + 1, :].astype(BF16).astype(F32), (HEAD_ROWS, HEAD_DIM))
                       for kvh in range(N_KV_HEADS)])


def _page_spec(layer, n_pages, page_of):
    return pl.BlockSpec((None, None, PAGE_ROWS, HEAD_DIM),
                        lambda b, c, pt: (layer, pt[b * n_pages + page_of(c)], 0, 0))


def _seq_spec(rows, width):
    return pl.BlockSpec((None, rows, width), lambda b, c, pt: (b, 0, 0))


def _suffix_total_matrix(n):
    upper, ones = _suffix_matrices(n)
    return jnp.concatenate([upper, ones], axis=1)


def _suffix_and_total(x, suffix_total):
    hi, lo = _split2(x)
    both = _dot(hi, suffix_total) + _dot(lo, suffix_total)
    return both[:, :PAGE_SIZE], both[:, PAGE_SIZE:]


def _page_logits(q, k_refs):
    return [_head_rows([_dot_nt(q, ks.astype(BF16)) for ks in _kv_rows(ref)]) for ref in k_refs]


def _page_values(w_pages, v_refs):
    pv = [jnp.zeros((HEAD_ROWS, HEAD_DIM), F32)] * N_KV_HEADS
    for w, ref in zip(w_pages, v_refs):
        wb = w.astype(BF16)
        pv = [a + _dot(wb, vs.astype(BF16)) for a, vs in zip(pv, _kv_rows(ref))]
    return _head_rows(pv)


def _carry_back(car, sufs, tots):
    after = [None] * len(sufs)
    for r in reversed(range(len(sufs))):
        after[r] = car + sufs[r]
        car = car + tots[r]
    return after, car


def _fox_dec_kernel(pt_ref, q_ref, kn_ref, vn_ref, lfn_ref, *refs, pages):
    k_refs = refs[:pages]
    v_refs = refs[pages:2 * pages]
    lf_refs = refs[2 * pages:3 * pages]
    o_ref, m_s, l_s, acc_s, car_s = refs[3 * pages:]
    c = pl.program_id(1)
    q = q_ref[...]
    suffix_total = _suffix_total_matrix(PAGE_SIZE)

    @pl.when(c == 0)
    def _():
        m_s[...] = jnp.broadcast_to(_new_token_logits(q, kn_ref[...]), m_s.shape)
        l_s[...] = jnp.ones(l_s.shape, F32)
        acc_s[...] = _new_token_values(vn_ref[...])
        car_s[...] = lfn_ref[...]

    pad = jnp.zeros((HEAD_ROWS - N_HEADS, PAGE_SIZE), F32)
    st = [_suffix_and_total(jnp.concatenate([ref[...], pad], axis=0), suffix_total) for ref in lf_refs]
    qk = _page_logits(q, k_refs)
    decay, car_s[...] = _carry_back(car_s[...], [x[0] for x in st], [x[1] for x in st])
    s_pages = [a + b for a, b in zip(qk, decay)]

    m_old = m_s[...]
    mx = functools.reduce(jnp.maximum, s_pages)
    m_new = jnp.maximum(m_old, jnp.max(mx, axis=1, keepdims=True))
    alpha = jnp.exp(m_old - m_new)
    p_pages = [jnp.exp(s - m_new) for s in s_pages]
    l_s[...] = l_s[...] * alpha + jnp.sum(functools.reduce(jnp.add, p_pages), axis=1, keepdims=True)
    acc_s[...] = acc_s[...] * alpha + _page_values(p_pages, v_refs)
    m_s[...] = m_new

    @pl.when(c == pl.num_programs(1) - 1)
    def _():
        o_ref[...] = acc_s[...] / l_s[...]


def fox_attention_decode(q16, kn, vn, lfn, cache_k, cache_v, cache_lft, page_table, layer, fox_layer):
    b = q16.shape[0]
    n_pages = page_table.shape[1]
    pages = min(8, n_pages)
    nc = n_pages // pages
    page = lambda r: (lambda c: (nc - 1 - c) * pages + r)
    kv_specs = [_page_spec(layer, n_pages, page(r)) for r in range(pages)]
    lf_specs = [pl.BlockSpec((None, None, N_HEADS, PAGE_SIZE),
                             lambda b, c, pt, f=page(r): (fox_layer, pt[b * n_pages + f(c)], 0, 0))
                for r in range(pages)]
    grid_spec = pltpu.PrefetchScalarGridSpec(
        num_scalar_prefetch=1,
        grid=(b, nc),
        in_specs=[_seq_spec(HEAD_ROWS, HEAD_DIM), _seq_spec(N_KV_HEADS, HEAD_DIM),
                  _seq_spec(N_KV_HEADS, HEAD_DIM), _seq_spec(HEAD_ROWS, PAGE_SIZE)]
        + kv_specs + kv_specs + lf_specs,
        out_specs=_seq_spec(HEAD_ROWS, HEAD_DIM),
        scratch_shapes=[pltpu.VMEM((HEAD_ROWS, LANES), F32)] * 4,
    )
    return pl.pallas_call(
        functools.partial(_fox_dec_kernel, pages=pages),
        grid_spec=grid_spec,
        out_shape=jax.ShapeDtypeStruct((b, HEAD_ROWS, HEAD_DIM), F32),
        compiler_params=_params("parallel", "arbitrary"),
        name="fox_attention_decode",
    )(page_table.reshape(-1), q16, kn, vn, lfn, *([cache_k] * pages), *([cache_v] * pages),
      *([cache_lft] * pages))


def _sb_dec_kernel(pt_ref, q_ref, *refs, pages):
    k_refs = refs[:pages]
    v_refs = refs[pages:2 * pages]
    o_ref, acc_s, car_s = refs[2 * pages:]
    c = pl.program_id(1)
    q = q_ref[...]
    suffix_total = _suffix_total_matrix(PAGE_SIZE)

    @pl.when(c == 0)
    def _():
        acc_s[...] = jnp.zeros(acc_s.shape, F32)
        car_s[...] = jnp.zeros(car_s.shape, F32)

    zs = _page_logits(q, k_refs)
    lks = [_neg_softplus(z) for z in zs]
    st = [_suffix_and_total(lk, suffix_total) for lk in lks]
    after, car_s[...] = _carry_back(car_s[...], [x[0] for x in st], [x[1] for x in st])
    a_pages = [jnp.exp(z + lk + af) for z, lk, af in zip(zs, lks, after)]
    acc_s[...] += _page_values(a_pages, v_refs)

    @pl.when(c == pl.num_programs(1) - 1)
    def _():
        o_ref[...] = acc_s[...]


def sb_attention_decode(q16, cache_k, cache_v, page_table, layer):
    b = q16.shape[0]
    n_pages = page_table.shape[1]
    pages = min(8, n_pages)
    nc = n_pages // pages
    kv_specs = [_page_spec(layer, n_pages, lambda c, r=r: (nc - 1 - c) * pages + r)
                for r in range(pages)]
    grid_spec = pltpu.PrefetchScalarGridSpec(
        num_scalar_prefetch=1,
        grid=(b, nc),
        in_specs=[_seq_spec(HEAD_ROWS, HEAD_DIM)] + kv_specs + kv_specs,
        out_specs=_seq_spec(HEAD_ROWS, HEAD_DIM),
        scratch_shapes=[pltpu.VMEM((HEAD_ROWS, LANES), F32)] * 2,
    )
    return pl.pallas_call(
        functools.partial(_sb_dec_kernel, pages=pages),
        grid_spec=grid_spec,
        out_shape=jax.ShapeDtypeStruct((b, HEAD_ROWS, HEAD_DIM), F32),
        compiler_params=_params("parallel", "arbitrary"),
        name="sb_attention_decode",
    )(page_table.reshape(-1), q16, *([cache_k] * pages), *([cache_v] * pages))


def _page_head_sums(page_ref):
    x = page_ref[...]
    s8 = jnp.sum(x.reshape(PAGE_ROWS // 8, 8, HEAD_DIM), axis=0)
    return s8[:N_KV_HEADS] + s8[N_KV_HEADS:]


def _moba_sel_kernel(pt_ref, q_ref, *refs, pages, nb):
    k_refs = refs[:pages]
    km_ref, sel_ref = refs[pages:]
    c = pl.program_id(1)
    bps = pages // 2
    sums = [_page_head_sums(k_refs[r]) for r in range(pages)]
    means = [(sums[2 * j] + sums[2 * j + 1]) * (1.0 / MOBA_BLOCK) for j in range(bps)]
    row0 = pl.multiple_of(c * bps, bps)
    for kvh in range(N_KV_HEADS):
        km_ref[kvh, pl.ds(row0, bps), :] = jnp.concatenate([mj[kvh:kvh + 1] for mj in means], axis=0)

    @pl.when(c == pl.num_programs(1) - 1)
    def _():
        q = q_ref[...]
        lane = lax.broadcasted_iota(I32, (nb, LANES), 1)
        gate = jnp.zeros((nb, LANES), F32)
        for kvh in range(N_KV_HEADS):
            hi, lo = _split2(km_ref[kvh])
            gate = jnp.where((lane >> 1) == kvh, _dot_nt(hi, q) + _dot_nt(lo, q), gate)
        sel = _top_mask(gate, nb, nb, 0)
        blk = lax.broadcasted_iota(I32, (nb, LANES), 0).astype(F32)
        rows = []
        for _ in range(MOBA_TOPK):
            first = jnp.min(jnp.where(sel, blk, float(nb)), axis=0, keepdims=True)
            rows.append(first.astype(I32))
            sel = sel & (blk != first)
        rows.append(jnp.zeros((8 - MOBA_TOPK, LANES), I32))
        sel_ref[...] = jnp.concatenate(rows, axis=0)


def moba_select_decode(q128, cache_k, page_table, layer):
    b = q128.shape[0]
    n_pages = page_table.shape[1]
    nb = n_pages * PAGE_SIZE // MOBA_BLOCK
    pages = min(16, n_pages)
    nc = n_pages // pages
    k_specs = [_page_spec(layer, n_pages, lambda c, r=r: c * pages + r) for r in range(pages)]
    grid_spec = pltpu.PrefetchScalarGridSpec(
        num_scalar_prefetch=1,
        grid=(b, nc),
        in_specs=[_seq_spec(LANES, HEAD_DIM)] + k_specs,
        out_specs=[pl.BlockSpec((None, N_KV_HEADS, nb, HEAD_DIM), lambda b, c, pt: (b, 0, 0, 0)),
                   _seq_spec(8, LANES)],
    )
    _, sel = pl.pallas_call(
        functools.partial(_moba_sel_kernel, pages=pages, nb=nb),
        grid_spec=grid_spec,
        out_shape=[jax.ShapeDtypeStruct((b, N_KV_HEADS, nb, HEAD_DIM), F32),
                   jax.ShapeDtypeStruct((b, 8, LANES), I32)],
        compiler_params=_params("parallel", "arbitrary"),
        name="moba_select_decode",
    )(page_table.reshape(-1), q128, *([cache_k] * pages))
    return sel


def _moba_dec_kernel(pt_ref, sel_ref, q_ref, kn_ref, vn_ref, bt_ref, *refs, nb):
    n = MOBA_TOPK * 2
    k_refs = refs[:n]
    v_refs = refs[n:2 * n]
    o_ref = refs[2 * n]
    b = pl.program_id(0)
    h = pl.program_id(1)
    head_rows = pl.ds(h // GROUP, PAGE_SIZE, stride=N_KV_HEADS)
    q = q_ref[...]
    bt = bt_ref[...]
    far = bt[:, 2 * PAGE_SIZE:3 * PAGE_SIZE]
    s_new = _new_token_logits(q, kn_ref[...]) + bt[:, 3 * PAGE_SIZE:3 * PAGE_SIZE + 1]

    s_piece = []
    for slot in range(MOBA_TOPK):
        last = sel_ref[(b * MOBA_TOPK + slot) * N_HEADS + h] == nb - 1
        for r in range(2):
            kp = k_refs[slot * 2 + r][head_rows, :].astype(BF16)
            bias = jnp.where(last, bt[:, r * PAGE_SIZE:(r + 1) * PAGE_SIZE], far)
            s_piece.append(_dot_nt(q, kp) + bias)
    mx = functools.reduce(jnp.maximum, s_piece)
    m = jnp.maximum(jnp.max(mx, axis=1, keepdims=True), s_new)
    p_piece = [jnp.exp(s - m) for s in s_piece]
    p_new = jnp.exp(s_new - m)
    l = jnp.sum(functools.reduce(jnp.add, p_piece), axis=1, keepdims=True) + p_new
    acc = p_new.astype(BF16).astype(F32) * _new_token_values(vn_ref[...])
    for i in range(n):
        acc = acc + _dot(p_piece[i].astype(BF16), v_refs[i][head_rows, :].astype(BF16))
    o = acc / l

    @pl.when(h == 0)
    def _():
        o_ref[...] = jnp.zeros(o_ref.shape, F32)

    rowi = lax.broadcasted_iota(I32, o.shape, 0)
    o_ref[pl.ds(h, 1), :] = jnp.sum(jnp.where(rowi == h, o, 0.0), axis=0, keepdims=True)


def moba_attention_decode(q16, kn, vn, bias_tab, sel, cache_k, cache_v, page_table, layer):
    b = q16.shape[0]
    n_pages = page_table.shape[1]
    nb = n_pages * PAGE_SIZE // MOBA_BLOCK

    def page_spec(slot, r):
        def index(b, h, pt, sel):
            blk = sel[(b * MOBA_TOPK + slot) * N_HEADS + h]
            return (layer, pt[b * n_pages + 2 * blk + r], 0, 0)
        return pl.BlockSpec((None, None, PAGE_ROWS, HEAD_DIM), index)

    kv_specs = [page_spec(slot, r) for slot in range(MOBA_TOPK) for r in range(2)]
    seq = lambda rows, width: pl.BlockSpec((None, rows, width), lambda b, h, pt, sel: (b, 0, 0))
    grid_spec = pltpu.PrefetchScalarGridSpec(
        num_scalar_prefetch=2,
        grid=(b, N_HEADS),
        in_specs=[seq(HEAD_ROWS, HEAD_DIM), seq(N_KV_HEADS, HEAD_DIM), seq(N_KV_HEADS, HEAD_DIM),
                  pl.BlockSpec((HEAD_ROWS, 4 * PAGE_SIZE), lambda b, h, pt, sel: (0, 0))] + kv_specs + kv_specs,
        out_specs=seq(HEAD_ROWS, HEAD_DIM),
    )
    n = len(kv_specs)
    return pl.pallas_call(
        functools.partial(_moba_dec_kernel, nb=nb),
        grid_spec=grid_spec,
        out_shape=jax.ShapeDtypeStruct((b, HEAD_ROWS, HEAD_DIM), F32),
        compiler_params=_params("parallel", "arbitrary"),
        name="moba_attention_decode",
    )(page_table.reshape(-1), sel.reshape(-1), q16, kn, vn, bias_tab, *([cache_k] * n), *([cache_v] * n))


def _rel_bias_by_distance(rel_bias, n):
    dist = jnp.arange(n, dtype=I32)
    max_exact = N_BUCKETS // 2
    large = max_exact + (jnp.log(jnp.maximum(dist, 1).astype(F32) / max_exact)
                         / math.log(MAX_DISTANCE / max_exact)
                         * (N_BUCKETS - max_exact)).astype(I32)
    large = jnp.minimum(large, N_BUCKETS - 1)
    return rel_bias[jnp.where(dist < max_exact, dist, large)]


def _pad_lanes(w, b):
    n = w.shape[1]
    return jnp.pad(w, ((0, 0), (0, LANES - n))), jnp.pad(b, (0, LANES - n)).reshape(1, LANES)


def _trunk(x, mod, cache, rel_bias, weights):
    (norm_g, final_g, w_in_b, w_out_b, fgates, w_router, b_router, wg_b, wu_b, wd_b) = weights
    b, t, d = x.shape
    m = b * t
    depth = w_in_b.shape[0]
    paged = cache is not None
    if paged:
        assert t == 1
        cache_k, cache_v, cache_lft, page_table = cache
        n_pages = page_table.shape[1]
        nb_past = n_pages * PAGE_SIZE // MOBA_BLOCK
        assert (n_pages * PAGE_SIZE) % MOBA_BLOCK == 0 and nb_past >= MOBA_TOPK
        tm_tok = tm_moe = m
        bias_d = _rel_bias_by_distance(rel_bias, 2 * MOBA_BLOCK + 1)
        assert MAX_DISTANCE <= MOBA_BLOCK
        near = bias_d[MOBA_BLOCK - jnp.arange(MOBA_BLOCK)].T
        tab = jnp.concatenate([near, jnp.broadcast_to(bias_d[2 * MOBA_BLOCK][:, None], (N_HEADS, PAGE_SIZE)),
                               jnp.broadcast_to(bias_d[0][:, None], (N_HEADS, PAGE_SIZE))], axis=1)
        bias_tab = jnp.pad(tab, ((0, HEAD_ROWS - N_HEADS), (0, 0)))
    else:
        assert t % MOBA_BLOCK == 0 and t // MOBA_BLOCK <= LANES
        tm_tok = min(512, t)
        tm_moe = min(1024, t)
        assert MAX_DISTANCE <= MOBA_BLOCK
        bias0, bias1 = rel_bias_tiles(rel_bias)

    q_scale = HEAD_DIM ** -0.5 * (1.0 if paged else LOG2E)
    wr, br = router_operands(w_router, b_router, transposed=tm_tok >= LANES)

    def mod_part(l, j):
        part = mod[l, :, j * d:(j + 1) * d]
        return part if paged else part.reshape(b, 1, d)

    xf = x.reshape(m, d)
    new_k, new_v, new_logf = [], [], []
    for l in range(depth):
        kind = l % N_MIXERS
        sh1, sc1, gt1, sh2, sc2, gt2 = [mod_part(l, j) for j in range(6)]
        fg = fgates[l // N_MIXERS] if kind == 0 else None
        outs = norm_qkv(xf, norm_g[l, 0].reshape(1, d), sc1, sh1, w_in_b[l], fg,
                        tm=tm_tok, rows_per_seq=t, q_scale=q_scale)
        q, k, v, kb, vb = outs[:5]
        new_k.append(k.reshape(b, t, N_KV_HEADS, HEAD_DIM))
        new_v.append(v.reshape(b, t, N_KV_HEADS, HEAD_DIM))
        if kind == 0:
            lf = outs[5]
            new_logf.append(lf[:, :N_HEADS].reshape(b, t, N_HEADS))
        if paged:
            q16 = jnp.pad(q.reshape(b, N_HEADS, HEAD_DIM), ((0, 0), (0, HEAD_ROWS - N_HEADS), (0, 0)))
            kn = k.reshape(b, N_KV_HEADS, HEAD_DIM)
            vn = v.reshape(b, N_KV_HEADS, HEAD_DIM)
            if kind == 0:
                lfn = jnp.broadcast_to(
                    jnp.pad(lf[:, :N_HEADS], ((0, 0), (0, HEAD_ROWS - N_HEADS)))[:, :, None],
                    (b, HEAD_ROWS, PAGE_SIZE))
                o16 = fox_attention_decode(q16, kn, vn, lfn, cache_k, cache_v, cache_lft, page_table,
                                           l, l // N_MIXERS)
            elif kind == 1:
                q128 = jnp.pad(q.reshape(b, N_HEADS, HEAD_DIM), ((0, 0), (0, LANES - N_HEADS), (0, 0)))
                sel = moba_select_decode(q128, cache_k, page_table, l)
                sel = sel[:, :MOBA_TOPK, :N_HEADS]
                o16 = moba_attention_decode(q16, kn, vn, bias_tab, sel, cache_k, cache_v, page_table, l)
            else:
                o16 = sb_attention_decode(q16, cache_k, cache_v, page_table, l)
            o = o16[:, :N_HEADS, :].reshape(m, Q_DIM).astype(BF16)
        else:
            q3 = q.reshape(b, t, Q_DIM)
            kb3 = kb.reshape(b, t, KV_DIM)
            vb3 = vb.reshape(b, t, KV_DIM)
            if kind == 0:
                o = fox_attention_prompt(q3, kb3, vb3, cumsum_time(lf.reshape(b, t, LANES)))
            elif kind == 1:
                kmean = block_mean(k.reshape(b, t * N_KV_HEADS, HEAD_DIM))
                o = moba_attention_prompt(q3, kb3, vb3, kmean, bias0, bias1)
            else:
                o = sb_attention_prompt(q3, kb3, vb3)
            o = o.reshape(m, Q_DIM)
        x1, h2, gates = out_proj_router(o, xf, w_out_b[l], gt1, norm_g[l, 1].reshape(1, d), sc2, sh2,
                                        wr, br, tm=tm_tok, rows_per_seq=t)
        fin = final_g.reshape(1, d) if l == depth - 1 else None
        xf = moe_ffn(h2, gates, x1, gt2, wg_b[l], wu_b[l], wd_b[l], fin, tm=tm_moe, rows_per_seq=t)
    return xf.reshape(b, t, d), jnp.stack(new_k), jnp.stack(new_v), jnp.stack(new_logf)


def kernel(x_prompt, x_sample, cache_k, cache_v, cache_logf, page_table, c_prompt, c_sample, rel_bias,
           w_ada, b_ada, norm_g, final_g, w_in, w_out, w_fgate, b_fgate, w_router, b_router,
           w_gate, w_up, w_down):
    n_prompt = c_prompt.shape[0]
    mod = ada_modulation(jnp.concatenate([c_prompt, c_sample], axis=0), w_ada, b_ada)

    fgates = []
    for a in range(w_fgate.shape[0]):
        wf, bf = _pad_lanes(w_fgate[a], b_fgate[a])
        fgates.append((wf.astype(BF16), bf))
    weights = (norm_g, final_g, w_in.astype(BF16), w_out.astype(BF16), fgates, w_router, b_router,
               w_gate.astype(BF16), w_up.astype(BF16), w_down.astype(BF16))

    depth, n_pool = cache_k.shape[:2]
    cache = (cache_k.reshape(depth, n_pool, PAGE_ROWS, HEAD_DIM),
             cache_v.reshape(depth, n_pool, PAGE_ROWS, HEAD_DIM),
             jnp.swapaxes(cache_logf, 2, 3), page_table)

    y_p, k_p, v_p, lf_p = _trunk(x_prompt, mod[:, :n_prompt], None, rel_bias, weights)
    y_s, k_s, v_s, lf_s = _trunk(x_sample, mod[:, n_prompt:], cache, rel_bias, weights)
    return (y_p, y_s, k_p, v_p, lf_p, k_s, v_s, lf_s)
```

```python
import functools
import math

import jax
import jax.numpy as jnp
from jax import lax
from jax.experimental import pallas as pl
from jax.experimental.pallas import tpu as pltpu

F32 = jnp.float32
BF16 = jnp.bfloat16
I32 = jnp.int32

N_MIXERS = 3
N_HEADS = 8
N_KV_HEADS = 4
GROUP = N_HEADS // N_KV_HEADS
HEAD_DIM = 128
Q_DIM = N_HEADS * HEAD_DIM
KV_DIM = N_KV_HEADS * HEAD_DIM
PAGE_SIZE = 128
MOBA_BLOCK = 256
MOBA_TOPK = 3
N_BUCKETS = 32
MAX_DISTANCE = 128
N_EXPERTS = 16
N_GROUPS = 4
EXPERTS_PER_GROUP = N_EXPERTS // N_GROUPS
TOP_K = 2
GROUP_LANE = N_EXPERTS
SORTED_ROWS = 256
RMS_EPS = 1e-6

PAGE_ROWS = PAGE_SIZE * N_KV_HEADS
LANES = 128
HEAD_ROWS = 16
MASKED = -1e30
LOG2E = math.log2(math.e)
VMEM_LIMIT = 56 * 1024 * 1024


def _params(*sem):
    return pltpu.CompilerParams(dimension_semantics=sem, vmem_limit_bytes=VMEM_LIMIT)


def _dot(a, b):
    return jnp.dot(a, b, preferred_element_type=F32)


def _dot_nt(a, b):
    return lax.dot_general(a, b, (((1,), (1,)), ((), ())), preferred_element_type=F32)


def _split2(x):
    hi = x.astype(BF16)
    lo = (x - hi.astype(F32)).astype(BF16)
    return hi, lo


def _split3(x):
    hi = x.astype(BF16)
    r = x - hi.astype(F32)
    mid = r.astype(BF16)
    lo = (r - mid.astype(F32)).astype(BF16)
    return hi, mid, lo


def _neg_softplus(z):
    return -(jnp.maximum(z, 0.0) + jnp.log1p(jnp.exp(-jnp.abs(z))))


def _silu(x):
    return x / (1.0 + jnp.exp(-x))


def _ada_kernel(c_ref, w_ref, b_ref, o_ref):
    s = _silu(c_ref[...])
    o_ref[...] = _dot(s.astype(BF16), w_ref[...].astype(BF16)) + b_ref[...]


def ada_modulation(c_all, w_ada, b_ada):
    depth, d, n = w_ada.shape
    mc = c_all.shape[0]
    tn = 1024
    return pl.pallas_call(
        _ada_kernel,
        grid=(depth, n // tn),
        in_specs=[
            pl.BlockSpec((mc, d), lambda l, j: (0, 0)),
            pl.BlockSpec((None, d, tn), lambda l, j: (l, 0, j)),
            pl.BlockSpec((None, 1, tn), lambda l, j: (l, 0, j)),
        ],
        out_specs=pl.BlockSpec((None, mc, tn), lambda l, j: (l, 0, j)),
        out_shape=jax.ShapeDtypeStruct((depth, mc, n), F32),
        compiler_params=_params("parallel", "parallel"),
        name="ada_modulation",
    )(c_all, w_ada, b_ada.reshape(depth, 1, n))


def _rms_mod(x, g, sc, sh):
    r = lax.rsqrt(jnp.mean(x * x, axis=-1, keepdims=True) + RMS_EPS)
    return (x * r * g) * (1.0 + sc) + sh


def _qkv_kernel(*refs, has_fgate, q_scale):
    if has_fgate:
        (x_ref, g_ref, sc_ref, sh_ref, w_ref, wf_ref, bf_ref,
         q_ref, k_ref, v_ref, kb_ref, vb_ref, lf_ref) = refs
    else:
        x_ref, g_ref, sc_ref, sh_ref, w_ref, q_ref, k_ref, v_ref, kb_ref, vb_ref = refs
    hb = _rms_mod(x_ref[...], g_ref[...], sc_ref[...], sh_ref[...]).astype(BF16)
    qkv = _dot(hb, w_ref[...])
    q_ref[...] = (qkv[:, :Q_DIM] * q_scale).astype(BF16)
    k = qkv[:, Q_DIM:Q_DIM + KV_DIM]
    v = qkv[:, Q_DIM + KV_DIM:]
    tm = k.shape[0]
    for kvh in range(N_KV_HEADS):
        rows = pl.ds(kvh, tm, stride=N_KV_HEADS)
        k_ref[rows, :] = k[:, kvh * HEAD_DIM:(kvh + 1) * HEAD_DIM]
        v_ref[rows, :] = v[:, kvh * HEAD_DIM:(kvh + 1) * HEAD_DIM]
    kb_ref[...] = k.astype(BF16)
    vb_ref[...] = v.astype(BF16)
    if has_fgate:
        z = _dot(hb, wf_ref[...]) + bf_ref[...]
        lf_ref[...] = jnp.minimum(z, 0.0) - jnp.log1p(jnp.exp(-jnp.abs(z)))


def _mod_spec(mod, tm, rows_per_seq):
    d = mod.shape[-1]
    if mod.ndim == 3:
        tiles = rows_per_seq // tm
        return pl.BlockSpec((None, 1, d), lambda i: (i // tiles, 0, 0))
    return pl.BlockSpec((tm, d), lambda i: (i, 0))


def norm_qkv(x, g, sc, sh, w_in_b, fgate, *, tm, rows_per_seq, q_scale):
    m, d = x.shape
    n = w_in_b.shape[1]
    has_fgate = fgate is not None
    row = lambda i: (i, 0)
    const = lambda i: (0, 0)
    in_specs = [pl.BlockSpec((tm, d), row), pl.BlockSpec((1, d), const),
                _mod_spec(sc, tm, rows_per_seq), _mod_spec(sh, tm, rows_per_seq),
                pl.BlockSpec((d, n), const)]
    args = [x, g, sc, sh, w_in_b]
    out_specs = [pl.BlockSpec((tm, Q_DIM), row), pl.BlockSpec((tm * N_KV_HEADS, HEAD_DIM), row),
                 pl.BlockSpec((tm * N_KV_HEADS, HEAD_DIM), row), pl.BlockSpec((tm, KV_DIM), row),
                 pl.BlockSpec((tm, KV_DIM), row)]
    out_shape = [jax.ShapeDtypeStruct((m, Q_DIM), BF16),
                 jax.ShapeDtypeStruct((m * N_KV_HEADS, HEAD_DIM), F32),
                 jax.ShapeDtypeStruct((m * N_KV_HEADS, HEAD_DIM), F32),
                 jax.ShapeDtypeStruct((m, KV_DIM), BF16), jax.ShapeDtypeStruct((m, KV_DIM), BF16)]
    if has_fgate:
        wf, bf = fgate
        in_specs += [pl.BlockSpec((d, LANES), const), pl.BlockSpec((1, LANES), const)]
        args += [wf, bf]
        out_specs.append(pl.BlockSpec((tm, LANES), row))
        out_shape.append(jax.ShapeDtypeStruct((m, LANES), F32))
    return pl.pallas_call(
        functools.partial(_qkv_kernel, has_fgate=has_fgate, q_scale=q_scale),
        grid=(m // tm,),
        in_specs=in_specs, out_specs=out_specs, out_shape=out_shape,
        compiler_params=_params("parallel"),
        name="norm_qkv",
    )(*args)


def _cumsum_kernel(lf_ref, dt_ref, *, chunk):
    t = lf_ref.shape[0]
    row = lax.broadcasted_iota(I32, (chunk, chunk), 0)
    col = lax.broadcasted_iota(I32, (chunk, chunk), 1)
    tri = jnp.where(col <= row, 1.0, 0.0).astype(BF16)
    carry = jnp.zeros((1, LANES), F32)
    for c in range(t // chunk):
        sl = slice(c * chunk, (c + 1) * chunk)
        hi, mid, lo = _split3(lf_ref[sl, :])
        cs = _dot(tri, hi) + _dot(tri, mid) + _dot(tri, lo) + carry
        dt_ref[:, sl] = cs.T[:N_HEADS, :]
        carry = cs[chunk - 1:chunk, :]


def cumsum_time(lf):
    b, t, _ = lf.shape
    chunk = min(256, t)
    return pl.pallas_call(
        functools.partial(_cumsum_kernel, chunk=chunk),
        grid=(b,),
        in_specs=[pl.BlockSpec((None, t, LANES), lambda i: (i, 0, 0))],
        out_specs=pl.BlockSpec((None, N_HEADS, t), lambda i: (i, 0, 0)),
        out_shape=jax.ShapeDtypeStruct((b, N_HEADS, t), F32),
        compiler_params=_params("parallel"),
        name="cumsum_time",
    )(lf)


def _lane_pick(x, idx):
    lane = lax.broadcasted_iota(I32, x.shape, 1)
    return jnp.sum(jnp.where(lane == idx, x, 0.0), axis=1, keepdims=True)


def _with_ones(v):
    return jnp.concatenate([v, jnp.ones(v.shape, v.dtype)], axis=1)


def _softmax2_first(s, v1):
    m = jnp.max(s, axis=1, keepdims=True)
    return m, _dot(jnp.exp2(s - m).astype(BF16), v1)


def _softmax2_step(s, v1, m, accl, keep=None):
    m_new = jnp.maximum(m, jnp.max(s, axis=1, keepdims=True))
    new = accl * jnp.exp2(m - m_new) + _dot(jnp.exp2(s - m_new).astype(BF16), v1)
    if keep is None:
        return m_new, new
    return jnp.where(keep, m_new, m), jnp.where(keep, new, accl)


def _softmax2_out(accl):
    return (accl[:, :HEAD_DIM] / accl[:, HEAD_DIM:HEAD_DIM + 1]).astype(BF16)


def _fox_kernel(q_ref, k_ref, v_ref, dt_ref, o_ref, *, tq):
    kvh = pl.program_id(1)
    i = pl.program_id(2)
    q0 = pl.multiple_of(i * tq, tq)
    row = lax.broadcasted_iota(I32, (tq, tq), 0)
    col = lax.broadcasted_iota(I32, (tq, tq), 1)
    causal = col <= row
    heads = [kvh * GROUP + g for g in range(GROUP)]
    qs = [q_ref[:, g * HEAD_DIM:(g + 1) * HEAD_DIM] for g in range(GROUP)]
    d0s = [dt_ref[pl.ds(h, 1), pl.ds(q0, tq)][:, 0:1] for h in heads]

    def logits(g, start, kt):
        dk = dt_ref[pl.ds(heads[g], 1), pl.ds(start, tq)]
        return _dot_nt(qs[g], kt) + (d0s[g] - dk) * LOG2E

    def step(s, vt, m, l, acc):
        m_new = jnp.maximum(m, jnp.max(s, axis=1, keepdims=True))
        alpha = jnp.exp2(m - m_new)
        p = jnp.exp2(s - m_new)
        return m_new, l * alpha + jnp.sum(p, axis=1, keepdims=True), acc * alpha + _dot(p.astype(BF16), vt)

    kt, vt = k_ref[pl.ds(q0, tq), :], v_ref[pl.ds(q0, tq), :]
    zero = (jnp.full((tq, 1), MASKED, F32), jnp.zeros((tq, 1), F32), jnp.zeros((tq, HEAD_DIM), F32))
    state = tuple(step(jnp.where(causal, logits(g, q0, kt), MASKED), vt, *zero) for g in range(GROUP))

    def body(j, state):
        start = pl.multiple_of(j * tq, tq)
        kt, vt = k_ref[pl.ds(start, tq), :], v_ref[pl.ds(start, tq), :]
        return tuple(step(logits(g, start, kt), vt, *state[g]) for g in range(GROUP))

    state = lax.fori_loop(0, i, body, state)
    for g, (_, l, acc) in enumerate(state):
        o_ref[:, g * HEAD_DIM:(g + 1) * HEAD_DIM] = (acc / l).astype(BF16)


def _attn_specs(t, tq):
    qo = pl.BlockSpec((None, tq, GROUP * HEAD_DIM), lambda b, kvh, i: (b, i, kvh))
    kv = pl.BlockSpec((None, t, HEAD_DIM), lambda b, kvh, i: (b, 0, kvh))
    return qo, kv


def fox_attention_prompt(q, kb, vb, dt):
    b, t, _ = q.shape
    tq = min(512, t)
    qo, kv = _attn_specs(t, tq)
    return pl.pallas_call(
        functools.partial(_fox_kernel, tq=tq),
        grid=(b, N_KV_HEADS, t // tq),
        in_specs=[qo, kv, kv, pl.BlockSpec((None, N_HEADS, t), lambda b, kvh, i: (b, 0, 0))],
        out_specs=qo,
        out_shape=jax.ShapeDtypeStruct((b, t, Q_DIM), BF16),
        compiler_params=_params("parallel", "parallel", "parallel"),
        name="fox_attention_prompt",
    )(q, kb, vb, dt)


def _suffix_matrices(n):
    row = lax.broadcasted_iota(I32, (n, n), 0)
    col = lax.broadcasted_iota(I32, (n, n), 1)
    return jnp.where(row > col, 1.0, 0.0).astype(BF16), jnp.ones((n, n), BF16)


def _sb_kernel(q_ref, k_ref, v_ref, o_ref, *, tq, tk):
    i = pl.program_id(2)
    q0 = pl.multiple_of(i * tq, tq)
    upper, _ = _suffix_matrices(tk)
    row = lax.broadcasted_iota(I32, (tq, tk), 0)
    col = lax.broadcasted_iota(I32, (tq, tk), 1)
    qs = [q_ref[:, g * HEAD_DIM:(g + 1) * HEAD_DIM] for g in range(GROUP)]

    def chunk(start, state, mask, r0=0):
        kt, vt = k_ref[pl.ds(start, tk), :], v_ref[pl.ds(start, tk), :]
        out = []
        for g in range(GROUP):
            c, acc = state[g]
            z = _dot_nt(qs[g][r0:], kt)
            ls = jnp.minimum(z, 0.0) - jnp.log2(1.0 + jnp.exp2(jnp.minimum(z, -z)))
            lk = ls - z
            if mask is not None:
                lk = jnp.where(mask[r0:], lk, 0.0)
            hi, lo = _split2(lk)
            after = _dot(hi, upper) + _dot(lo, upper) + c[r0:]
            a = jnp.exp2(ls + after)
            if mask is not None:
                a = jnp.where(mask[r0:], a, 0.0)
            c_new = c[r0:] + jnp.sum(lk, axis=1, keepdims=True)
            acc_new = acc[r0:] + _dot(a.astype(BF16), vt)
            if r0:
                c_new = jnp.concatenate([c[:r0], c_new], axis=0)
                acc_new = jnp.concatenate([acc[:r0], acc_new], axis=0)
            out.append((c_new, acc_new))
        return tuple(out)

    state = tuple((jnp.zeros((tq, 1), F32), jnp.zeros((tq, HEAD_DIM), F32)) for _ in range(GROUP))
    for mth in reversed(range(tq // tk)):
        start = pl.multiple_of(q0 + mth * tk, tk)
        state = chunk(start, state, (col + mth * tk) < row, r0=mth * tk)

    n_past = i * (tq // tk)

    def body(it, state):
        return chunk(pl.multiple_of((n_past - 1 - it) * tk, tk), state, None)

    state = lax.fori_loop(0, n_past, body, state)
    for g, (_, acc) in enumerate(state):
        o_ref[:, g * HEAD_DIM:(g + 1) * HEAD_DIM] = acc.astype(BF16)


def sb_attention_prompt(q, kb, vb):
    b, t, _ = q.shape
    tq = min(1024, t)
    tk = min(256, t)
    qo, kv = _attn_specs(t, tq)
    return pl.pallas_call(
        functools.partial(_sb_kernel, tq=tq, tk=tk),
        grid=(b, N_KV_HEADS, t // tq),
        in_specs=[qo, kv, kv],
        out_specs=qo,
        out_shape=jax.ShapeDtypeStruct((b, t, Q_DIM), BF16),
        compiler_params=_params("parallel", "parallel", "parallel"),
        name="sb_attention_prompt",
    )(q, kb, vb)


def _block_mean_kernel(k_ref, o_ref, *, nb):
    t = k_ref.shape[0] // N_KV_HEADS
    o_ref[...] = jnp.zeros(o_ref.shape, F32)
    for kvh in range(N_KV_HEADS):
        x = k_ref[pl.ds(kvh, t, stride=N_KV_HEADS), :]
        o_ref[kvh, 0:nb, :] = jnp.sum(x.reshape(nb, t // nb, HEAD_DIM), axis=1) * (1.0 / (t // nb))


def block_mean(k4):
    b, rows, _ = k4.shape
    nb = rows // N_KV_HEADS // MOBA_BLOCK
    return pl.pallas_call(
        functools.partial(_block_mean_kernel, nb=nb),
        grid=(b,),
        in_specs=[pl.BlockSpec((None, rows, HEAD_DIM), lambda i: (i, 0, 0))],
        out_specs=pl.BlockSpec((None, N_KV_HEADS, LANES, HEAD_DIM), lambda i: (i, 0, 0, 0)),
        out_shape=jax.ShapeDtypeStruct((b, N_KV_HEADS, LANES, HEAD_DIM), F32),
        compiler_params=_params("parallel"),
        name="block_mean",
    )(k4)


def _top_mask(gate, n_valid, n_cand, axis):
    idx = lax.broadcasted_iota(I32, gate.shape, axis)
    cnt = jnp.zeros(gate.shape, I32)
    for jp in range(n_cand):
        other = gate[:, jp:jp + 1] if axis == 1 else gate[jp:jp + 1, :]
        beats = (other > gate) | ((other == gate) & (jp < idx))
        cnt = cnt + jnp.where(beats, jnp.where(jp < n_valid, 1, 0), 0)
    return (idx < n_valid) & (cnt < MOBA_TOPK)


def _moba_kernel(q_ref, k_ref, v_ref, km_ref, b0_ref, b1_ref, o_ref, *, tq, nb):
    i = pl.program_id(2)
    q0 = pl.multiple_of(i * tq, tq)
    row = lax.broadcasted_iota(I32, (tq, tq), 0)
    col = lax.broadcasted_iota(I32, (tq, tq), 1)
    causal = col <= row
    km_hi, km_lo = _split2(km_ref[...])
    qs = [q_ref[:, g * HEAD_DIM:(g + 1) * HEAD_DIM] for g in range(GROUP)]
    sels = [jnp.where(_top_mask(_dot_nt(qh, km_hi) + _dot_nt(qh, km_lo), i, nb, 1), 1.0, 0.0) for qh in qs]

    def tile(j):
        start = pl.multiple_of(j * tq, tq)
        return k_ref[pl.ds(start, tq), :], _with_ones(v_ref[pl.ds(start, tq), :])

    def picked(g, j):
        return _lane_pick(sels[g], j) > 0.0

    kt, v1 = tile(i)
    state = tuple(_softmax2_first(jnp.where(causal, _dot_nt(qs[g], kt) + b0_ref[g], MASKED), v1)
                  for g in range(GROUP))
    prev = jnp.maximum(i - 1, 0)
    kt, v1 = tile(prev)
    state = tuple(_softmax2_step(_dot_nt(qs[g], kt) + b1_ref[g], v1, *state[g], keep=picked(g, i - 1))
                  for g in range(GROUP))

    def body(j, state):
        kt, v1 = tile(j)
        return tuple(_softmax2_step(_dot_nt(qs[g], kt), v1, *state[g], keep=picked(g, j))
                     for g in range(GROUP))

    state = lax.fori_loop(0, i - 1, body, state)
    for g, (_, accl) in enumerate(state):
        o_ref[:, g * HEAD_DIM:(g + 1) * HEAD_DIM] = _softmax2_out(accl)


def _bias_tile_kernel(rb_ref, b0_ref, b1_ref, *, n):
    h = pl.program_id(0)
    row = lax.broadcasted_iota(I32, (n, n), 0)
    col = lax.broadcasted_iota(I32, (n, n), 1)
    max_exact = N_BUCKETS // 2
    for out_ref, shift in ((b0_ref, 0), (b1_ref, n)):
        dist = jnp.maximum(row - col + shift, 0)
        large = max_exact + (jnp.log(jnp.maximum(dist, 1).astype(F32) / max_exact)
                             / math.log(MAX_DISTANCE / max_exact) * (N_BUCKETS - max_exact)).astype(I32)
        bucket = jnp.where(dist < max_exact, dist, jnp.minimum(large, N_BUCKETS - 1))
        acc = jnp.zeros((n, n), F32)
        for bk in range(N_BUCKETS):
            acc = jnp.where(bucket == bk, rb_ref[bk, h], acc)
        out_ref[...] = (acc - rb_ref[N_BUCKETS - 1, h]) * LOG2E


def rel_bias_tiles(rel_bias):
    n = MOBA_BLOCK
    shape = jax.ShapeDtypeStruct((N_HEADS, n, n), F32)
    spec = pl.BlockSpec((None, n, n), lambda h: (h, 0, 0))
    return pl.pallas_call(
        functools.partial(_bias_tile_kernel, n=n),
        grid=(N_HEADS,),
        in_specs=[pl.BlockSpec(memory_space=pltpu.SMEM)],
        out_specs=[spec, spec],
        out_shape=[shape, shape],
        compiler_params=_params("parallel"),
        name="rel_bias_tiles",
    )(rel_bias)


def moba_attention_prompt(q, kb, vb, kmean, bias0, bias1):
    b, t, _ = q.shape
    tq = MOBA_BLOCK
    nb = t // tq
    qo, kv = _attn_specs(t, tq)
    head_pair = lambda b, kvh, i: (kvh, 0, 0)
    return pl.pallas_call(
        functools.partial(_moba_kernel, tq=tq, nb=nb),
        grid=(b, N_KV_HEADS, nb),
        in_specs=[qo, kv, kv,
                  pl.BlockSpec((None, None, LANES, HEAD_DIM), lambda b, kvh, i: (b, kvh, 0, 0)),
                  pl.BlockSpec((GROUP, tq, tq), head_pair),
                  pl.BlockSpec((GROUP, tq, tq), head_pair)],
        out_specs=qo,
        out_shape=jax.ShapeDtypeStruct((b, t, Q_DIM), BF16),
        compiler_params=_params("parallel", "parallel", "parallel"),
        name="moba_attention_prompt",
    )(q, kb, vb, kmean, bias0, bias1)


def _route(logits):
    idx = lax.broadcasted_iota(I32, logits.shape, 1)
    valid = idx < N_EXPERTS
    lg = jnp.where(valid, logits, MASKED)
    e = jnp.where(valid, jnp.exp(lg - jnp.max(lg, axis=1, keepdims=True)), 0.0)
    grp = idx >> 2

    def peers(x, shifts):
        for s in shifts:
            for sh in (s, LANES - s):
                oi = pltpu.roll(idx, sh, 1)
                yield pltpu.roll(x, sh, 1), oi, oi < N_EXPERTS

    cnt = jnp.zeros(logits.shape, I32)
    for oe, oi, ok in peers(e, (1, 2, 3)):
        beats = ok & ((oi >> 2) == grp) & ((oe > e) | ((oe == e) & (oi < idx)))
        cnt = cnt + jnp.where(beats, 1, 0)
    top2 = valid & (cnt < TOP_K)
    t2e = jnp.where(top2, e, 0.0)
    score = t2e
    for ot, oi, ok in peers(t2e, (1, 2, 3)):
        score = score + jnp.where(ok & ((oi >> 2) == grp), ot, 0.0)
    lost = jnp.zeros(logits.shape, I32)
    for osc, oi, ok in peers(score, (4, 8, 12)):
        beats = ok & ((osc > score) | ((osc == score) & ((oi >> 2) < grp)))
        lost = lost + jnp.where(beats, 1, 0)
    return jnp.where(top2 & valid & (lost == 0), e / score, 0.0)


def _route_t(lg):
    npos, ngrp = EXPERTS_PER_GROUP, N_GROUPS
    e = jnp.exp(lg - jnp.max(lg, axis=0, keepdims=True))
    pos = [jnp.concatenate([e[g * npos + a:g * npos + a + 1, :] for g in range(ngrp)], axis=0)
           for a in range(npos)]
    top2 = []
    for a in range(npos):
        cnt = jnp.zeros(pos[a].shape, I32)
        for b in range(npos):
            if b != a:
                cnt = cnt + jnp.where((pos[b] >= pos[a]) if b < a else (pos[b] > pos[a]), 1, 0)
        top2.append(cnt < TOP_K)
    score = functools.reduce(jnp.add, [jnp.where(t, p, 0.0) for t, p in zip(top2, pos)])
    rows = [score[g:g + 1, :] for g in range(ngrp)]
    won = []
    for g in range(ngrp):
        lost = jnp.zeros(rows[g].shape, I32)
        for o in range(ngrp):
            if o != g:
                lost = lost + jnp.where((rows[o] >= rows[g]) if o < g else (rows[o] > rows[g]), 1, 0)
        won.append(jnp.where(lost == 0, 1.0, 0.0))
    group_id = functools.reduce(jnp.add, [won[g] * float(g) for g in range(ngrp)])
    won = jnp.concatenate(won, axis=0) > 0.0
    gate = [jnp.where(t & won, p / score, 0.0) for t, p in zip(top2, pos)]
    gates = jnp.concatenate([gate[a][g:g + 1, :] for g in range(ngrp) for a in range(npos)], axis=0)
    return gates, group_id


def _oproj_kernel(o_ref, x_ref, w_ref, gt_ref, g_ref, sc_ref, sh_ref, wr_ref, br_ref,
                  x1_ref, h2_ref, gates_ref, *, transposed):
    x1 = x_ref[...] + gt_ref[...] * _dot(o_ref[...], w_ref[...])
    x1_ref[...] = x1
    h2 = _rms_mod(x1, g_ref[...], sc_ref[...], sh_ref[...])
    hi, lo = _split2(h2)
    h2_ref[...] = hi
    if transposed:
        both = _dot_nt(wr_ref[...], hi)
        lg = both[:N_EXPERTS] + both[N_EXPERTS:] + _dot_nt(wr_ref[0:N_EXPERTS, :], lo) + br_ref[:, 0:1]
        gates, group_id = _route_t(lg)
        pad = jnp.zeros((LANES - N_EXPERTS - 1, gates.shape[1]), F32)
        gates_ref[...] = jnp.concatenate([gates, group_id, pad], axis=0).T
    else:
        both = _dot(hi, wr_ref[...])
        logits = both[:, :LANES] + both[:, LANES:] + _dot(lo, wr_ref[:, 0:LANES]) + br_ref[...]
        gates_ref[...] = _route(logits)


def router_operands(w_router, b_router, transposed):
    if transposed:
        hi, lo = _split2(w_router.T)
        return jnp.concatenate([hi, lo], axis=0), jnp.broadcast_to(b_router[:, None], (N_EXPERTS, LANES))
    wr, br = _pad_lanes(w_router, b_router)
    hi, lo = _split2(wr)
    return jnp.concatenate([hi, lo], axis=1), br


def out_proj_router(o, x, w_out_b, gt, g2, sc, sh, wr, br, *, tm, rows_per_seq):
    m, d = x.shape
    transposed = wr.shape[1] == d
    row = lambda i: (i, 0)
    const = lambda i: (0, 0)
    ms = lambda a: _mod_spec(a, tm, rows_per_seq)
    return pl.pallas_call(
        functools.partial(_oproj_kernel, transposed=transposed),
        grid=(m // tm,),
        in_specs=[pl.BlockSpec((tm, Q_DIM), row), pl.BlockSpec((tm, d), row),
                  pl.BlockSpec((Q_DIM, d), const), ms(gt), pl.BlockSpec((1, d), const),
                  ms(sc), ms(sh), pl.BlockSpec(wr.shape, const), pl.BlockSpec(br.shape, const)],
        out_specs=[pl.BlockSpec((tm, d), row), pl.BlockSpec((tm, d), row),
                   pl.BlockSpec((tm, LANES), row)],
        out_shape=[jax.ShapeDtypeStruct((m, d), F32), jax.ShapeDtypeStruct((m, d), BF16),
                   jax.ShapeDtypeStruct((m, LANES), F32)],
        compiler_params=_params("parallel"),
        name="out_proj_router",
    )(o, x, w_out_b, gt, g2, sc, sh, wr, br)


def _moe_kernel(h_ref, gates_ref, x_ref, gt_ref, wg_ref, wu_ref, wd_ref, *rest, epc, final):
    fg_ref, o_ref, acc_ref = rest if final else (None, *rest)
    c = pl.program_id(1)

    @pl.when(c == 0)
    def _():
        acc_ref[...] = jnp.zeros(acc_ref.shape, F32)

    h = h_ref[...]
    gates = gates_ref[...]
    acts = []
    for j in range(epc):
        gcol = _lane_pick(gates, c * epc + j)
        acts.append((_silu(_dot(h, wg_ref[j])) * _dot(h, wu_ref[j]) * gcol).astype(BF16))
    wd = wd_ref[...]
    acc_ref[...] += _dot(jnp.concatenate(acts, axis=1), wd.reshape(wd.shape[0] * wd.shape[1], wd.shape[2]))

    @pl.when(c == pl.num_programs(1) - 1)
    def _():
        x = x_ref[...] + gt_ref[...] * acc_ref[...]
        if fg_ref is not None:
            x = x * lax.rsqrt(jnp.mean(x * x, axis=-1, keepdims=True) + RMS_EPS) * fg_ref[...]
        o_ref[...] = x


def moe_ffn(h2, gates, x1, gt, wg_b, wu_b, wd_b, final_g, *, tm, rows_per_seq):
    m, d = x1.shape
    n_e, _, de = wg_b.shape
    epc = 4
    row = lambda i, c: (i, 0)
    gt_spec = _mod_spec(gt, tm, rows_per_seq)
    gt_spec = pl.BlockSpec(gt_spec.block_shape, lambda i, c, f=gt_spec.index_map: f(i))
    in_specs = [pl.BlockSpec((tm, d), row), pl.BlockSpec((tm, LANES), row),
                pl.BlockSpec((tm, d), row), gt_spec,
                pl.BlockSpec((epc, d, de), lambda i, c: (c, 0, 0)),
                pl.BlockSpec((epc, d, de), lambda i, c: (c, 0, 0)),
                pl.BlockSpec((epc, de, d), lambda i, c: (c, 0, 0))]
    args = [h2, gates, x1, gt, wg_b, wu_b, wd_b]
    if final_g is not None:
        in_specs.append(pl.BlockSpec((1, d), lambda i, c: (0, 0)))
        args.append(final_g)
    return pl.pallas_call(
        functools.partial(_moe_kernel, epc=epc, final=final_g is not None),
        grid=(m // tm, n_e // epc),
        in_specs=in_specs,
        out_specs=pl.BlockSpec((tm, d), row),
        out_shape=jax.ShapeDtypeStruct((m, d), F32),
        scratch_shapes=[pltpu.VMEM((tm, d), F32)],
        compiler_params=_params("parallel", "arbitrary"),
        name="moe_ffn",
    )(*args)


def _moe_sorted_kernel(cnt_ref, h_ref, route_ref, x_ref, gt_ref, wg_ref, wu_ref, wd_ref, *rest, final):
    fg_ref, o_ref, acc_ref, posc_ref, posr_ref = rest if final else (None, *rest)
    i = pl.program_id(0)
    g = pl.program_id(1)
    tm = h_ref.shape[0]
    route = route_ref[...]

    @pl.when(g == 0)
    def _():
        acc_ref[...] = jnp.zeros(acc_ref.shape, F32)
        lane = lax.broadcasted_iota(I32, (tm, LANES), 1)
        gid = route[:, GROUP_LANE:GROUP_LANE + 1]
        onehot = jnp.where(lane.astype(F32) == gid, 1.0, 0.0)
        r = lax.broadcasted_iota(I32, (tm, tm), 0)
        c = lax.broadcasted_iota(I32, (tm, tm), 1)
        earlier = jnp.where(c < r, 1.0, 0.0).astype(BF16)
        rank = jnp.sum(onehot * _dot(earlier, onehot.astype(BF16)), axis=1, keepdims=True)
        posc = jnp.where(lane == 0, gid, jnp.where(lane == 1, rank, 0.0))
        posc_ref[...] = posc
        posr_ref[...] = posc.T

    n_rows = cnt_ref[i * N_GROUPS + g]
    gf = g.astype(F32)
    pos_col = jnp.where(posc_ref[:, 0:1] == gf, posc_ref[:, 1:2], -1.0)
    pos_row = jnp.where(posr_ref[0:1, :] == gf, posr_ref[1:2, :], -1.0)
    r_hi, r_lo = _split2(route)
    wd = wd_ref[...]
    wd = wd.reshape(wd.shape[0] * wd.shape[1], wd.shape[2])
    for chunk in range(tm // SORTED_ROWS):
        @pl.when(n_rows > chunk * SORTED_ROWS)
        def _(base=float(chunk * SORTED_ROWS)):
            rid = lax.broadcasted_iota(I32, (SORTED_ROWS, tm), 0).astype(F32) + base
            pack = jnp.where(pos_row == rid, 1.0, 0.0).astype(BF16)
            xs = _dot(pack, h_ref[...]).astype(BF16)
            gs = _dot(pack, r_hi) + _dot(pack, r_lo)
            acts = []
            for j in range(EXPERTS_PER_GROUP):
                gcol = _lane_pick(gs, g * EXPERTS_PER_GROUP + j)
                acts.append((_silu(_dot(xs, wg_ref[j])) * _dot(xs, wu_ref[j]) * gcol).astype(BF16))
            ys = _dot(jnp.concatenate(acts, axis=1), wd)
            cid = lax.broadcasted_iota(I32, (tm, SORTED_ROWS), 1).astype(F32) + base
            unpack = jnp.where(pos_col == cid, 1.0, 0.0).astype(BF16)
            acc_ref[...] += _dot(unpack, ys.astype(BF16))

    @pl.when(g == pl.num_programs(1) - 1)
    def _():
        x = x_ref[...] + gt_ref[...] * acc_ref[...]
        if fg_ref is not None:
            x = x * lax.rsqrt(jnp.mean(x * x, axis=-1, keepdims=True) + RMS_EPS) * fg_ref[...]
        o_ref[...] = x


def moe_ffn_sorted(h2, route, x1, gt, wg_b, wu_b, wd_b, final_g, *, tm, rows_per_seq):
    m, d = x1.shape
    _, _, de = wg_b.shape
    npos = EXPERTS_PER_GROUP
    n_tiles = m // tm
    group = route[:, GROUP_LANE].reshape(n_tiles, 1, tm)
    counts = jnp.sum(group == jnp.arange(N_GROUPS, dtype=F32)[None, :, None], axis=2, dtype=I32)
    row = lambda i, g, cnt: (i, 0)
    gt_spec = _mod_spec(gt, tm, rows_per_seq)
    gt_spec = pl.BlockSpec(gt_spec.block_shape, lambda i, g, cnt, f=gt_spec.index_map: f(i))
    in_specs = [pl.BlockSpec((tm, d), row), pl.BlockSpec((tm, LANES), row),
                pl.BlockSpec((tm, d), row), gt_spec,
                pl.BlockSpec((npos, d, de), lambda i, g, cnt: (g, 0, 0)),
                pl.BlockSpec((npos, d, de), lambda i, g, cnt: (g, 0, 0)),
                pl.BlockSpec((npos, de, d), lambda i, g, cnt: (g, 0, 0))]
    args = [h2, route, x1, gt, wg_b, wu_b, wd_b]
    if final_g is not None:
        in_specs.append(pl.BlockSpec((1, d), lambda i, g, cnt: (0, 0)))
        args.append(final_g)
    grid_spec = pltpu.PrefetchScalarGridSpec(
        num_scalar_prefetch=1,
        grid=(n_tiles, N_GROUPS),
        in_specs=in_specs,
        out_specs=pl.BlockSpec((tm, d), row),
        scratch_shapes=[pltpu.VMEM((tm, d), F32), pltpu.VMEM((tm, LANES), F32), pltpu.VMEM((LANES, tm), F32)],
    )
    return pl.pallas_call(
        functools.partial(_moe_sorted_kernel, final=final_g is not None),
        grid_spec=grid_spec,
        out_shape=jax.ShapeDtypeStruct((m, d), F32),
        compiler_params=_params("parallel", "arbitrary"),
        name="moe_ffn_sorted",
    )(counts.reshape(-1), *args)


def _head_rows(mats):
    rowi = lax.broadcasted_iota(I32, mats[0].shape, 0)
    out = mats[0]
    for kvh in range(1, N_KV_HEADS):
        out = jnp.where((rowi >> 1) == kvh, mats[kvh], out)
    return out


def _kv_rows(page_ref):
    return [page_ref[pl.ds(kvh, PAGE_SIZE, stride=N_KV_HEADS), :] for kvh in range(N_KV_HEADS)]


def _new_token_logits(q, kn):
    qf = q.astype(F32)
    prods = [qf * kn[kvh:kvh + 1, :].astype(BF16).astype(F32) for kvh in range(N_KV_HEADS)]
    return jnp.sum(_head_rows(prods), axis=1, keepdims=True)


def _new_token_values(vn):
    return _head_rows([jnp.broadcast_to(vn[kvh:kvh + 1, :].astype(BF16).astype(F32), (HEAD_ROWS, HEAD_DIM))
                       for kvh in range(N_KV_HEADS)])


def _page_spec(layer, n_pages, page_of):
    return pl.BlockSpec((None, None, PAGE_ROWS, HEAD_DIM),
                        lambda b, c, pt: (layer, pt[b * n_pages + page_of(c)], 0, 0))


def _seq_spec(rows, width):
    return pl.BlockSpec((None, rows, width), lambda b, c, pt: (b, 0, 0))


def _suffix_total_matrix(n):
    upper, ones = _suffix_matrices(n)
    return jnp.concatenate([upper, ones], axis=1)


def _suffix_and_total(x, suffix_total):
    hi, lo = _split2(x)
    both = _dot(hi, suffix_total) + _dot(lo, suffix_total)
    return both[:, :PAGE_SIZE], both[:, PAGE_SIZE:]


def _page_logits(q, k_refs):
    return [_head_rows([_dot_nt(q, ks.astype(BF16)) for ks in _kv_rows(ref)]) for ref in k_refs]


def _page_values(w_pages, v_refs):
    pv = [jnp.zeros((HEAD_ROWS, HEAD_DIM), F32)] * N_KV_HEADS
    for w, ref in zip(w_pages, v_refs):
        wb = w.astype(BF16)
        pv = [a + _dot(wb, vs.astype(BF16)) for a, vs in zip(pv, _kv_rows(ref))]
    return _head_rows(pv)


def _carry_back(car, sufs, tots):
    after = [None] * len(sufs)
    for r in reversed(range(len(sufs))):
        after[r] = car + sufs[r]
        car = car + tots[r]
    return after, car


def _fox_dec_kernel(pt_ref, q_ref, kn_ref, vn_ref, lfn_ref, *refs, pages):
    k_refs = refs[:pages]
    v_refs = refs[pages:2 * pages]
    lf_refs = refs[2 * pages:3 * pages]
    o_ref, m_s, l_s, acc_s, car_s = refs[3 * pages:]
    c = pl.program_id(1)
    q = q_ref[...]
    suffix_total = _suffix_total_matrix(PAGE_SIZE)

    @pl.when(c == 0)
    def _():
        m_s[...] = jnp.broadcast_to(_new_token_logits(q, kn_ref[...]), m_s.shape)
        l_s[...] = jnp.ones(l_s.shape, F32)
        acc_s[...] = _new_token_values(vn_ref[...])
        car_s[...] = lfn_ref[...]

    pad = jnp.zeros((HEAD_ROWS - N_HEADS, PAGE_SIZE), F32)
    st = [_suffix_and_total(jnp.concatenate([ref[...], pad], axis=0), suffix_total) for ref in lf_refs]
    qk = _page_logits(q, k_refs)
    decay, car_s[...] = _carry_back(car_s[...], [x[0] for x in st], [x[1] for x in st])
    s_pages = [a + b for a, b in zip(qk, decay)]

    m_old = m_s[...]
    mx = functools.reduce(jnp.maximum, s_pages)
    m_new = jnp.maximum(m_old, jnp.max(mx, axis=1, keepdims=True))
    alpha = jnp.exp(m_old - m_new)
    p_pages = [jnp.exp(s - m_new) for s in s_pages]
    l_s[...] = l_s[...] * alpha + jnp.sum(functools.reduce(jnp.add, p_pages), axis=1, keepdims=True)
    acc_s[...] = acc_s[...] * alpha + _page_values(p_pages, v_refs)
    m_s[...] = m_new

    @pl.when(c == pl.num_programs(1) - 1)
    def _():
        o_ref[...] = acc_s[...] / l_s[...]


def fox_attention_decode(q16, kn, vn, lfn, cache_k, cache_v, cache_lft, page_table, layer, fox_layer):
    b = q16.shape[0]
    n_pages = page_table.shape[1]
    pages = min(8, n_pages)
    nc = n_pages // pages
    page = lambda r: (lambda c: (nc - 1 - c) * pages + r)
    kv_specs = [_page_spec(layer, n_pages, page(r)) for r in range(pages)]
    lf_specs = [pl.BlockSpec((None, None, N_HEADS, PAGE_SIZE),
                             lambda b, c, pt, f=page(r): (fox_layer, pt[b * n_pages + f(c)], 0, 0))
                for r in range(pages)]
    grid_spec = pltpu.PrefetchScalarGridSpec(
        num_scalar_prefetch=1,
        grid=(b, nc),
        in_specs=[_seq_spec(HEAD_ROWS, HEAD_DIM), _seq_spec(N_KV_HEADS, HEAD_DIM),
                  _seq_spec(N_KV_HEADS, HEAD_DIM), _seq_spec(HEAD_ROWS, PAGE_SIZE)]
        + kv_specs + kv_specs + lf_specs,
        out_specs=_seq_spec(HEAD_ROWS, HEAD_DIM),
        scratch_shapes=[pltpu.VMEM((HEAD_ROWS, LANES), F32)] * 4,
    )
    return pl.pallas_call(
        functools.partial(_fox_dec_kernel, pages=pages),
        grid_spec=grid_spec,
        out_shape=jax.ShapeDtypeStruct((b, HEAD_ROWS, HEAD_DIM), F32),
        compiler_params=_params("parallel", "arbitrary"),
        name="fox_attention_decode",
    )(page_table.reshape(-1), q16, kn, vn, lfn, *([cache_k] * pages), *([cache_v] * pages),
      *([cache_lft] * pages))


def _sb_dec_kernel(pt_ref, q_ref, *refs, pages):
    k_refs = refs[:pages]
    v_refs = refs[pages:2 * pages]
    o_ref, acc_s, car_s = refs[2 * pages:]
    c = pl.program_id(1)
    q = q_ref[...]
    suffix_total = _suffix_total_matrix(PAGE_SIZE)

    @pl.when(c == 0)
    def _():
        acc_s[...] = jnp.zeros(acc_s.shape, F32)
        car_s[...] = jnp.zeros(car_s.shape, F32)

    zs = _page_logits(q, k_refs)
    lks = [_neg_softplus(z) for z in zs]
    st = [_suffix_and_total(lk, suffix_total) for lk in lks]
    after, car_s[...] = _carry_back(car_s[...], [x[0] for x in st], [x[1] for x in st])
    a_pages = [jnp.exp(z + lk + af) for z, lk, af in zip(zs, lks, after)]
    acc_s[...] += _page_values(a_pages, v_refs)

    @pl.when(c == pl.num_programs(1) - 1)
    def _():
        o_ref[...] = acc_s[...]


def sb_attention_decode(q16, cache_k, cache_v, page_table, layer):
    b = q16.shape[0]
    n_pages = page_table.shape[1]
    pages = min(8, n_pages)
    nc = n_pages // pages
    kv_specs = [_page_spec(layer, n_pages, lambda c, r=r: (nc - 1 - c) * pages + r)
                for r in range(pages)]
    grid_spec = pltpu.PrefetchScalarGridSpec(
        num_scalar_prefetch=1,
        grid=(b, nc),
        in_specs=[_seq_spec(HEAD_ROWS, HEAD_DIM)] + kv_specs + kv_specs,
        out_specs=_seq_spec(HEAD_ROWS, HEAD_DIM),
        scratch_shapes=[pltpu.VMEM((HEAD_ROWS, LANES), F32)] * 2,
    )
    return pl.pallas_call(
        functools.partial(_sb_dec_kernel, pages=pages),
        grid_spec=grid_spec,
        out_shape=jax.ShapeDtypeStruct((b, HEAD_ROWS, HEAD_DIM), F32),
        compiler_params=_params("parallel", "arbitrary"),
        name="sb_attention_decode",
    )(page_table.reshape(-1), q16, *([cache_k] * pages), *([cache_v] * pages))


def _page_head_sums(page_ref):
    x = page_ref[...]
    s8 = jnp.sum(x.reshape(PAGE_ROWS // 8, 8, HEAD_DIM), axis=0)
    return s8[:N_KV_HEADS] + s8[N_KV_HEADS:]


def _moba_sel_kernel(pt_ref, q_ref, *refs, pages, nb):
    k_refs = refs[:pages]
    km_ref, sel_ref = refs[pages:]
    c = pl.program_id(1)
    bps = pages // 2
    sums = [_page_head_sums(k_refs[r]) for r in range(pages)]
    means = [(sums[2 * j] + sums[2 * j + 1]) * (1.0 / MOBA_BLOCK) for j in range(bps)]
    row0 = pl.multiple_of(c * bps, bps)
    for kvh in range(N_KV_HEADS):
        km_ref[kvh, pl.ds(row0, bps), :] = jnp.concatenate([mj[kvh:kvh + 1] for mj in means], axis=0)

    @pl.when(c == pl.num_programs(1) - 1)
    def _():
        q = q_ref[...]
        lane = lax.broadcasted_iota(I32, (nb, LANES), 1)
        gate = jnp.zeros((nb, LANES), F32)
        for kvh in range(N_KV_HEADS):
            hi, lo = _split2(km_ref[kvh])
            gate = jnp.where((lane >> 1) == kvh, _dot_nt(hi, q) + _dot_nt(lo, q), gate)
        sel = _top_mask(gate, nb, nb, 0)
        blk = lax.broadcasted_iota(I32, (nb, LANES), 0).astype(F32)
        rows = []
        for _ in range(MOBA_TOPK):
            first = jnp.min(jnp.where(sel, blk, float(nb)), axis=0, keepdims=True)
            rows.append(first.astype(I32))
            sel = sel & (blk != first)
        rows.append(jnp.zeros((8 - MOBA_TOPK, LANES), I32))
        sel_ref[...] = jnp.concatenate(rows, axis=0)


def moba_select_decode(q128, cache_k, page_table, layer):
    b = q128.shape[0]
    n_pages = page_table.shape[1]
    nb = n_pages * PAGE_SIZE // MOBA_BLOCK
    pages = min(16, n_pages)
    nc = n_pages // pages
    k_specs = [_page_spec(layer, n_pages, lambda c, r=r: c * pages + r) for r in range(pages)]
    grid_spec = pltpu.PrefetchScalarGridSpec(
        num_scalar_prefetch=1,
        grid=(b, nc),
        in_specs=[_seq_spec(LANES, HEAD_DIM)] + k_specs,
        out_specs=[pl.BlockSpec((None, N_KV_HEADS, nb, HEAD_DIM), lambda b, c, pt: (b, 0, 0, 0)),
                   _seq_spec(8, LANES)],
    )
    _, sel = pl.pallas_call(
        functools.partial(_moba_sel_kernel, pages=pages, nb=nb),
        grid_spec=grid_spec,
        out_shape=[jax.ShapeDtypeStruct((b, N_KV_HEADS, nb, HEAD_DIM), F32),
                   jax.ShapeDtypeStruct((b, 8, LANES), I32)],
        compiler_params=_params("parallel", "arbitrary"),
        name="moba_select_decode",
    )(page_table.reshape(-1), q128, *([cache_k] * pages))
    return sel


def _moba_dec_kernel(pt_ref, sel_ref, q_ref, kn_ref, vn_ref, bt_ref, *refs, nb):
    n = MOBA_TOPK * 2
    k_refs = refs[:n]
    v_refs = refs[n:2 * n]
    o_ref = refs[2 * n]
    b = pl.program_id(0)
    h = pl.program_id(1)
    head_rows = pl.ds(h // GROUP, PAGE_SIZE, stride=N_KV_HEADS)
    q = q_ref[...]
    bt = bt_ref[...]
    far = bt[:, 2 * PAGE_SIZE:3 * PAGE_SIZE]
    s_new = _new_token_logits(q, kn_ref[...]) + bt[:, 3 * PAGE_SIZE:3 * PAGE_SIZE + 1]

    s_piece = []
    for slot in range(MOBA_TOPK):
        last = sel_ref[(b * MOBA_TOPK + slot) * N_HEADS + h] == nb - 1
        for r in range(2):
            kp = k_refs[slot * 2 + r][head_rows, :].astype(BF16)
            bias = jnp.where(last, bt[:, r * PAGE_SIZE:(r + 1) * PAGE_SIZE], far)
            s_piece.append(_dot_nt(q, kp) + bias)
    mx = functools.reduce(jnp.maximum, s_piece)
    m = jnp.maximum(jnp.max(mx, axis=1, keepdims=True), s_new)
    p_piece = [jnp.exp(s - m) for s in s_piece]
    p_new = jnp.exp(s_new - m)
    l = jnp.sum(functools.reduce(jnp.add, p_piece), axis=1, keepdims=True) + p_new
    acc = p_new.astype(BF16).astype(F32) * _new_token_values(vn_ref[...])
    for i in range(n):
        acc = acc + _dot(p_piece[i].astype(BF16), v_refs[i][head_rows, :].astype(BF16))
    o = acc / l

    @pl.when(h == 0)
    def _():
        o_ref[...] = jnp.zeros(o_ref.shape, F32)

    rowi = lax.broadcasted_iota(I32, o.shape, 0)
    o_ref[pl.ds(h, 1), :] = jnp.sum(jnp.where(rowi == h, o, 0.0), axis=0, keepdims=True)


def moba_attention_decode(q16, kn, vn, bias_tab, sel, cache_k, cache_v, page_table, layer):
    b = q16.shape[0]
    n_pages = page_table.shape[1]
    nb = n_pages * PAGE_SIZE // MOBA_BLOCK

    def page_spec(slot, r):
        def index(b, h, pt, sel):
            blk = sel[(b * MOBA_TOPK + slot) * N_HEADS + h]
            return (layer, pt[b * n_pages + 2 * blk + r], 0, 0)
        return pl.BlockSpec((None, None, PAGE_ROWS, HEAD_DIM), index)

    kv_specs = [page_spec(slot, r) for slot in range(MOBA_TOPK) for r in range(2)]
    seq = lambda rows, width: pl.BlockSpec((None, rows, width), lambda b, h, pt, sel: (b, 0, 0))
    grid_spec = pltpu.PrefetchScalarGridSpec(
        num_scalar_prefetch=2,
        grid=(b, N_HEADS),
        in_specs=[seq(HEAD_ROWS, HEAD_DIM), seq(N_KV_HEADS, HEAD_DIM), seq(N_KV_HEADS, HEAD_DIM),
                  pl.BlockSpec((HEAD_ROWS, 4 * PAGE_SIZE), lambda b, h, pt, sel: (0, 0))] + kv_specs + kv_specs,
        out_specs=seq(HEAD_ROWS, HEAD_DIM),
    )
    n = len(kv_specs)
    return pl.pallas_call(
        functools.partial(_moba_dec_kernel, nb=nb),
        grid_spec=grid_spec,
        out_shape=jax.ShapeDtypeStruct((b, HEAD_ROWS, HEAD_DIM), F32),
        compiler_params=_params("parallel", "arbitrary"),
        name="moba_attention_decode",
    )(page_table.reshape(-1), sel.reshape(-1), q16, kn, vn, bias_tab, *([cache_k] * n), *([cache_v] * n))


def _rel_bias_by_distance(rel_bias, n):
    dist = jnp.arange(n, dtype=I32)
    max_exact = N_BUCKETS // 2
    large = max_exact + (jnp.log(jnp.maximum(dist, 1).astype(F32) / max_exact)
                         / math.log(MAX_DISTANCE / max_exact)
                         * (N_BUCKETS - max_exact)).astype(I32)
    large = jnp.minimum(large, N_BUCKETS - 1)
    return rel_bias[jnp.where(dist < max_exact, dist, large)]


def _pad_lanes(w, b):
    n = w.shape[1]
    return jnp.pad(w, ((0, 0), (0, LANES - n))), jnp.pad(b, (0, LANES - n)).reshape(1, LANES)


def _trunk(x, mod, cache, rel_bias, weights):
    (norm_g, final_g, w_in_b, w_out_b, fgates, w_router, b_router, wg_b, wu_b, wd_b) = weights
    b, t, d = x.shape
    m = b * t
    depth = w_in_b.shape[0]
    paged = cache is not None
    if paged:
        assert t == 1
        cache_k, cache_v, cache_lft, page_table = cache
        n_pages = page_table.shape[1]
        nb_past = n_pages * PAGE_SIZE // MOBA_BLOCK
        assert (n_pages * PAGE_SIZE) % MOBA_BLOCK == 0 and nb_past >= MOBA_TOPK
        tm_tok = tm_moe = m
        bias_d = _rel_bias_by_distance(rel_bias, 2 * MOBA_BLOCK + 1)
        assert MAX_DISTANCE <= MOBA_BLOCK
        near = bias_d[MOBA_BLOCK - jnp.arange(MOBA_BLOCK)].T
        tab = jnp.concatenate([near, jnp.broadcast_to(bias_d[2 * MOBA_BLOCK][:, None], (N_HEADS, PAGE_SIZE)),
                               jnp.broadcast_to(bias_d[0][:, None], (N_HEADS, PAGE_SIZE))], axis=1)
        bias_tab = jnp.pad(tab, ((0, HEAD_ROWS - N_HEADS), (0, 0)))
    else:
        assert t % MOBA_BLOCK == 0 and t // MOBA_BLOCK <= LANES
        tm_tok = min(512, t)
        tm_moe = min(1024, t)
        assert MAX_DISTANCE <= MOBA_BLOCK
        bias0, bias1 = rel_bias_tiles(rel_bias)

    q_scale = HEAD_DIM ** -0.5 * (1.0 if paged else LOG2E)
    wr, br = router_operands(w_router, b_router, transposed=tm_tok >= LANES)

    def mod_part(l, j):
        part = mod[l, :, j * d:(j + 1) * d]
        return part if paged else part.reshape(b, 1, d)

    xf = x.reshape(m, d)
    new_k, new_v, new_logf = [], [], []
    for l in range(depth):
        kind = l % N_MIXERS
        sh1, sc1, gt1, sh2, sc2, gt2 = [mod_part(l, j) for j in range(6)]
        fg = fgates[l // N_MIXERS] if kind == 0 else None
        outs = norm_qkv(xf, norm_g[l, 0].reshape(1, d), sc1, sh1, w_in_b[l], fg,
                        tm=tm_tok, rows_per_seq=t, q_scale=q_scale)
        q, k, v, kb, vb = outs[:5]
        new_k.append(k.reshape(b, t, N_KV_HEADS, HEAD_DIM))
        new_v.append(v.reshape(b, t, N_KV_HEADS, HEAD_DIM))
        if kind == 0:
            lf = outs[5]
            new_logf.append(lf[:, :N_HEADS].reshape(b, t, N_HEADS))
        if paged:
            q16 = jnp.pad(q.reshape(b, N_HEADS, HEAD_DIM), ((0, 0), (0, HEAD_ROWS - N_HEADS), (0, 0)))
            kn = k.reshape(b, N_KV_HEADS, HEAD_DIM)
            vn = v.reshape(b, N_KV_HEADS, HEAD_DIM)
            if kind == 0:
                lfn = jnp.broadcast_to(
                    jnp.pad(lf[:, :N_HEADS], ((0, 0), (0, HEAD_ROWS - N_HEADS)))[:, :, None],
                    (b, HEAD_ROWS, PAGE_SIZE))
                o16 = fox_attention_decode(q16, kn, vn, lfn, cache_k, cache_v, cache_lft, page_table,
                                           l, l // N_MIXERS)
            elif kind == 1:
                q128 = jnp.pad(q.reshape(b, N_HEADS, HEAD_DIM), ((0, 0), (0, LANES - N_HEADS), (0, 0)))
                sel = moba_select_decode(q128, cache_k, page_table, l)
                sel = sel[:, :MOBA_TOPK, :N_HEADS]
                o16 = moba_attention_decode(q16, kn, vn, bias_tab, sel, cache_k, cache_v, page_table, l)
            else:
                o16 = sb_attention_decode(q16, cache_k, cache_v, page_table, l)
            o = o16[:, :N_HEADS, :].reshape(m, Q_DIM).astype(BF16)
        else:
            q3 = q.reshape(b, t, Q_DIM)
            kb3 = kb.reshape(b, t, KV_DIM)
            vb3 = vb.reshape(b, t, KV_DIM)
            if kind == 0:
                o = fox_attention_prompt(q3, kb3, vb3, cumsum_time(lf.reshape(b, t, LANES)))
            elif kind == 1:
                kmean = block_mean(k.reshape(b, t * N_KV_HEADS, HEAD_DIM))
                o = moba_attention_prompt(q3, kb3, vb3, kmean, bias0, bias1)
            else:
                o = sb_attention_prompt(q3, kb3, vb3)
            o = o.reshape(m, Q_DIM)
        x1, h2, gates = out_proj_router(o, xf, w_out_b[l], gt1, norm_g[l, 1].reshape(1, d), sc2, sh2,
                                        wr, br, tm=tm_tok, rows_per_seq=t)
        fin = final_g.reshape(1, d) if l == depth - 1 else None
        moe = moe_ffn if paged else moe_ffn_sorted
        xf = moe(h2, gates, x1, gt2, wg_b[l], wu_b[l], wd_b[l], fin, tm=tm_moe, rows_per_seq=t)
    return xf.reshape(b, t, d), jnp.stack(new_k), jnp.stack(new_v), jnp.stack(new_logf)


def kernel(x_prompt, x_sample, cache_k, cache_v, cache_logf, page_table, c_prompt, c_sample, rel_bias,
           w_ada, b_ada, norm_g, final_g, w_in, w_out, w_fgate, b_fgate, w_router, b_router,
           w_gate, w_up, w_down):
    n_prompt = c_prompt.shape[0]
    mod = ada_modulation(jnp.concatenate([c_prompt, c_sample], axis=0), w_ada, b_ada)

    fgates = []
    for a in range(w_fgate.shape[0]):
        wf, bf = _pad_lanes(w_fgate[a], b_fgate[a])
        fgates.append((wf.astype(BF16), bf))
    weights = (norm_g, final_g, w_in.astype(BF16), w_out.astype(BF16), fgates, w_router, b_router,
               w_gate.astype(BF16), w_up.astype(BF16), w_down.astype(BF16))

    depth, n_pool = cache_k.shape[:2]
    cache = (cache_k.reshape(depth, n_pool, PAGE_ROWS, HEAD_DIM),
             cache_v.reshape(depth, n_pool, PAGE_ROWS, HEAD_DIM),
             jnp.swapaxes(cache_logf, 2, 3), page_table)

    y_p, k_p, v_p, lf_p = _trunk(x_prompt, mod[:, :n_prompt], None, rel_bias, weights)
    y_s, k_s, v_s, lf_s = _trunk(x_sample, mod[:, n_prompt:], cache, rel_bias, weights)
    return (y_p, y_s, k_p, v_p, lf_p, k_s, v_s, lf_s)
```

```python
import functools
import math

import jax
import jax.numpy as jnp
from jax import lax
from jax.experimental import pallas as pl
from jax.experimental.pallas import tpu as pltpu

F32 = jnp.float32
BF16 = jnp.bfloat16
I32 = jnp.int32

N_MIXERS = 3
N_HEADS = 8
N_KV_HEADS = 4
GROUP = N_HEADS // N_KV_HEADS
HEAD_DIM = 128
Q_DIM = N_HEADS * HEAD_DIM
KV_DIM = N_KV_HEADS * HEAD_DIM
PAGE_SIZE = 128
MOBA_BLOCK = 256
MOBA_TOPK = 3
N_BUCKETS = 32
MAX_DISTANCE = 128
N_EXPERTS = 16
N_GROUPS = 4
EXPERTS_PER_GROUP = N_EXPERTS // N_GROUPS
TOP_K = 2
GROUP_LANE = N_EXPERTS
SORTED_ROWS = 256
RMS_EPS = 1e-6

PAGE_ROWS = PAGE_SIZE * N_KV_HEADS
LANES = 128
HEAD_ROWS = 16
MASKED = -1e30
LOG2E = math.log2(math.e)
VMEM_LIMIT = 56 * 1024 * 1024


def _params(*sem):
    return pltpu.CompilerParams(dimension_semantics=sem, vmem_limit_bytes=VMEM_LIMIT)


def _dot(a, b):
    return jnp.dot(a, b, preferred_element_type=F32)


def _dot_nt(a, b):
    return lax.dot_general(a, b, (((1,), (1,)), ((), ())), preferred_element_type=F32)


def _split2(x):
    hi = x.astype(BF16)
    lo = (x - hi.astype(F32)).astype(BF16)
    return hi, lo


def _split3(x):
    hi = x.astype(BF16)
    r = x - hi.astype(F32)
    mid = r.astype(BF16)
    lo = (r - mid.astype(F32)).astype(BF16)
    return hi, mid, lo


def _neg_softplus(z):
    return -(jnp.maximum(z, 0.0) + jnp.log1p(jnp.exp(-jnp.abs(z))))


def _silu(x):
    return x / (1.0 + jnp.exp(-x))


def _ada_kernel(c_ref, w_ref, b_ref, o_ref):
    s = _silu(c_ref[...])
    o_ref[...] = _dot(s.astype(BF16), w_ref[...].astype(BF16)) + b_ref[...]


def ada_modulation(c_all, w_ada, b_ada):
    depth, d, n = w_ada.shape
    mc = c_all.shape[0]
    tn = 1024
    return pl.pallas_call(
        _ada_kernel,
        grid=(depth, n // tn),
        in_specs=[
            pl.BlockSpec((mc, d), lambda l, j: (0, 0)),
            pl.BlockSpec((None, d, tn), lambda l, j: (l, 0, j)),
            pl.BlockSpec((None, 1, tn), lambda l, j: (l, 0, j)),
        ],
        out_specs=pl.BlockSpec((None, mc, tn), lambda l, j: (l, 0, j)),
        out_shape=jax.ShapeDtypeStruct((depth, mc, n), F32),
        compiler_params=_params("parallel", "parallel"),
        name="ada_modulation",
    )(c_all, w_ada, b_ada.reshape(depth, 1, n))


def _rms_mod(x, g, sc, sh):
    r = lax.rsqrt(jnp.mean(x * x, axis=-1, keepdims=True) + RMS_EPS)
    return (x * r * g) * (1.0 + sc) + sh


def _qkv_kernel(*refs, has_fgate, q_scale):
    if has_fgate:
        (x_ref, g_ref, sc_ref, sh_ref, w_ref, wf_ref, bf_ref,
         q_ref, k_ref, v_ref, kb_ref, vb_ref, lf_ref) = refs
    else:
        x_ref, g_ref, sc_ref, sh_ref, w_ref, q_ref, k_ref, v_ref, kb_ref, vb_ref = refs
    hb = _rms_mod(x_ref[...], g_ref[...], sc_ref[...], sh_ref[...]).astype(BF16)
    qkv = _dot(hb, w_ref[...])
    q_ref[...] = (qkv[:, :Q_DIM] * q_scale).astype(BF16)
    k = qkv[:, Q_DIM:Q_DIM + KV_DIM]
    v = qkv[:, Q_DIM + KV_DIM:]
    tm = k.shape[0]
    for kvh in range(N_KV_HEADS):
        rows = pl.ds(kvh, tm, stride=N_KV_HEADS)
        k_ref[rows, :] = k[:, kvh * HEAD_DIM:(kvh + 1) * HEAD_DIM]
        v_ref[rows, :] = v[:, kvh * HEAD_DIM:(kvh + 1) * HEAD_DIM]
    kb_ref[...] = k.astype(BF16)
    vb_ref[...] = v.astype(BF16)
    if has_fgate:
        z = _dot(hb, wf_ref[...]) + bf_ref[...]
        lf_ref[...] = jnp.minimum(z, 0.0) - jnp.log1p(jnp.exp(-jnp.abs(z)))


def _mod_spec(mod, tm, rows_per_seq):
    d = mod.shape[-1]
    if mod.ndim == 3:
        tiles = rows_per_seq // tm
        return pl.BlockSpec((None, 1, d), lambda i: (i // tiles, 0, 0))
    return pl.BlockSpec((tm, d), lambda i: (i, 0))


def norm_qkv(x, g, sc, sh, w_in_b, fgate, *, tm, rows_per_seq, q_scale):
    m, d = x.shape
    n = w_in_b.shape[1]
    has_fgate = fgate is not None
    row = lambda i: (i, 0)
    const = lambda i: (0, 0)
    in_specs = [pl.BlockSpec((tm, d), row), pl.BlockSpec((1, d), const),
                _mod_spec(sc, tm, rows_per_seq), _mod_spec(sh, tm, rows_per_seq),
                pl.BlockSpec((d, n), const)]
    args = [x, g, sc, sh, w_in_b]
    out_specs = [pl.BlockSpec((tm, Q_DIM), row), pl.BlockSpec((tm * N_KV_HEADS, HEAD_DIM), row),
                 pl.BlockSpec((tm * N_KV_HEADS, HEAD_DIM), row), pl.BlockSpec((tm, KV_DIM), row),
                 pl.BlockSpec((tm, KV_DIM), row)]
    out_shape = [jax.ShapeDtypeStruct((m, Q_DIM), BF16),
                 jax.ShapeDtypeStruct((m * N_KV_HEADS, HEAD_DIM), F32),
                 jax.ShapeDtypeStruct((m * N_KV_HEADS, HEAD_DIM), F32),
                 jax.ShapeDtypeStruct((m, KV_DIM), BF16), jax.ShapeDtypeStruct((m, KV_DIM), BF16)]
    if has_fgate:
        wf, bf = fgate
        in_specs += [pl.BlockSpec((d, LANES), const), pl.BlockSpec((1, LANES), const)]
        args += [wf, bf]
        out_specs.append(pl.BlockSpec((tm, LANES), row))
        out_shape.append(jax.ShapeDtypeStruct((m, LANES), F32))
    return pl.pallas_call(
        functools.partial(_qkv_kernel, has_fgate=has_fgate, q_scale=q_scale),
        grid=(m // tm,),
        in_specs=in_specs, out_specs=out_specs, out_shape=out_shape,
        compiler_params=_params("parallel"),
        name="norm_qkv",
    )(*args)


def _cumsum_kernel(lf_ref, dt_ref, *, chunk):
    t = lf_ref.shape[0]
    row = lax.broadcasted_iota(I32, (chunk, chunk), 0)
    col = lax.broadcasted_iota(I32, (chunk, chunk), 1)
    tri = jnp.where(col <= row, 1.0, 0.0).astype(BF16)
    carry = jnp.zeros((1, LANES), F32)
    for c in range(t // chunk):
        sl = slice(c * chunk, (c + 1) * chunk)
        hi, mid, lo = _split3(lf_ref[sl, :])
        cs = _dot(tri, hi) + _dot(tri, mid) + _dot(tri, lo) + carry
        dt_ref[:, sl] = cs.T[:N_HEADS, :]
        carry = cs[chunk - 1:chunk, :]


def cumsum_time(lf):
    b, t, _ = lf.shape
    chunk = min(256, t)
    return pl.pallas_call(
        functools.partial(_cumsum_kernel, chunk=chunk),
        grid=(b,),
        in_specs=[pl.BlockSpec((None, t, LANES), lambda i: (i, 0, 0))],
        out_specs=pl.BlockSpec((None, N_HEADS, t), lambda i: (i, 0, 0)),
        out_shape=jax.ShapeDtypeStruct((b, N_HEADS, t), F32),
        compiler_params=_params("parallel"),
        name="cumsum_time",
    )(lf)


def _lane_pick(x, idx):
    lane = lax.broadcasted_iota(I32, x.shape, 1)
    return jnp.sum(jnp.where(lane == idx, x, 0.0), axis=1, keepdims=True)


def _with_ones(v):
    return jnp.concatenate([v, jnp.ones(v.shape, v.dtype)], axis=1)


def _softmax2_first(s, v1):
    m = jnp.max(s, axis=1, keepdims=True)
    return m, _dot(jnp.exp2(s - m).astype(BF16), v1)


def _softmax2_step(s, v1, m, accl, keep=None):
    m_new = jnp.maximum(m, jnp.max(s, axis=1, keepdims=True))
    new = accl * jnp.exp2(m - m_new) + _dot(jnp.exp2(s - m_new).astype(BF16), v1)
    if keep is None:
        return m_new, new
    return jnp.where(keep, m_new, m), jnp.where(keep, new, accl)


def _softmax2_out(accl):
    return (accl[:, :HEAD_DIM] / accl[:, HEAD_DIM:HEAD_DIM + 1]).astype(BF16)


def _fox_kernel(q_ref, k_ref, v_ref, dt_ref, o_ref, *, tq):
    kvh = pl.program_id(1)
    i = pl.program_id(2)
    q0 = pl.multiple_of(i * tq, tq)
    row = lax.broadcasted_iota(I32, (tq, tq), 0)
    col = lax.broadcasted_iota(I32, (tq, tq), 1)
    causal = col <= row
    heads = [kvh * GROUP + g for g in range(GROUP)]
    qs = [q_ref[:, g * HEAD_DIM:(g + 1) * HEAD_DIM] for g in range(GROUP)]
    d0s = [dt_ref[pl.ds(h, 1), pl.ds(q0, tq)][:, 0:1] for h in heads]

    def logits(g, start, kt):
        dk = dt_ref[pl.ds(heads[g], 1), pl.ds(start, tq)]
        return _dot_nt(qs[g], kt) + (d0s[g] - dk) * LOG2E

    def step(s, vt, m, l, acc):
        m_new = jnp.maximum(m, jnp.max(s, axis=1, keepdims=True))
        alpha = jnp.exp2(m - m_new)
        p = jnp.exp2(s - m_new)
        return m_new, l * alpha + jnp.sum(p, axis=1, keepdims=True), acc * alpha + _dot(p.astype(BF16), vt)

    kt, vt = k_ref[pl.ds(q0, tq), :], v_ref[pl.ds(q0, tq), :]
    zero = (jnp.full((tq, 1), MASKED, F32), jnp.zeros((tq, 1), F32), jnp.zeros((tq, HEAD_DIM), F32))
    state = tuple(step(jnp.where(causal, logits(g, q0, kt), MASKED), vt, *zero) for g in range(GROUP))

    def body(j, state):
        start = pl.multiple_of(j * tq, tq)
        kt, vt = k_ref[pl.ds(start, tq), :], v_ref[pl.ds(start, tq), :]
        return tuple(step(logits(g, start, kt), vt, *state[g]) for g in range(GROUP))

    state = lax.fori_loop(0, i, body, state)
    for g, (_, l, acc) in enumerate(state):
        o_ref[:, g * HEAD_DIM:(g + 1) * HEAD_DIM] = (acc / l).astype(BF16)


def _attn_specs(t, tq):
    qo = pl.BlockSpec((None, tq, GROUP * HEAD_DIM), lambda b, kvh, i: (b, i, kvh))
    kv = pl.BlockSpec((None, t, HEAD_DIM), lambda b, kvh, i: (b, 0, kvh))
    return qo, kv


def fox_attention_prompt(q, kb, vb, dt):
    b, t, _ = q.shape
    tq = min(512, t)
    qo, kv = _attn_specs(t, tq)
    return pl.pallas_call(
        functools.partial(_fox_kernel, tq=tq),
        grid=(b, N_KV_HEADS, t // tq),
        in_specs=[qo, kv, kv, pl.BlockSpec((None, N_HEADS, t), lambda b, kvh, i: (b, 0, 0))],
        out_specs=qo,
        out_shape=jax.ShapeDtypeStruct((b, t, Q_DIM), BF16),
        compiler_params=_params("parallel", "parallel", "parallel"),
        name="fox_attention_prompt",
    )(q, kb, vb, dt)


def _suffix_matrices(n):
    row = lax.broadcasted_iota(I32, (n, n), 0)
    col = lax.broadcasted_iota(I32, (n, n), 1)
    return jnp.where(row > col, 1.0, 0.0).astype(BF16), jnp.ones((n, n), BF16)


def _sb_kernel(q_ref, k_ref, v_ref, o_ref, *, tq, tk):
    i = pl.program_id(2)
    q0 = pl.multiple_of(i * tq, tq)
    upper, _ = _suffix_matrices(tk)
    row = lax.broadcasted_iota(I32, (tq, tk), 0)
    col = lax.broadcasted_iota(I32, (tq, tk), 1)
    qs = [q_ref[:, g * HEAD_DIM:(g + 1) * HEAD_DIM] for g in range(GROUP)]

    def chunk(start, state, mask, r0=0):
        kt, vt = k_ref[pl.ds(start, tk), :], v_ref[pl.ds(start, tk), :]
        out = []
        for g in range(GROUP):
            c, acc = state[g]
            z = _dot_nt(qs[g][r0:], kt)
            ls = jnp.minimum(z, 0.0) - jnp.log2(1.0 + jnp.exp2(jnp.minimum(z, -z)))
            lk = ls - z
            if mask is not None:
                lk = jnp.where(mask[r0:], lk, 0.0)
            hi, lo = _split2(lk)
            after = _dot(hi, upper) + _dot(lo, upper) + c[r0:]
            a = jnp.exp2(ls + after)
            if mask is not None:
                a = jnp.where(mask[r0:], a, 0.0)
            c_new = c[r0:] + jnp.sum(lk, axis=1, keepdims=True)
            acc_new = acc[r0:] + _dot(a.astype(BF16), vt)
            if r0:
                c_new = jnp.concatenate([c[:r0], c_new], axis=0)
                acc_new = jnp.concatenate([acc[:r0], acc_new], axis=0)
            out.append((c_new, acc_new))
        return tuple(out)

    state = tuple((jnp.zeros((tq, 1), F32), jnp.zeros((tq, HEAD_DIM), F32)) for _ in range(GROUP))
    for mth in reversed(range(tq // tk)):
        start = pl.multiple_of(q0 + mth * tk, tk)
        state = chunk(start, state, (col + mth * tk) < row, r0=mth * tk)

    n_past = i * (tq // tk)

    def body(it, state):
        return chunk(pl.multiple_of((n_past - 1 - it) * tk, tk), state, None)

    state = lax.fori_loop(0, n_past, body, state)
    for g, (_, acc) in enumerate(state):
        o_ref[:, g * HEAD_DIM:(g + 1) * HEAD_DIM] = acc.astype(BF16)


def sb_attention_prompt(q, kb, vb):
    b, t, _ = q.shape
    tq = min(1024, t)
    tk = min(256, t)
    qo, kv = _attn_specs(t, tq)
    return pl.pallas_call(
        functools.partial(_sb_kernel, tq=tq, tk=tk),
        grid=(b, N_KV_HEADS, t // tq),
        in_specs=[qo, kv, kv],
        out_specs=qo,
        out_shape=jax.ShapeDtypeStruct((b, t, Q_DIM), BF16),
        compiler_params=_params("parallel", "parallel", "parallel"),
        name="sb_attention_prompt",
    )(q, kb, vb)


def _block_mean_kernel(k_ref, o_ref, *, nb):
    t = k_ref.shape[0] // N_KV_HEADS
    o_ref[...] = jnp.zeros(o_ref.shape, F32)
    for kvh in range(N_KV_HEADS):
        x = k_ref[pl.ds(kvh, t, stride=N_KV_HEADS), :]
        o_ref[kvh, 0:nb, :] = jnp.sum(x.reshape(nb, t // nb, HEAD_DIM), axis=1) * (1.0 / (t // nb))


def block_mean(k4):
    b, rows, _ = k4.shape
    nb = rows // N_KV_HEADS // MOBA_BLOCK
    return pl.pallas_call(
        functools.partial(_block_mean_kernel, nb=nb),
        grid=(b,),
        in_specs=[pl.BlockSpec((None, rows, HEAD_DIM), lambda i: (i, 0, 0))],
        out_specs=pl.BlockSpec((None, N_KV_HEADS, LANES, HEAD_DIM), lambda i: (i, 0, 0, 0)),
        out_shape=jax.ShapeDtypeStruct((b, N_KV_HEADS, LANES, HEAD_DIM), F32),
        compiler_params=_params("parallel"),
        name="block_mean",
    )(k4)


def _top_mask(gate, n_valid, n_cand, axis):
    idx = lax.broadcasted_iota(I32, gate.shape, axis)
    cnt = jnp.zeros(gate.shape, I32)
    for jp in range(n_cand):
        other = gate[:, jp:jp + 1] if axis == 1 else gate[jp:jp + 1, :]
        beats = (other > gate) | ((other == gate) & (jp < idx))
        cnt = cnt + jnp.where(beats, jnp.where(jp < n_valid, 1, 0), 0)
    return (idx < n_valid) & (cnt < MOBA_TOPK)


def _moba_kernel(q_ref, k_ref, v_ref, km_ref, b0_ref, b1_ref, o_ref, *, tq, nb):
    i = pl.program_id(2)
    q0 = pl.multiple_of(i * tq, tq)
    row = lax.broadcasted_iota(I32, (tq, tq), 0)
    col = lax.broadcasted_iota(I32, (tq, tq), 1)
    causal = col <= row
    nbp = -(-nb // 16) * 16
    km_hi, km_lo = _split2(km_ref[0:nbp, :])
    qs = [q_ref[:, g * HEAD_DIM:(g + 1) * HEAD_DIM] for g in range(GROUP)]

    def select(qh):
        gate = _dot_nt(km_hi, qh) + _dot_nt(km_lo, qh)
        sel = jnp.where(_top_mask(gate, i, nb, 0), 1.0, 0.0)
        return jnp.concatenate([sel, jnp.zeros((LANES - nbp, tq), F32)], axis=0).T

    sels = [select(qh) for qh in qs]

    def tile(j):
        start = pl.multiple_of(j * tq, tq)
        return k_ref[pl.ds(start, tq), :], _with_ones(v_ref[pl.ds(start, tq), :])

    def picked(g, j):
        return _lane_pick(sels[g], j) > 0.0

    kt, v1 = tile(i)
    state = tuple(_softmax2_first(jnp.where(causal, _dot_nt(qs[g], kt) + b0_ref[g], MASKED), v1)
                  for g in range(GROUP))
    prev = jnp.maximum(i - 1, 0)
    kt, v1 = tile(prev)
    state = tuple(_softmax2_step(_dot_nt(qs[g], kt) + b1_ref[g], v1, *state[g], keep=picked(g, i - 1))
                  for g in range(GROUP))

    def body(j, state):
        kt, v1 = tile(j)
        return tuple(_softmax2_step(_dot_nt(qs[g], kt), v1, *state[g], keep=picked(g, j))
                     for g in range(GROUP))

    state = lax.fori_loop(0, i - 1, body, state)
    for g, (_, accl) in enumerate(state):
        o_ref[:, g * HEAD_DIM:(g + 1) * HEAD_DIM] = _softmax2_out(accl)


def _bias_tile_kernel(rb_ref, b0_ref, b1_ref, *, n):
    h = pl.program_id(0)
    row = lax.broadcasted_iota(I32, (n, n), 0)
    col = lax.broadcasted_iota(I32, (n, n), 1)
    max_exact = N_BUCKETS // 2
    for out_ref, shift in ((b0_ref, 0), (b1_ref, n)):
        dist = jnp.maximum(row - col + shift, 0)
        large = max_exact + (jnp.log(jnp.maximum(dist, 1).astype(F32) / max_exact)
                             / math.log(MAX_DISTANCE / max_exact) * (N_BUCKETS - max_exact)).astype(I32)
        bucket = jnp.where(dist < max_exact, dist, jnp.minimum(large, N_BUCKETS - 1))
        acc = jnp.zeros((n, n), F32)
        for bk in range(N_BUCKETS):
            acc = jnp.where(bucket == bk, rb_ref[bk, h], acc)
        out_ref[...] = (acc - rb_ref[N_BUCKETS - 1, h]) * LOG2E


def rel_bias_tiles(rel_bias):
    n = MOBA_BLOCK
    shape = jax.ShapeDtypeStruct((N_HEADS, n, n), F32)
    spec = pl.BlockSpec((None, n, n), lambda h: (h, 0, 0))
    return pl.pallas_call(
        functools.partial(_bias_tile_kernel, n=n),
        grid=(N_HEADS,),
        in_specs=[pl.BlockSpec(memory_space=pltpu.SMEM)],
        out_specs=[spec, spec],
        out_shape=[shape, shape],
        compiler_params=_params("parallel"),
        name="rel_bias_tiles",
    )(rel_bias)


def moba_attention_prompt(q, kb, vb, kmean, bias0, bias1):
    b, t, _ = q.shape
    tq = MOBA_BLOCK
    nb = t // tq
    qo, kv = _attn_specs(t, tq)
    head_pair = lambda b, kvh, i: (kvh, 0, 0)
    return pl.pallas_call(
        functools.partial(_moba_kernel, tq=tq, nb=nb),
        grid=(b, N_KV_HEADS, nb),
        in_specs=[qo, kv, kv,
                  pl.BlockSpec((None, None, LANES, HEAD_DIM), lambda b, kvh, i: (b, kvh, 0, 0)),
                  pl.BlockSpec((GROUP, tq, tq), head_pair),
                  pl.BlockSpec((GROUP, tq, tq), head_pair)],
        out_specs=qo,
        out_shape=jax.ShapeDtypeStruct((b, t, Q_DIM), BF16),
        compiler_params=_params("parallel", "parallel", "parallel"),
        name="moba_attention_prompt",
    )(q, kb, vb, kmean, bias0, bias1)


def _route(logits):
    idx = lax.broadcasted_iota(I32, logits.shape, 1)
    valid = idx < N_EXPERTS
    lg = jnp.where(valid, logits, MASKED)
    e = jnp.where(valid, jnp.exp(lg - jnp.max(lg, axis=1, keepdims=True)), 0.0)
    grp = idx >> 2

    def peers(x, shifts):
        for s in shifts:
            for sh in (s, LANES - s):
                oi = pltpu.roll(idx, sh, 1)
                yield pltpu.roll(x, sh, 1), oi, oi < N_EXPERTS

    cnt = jnp.zeros(logits.shape, I32)
    for oe, oi, ok in peers(e, (1, 2, 3)):
        beats = ok & ((oi >> 2) == grp) & ((oe > e) | ((oe == e) & (oi < idx)))
        cnt = cnt + jnp.where(beats, 1, 0)
    top2 = valid & (cnt < TOP_K)
    t2e = jnp.where(top2, e, 0.0)
    score = t2e
    for ot, oi, ok in peers(t2e, (1, 2, 3)):
        score = score + jnp.where(ok & ((oi >> 2) == grp), ot, 0.0)
    lost = jnp.zeros(logits.shape, I32)
    for osc, oi, ok in peers(score, (4, 8, 12)):
        beats = ok & ((osc > score) | ((osc == score) & ((oi >> 2) < grp)))
        lost = lost + jnp.where(beats, 1, 0)
    return jnp.where(top2 & valid & (lost == 0), e / score, 0.0)


def _route_t(lg):
    npos, ngrp = EXPERTS_PER_GROUP, N_GROUPS
    e = jnp.exp(lg - jnp.max(lg, axis=0, keepdims=True))
    pos = [jnp.concatenate([e[g * npos + a:g * npos + a + 1, :] for g in range(ngrp)], axis=0)
           for a in range(npos)]
    top2 = []
    for a in range(npos):
        cnt = jnp.zeros(pos[a].shape, I32)
        for b in range(npos):
            if b != a:
                cnt = cnt + jnp.where((pos[b] >= pos[a]) if b < a else (pos[b] > pos[a]), 1, 0)
        top2.append(cnt < TOP_K)
    score = functools.reduce(jnp.add, [jnp.where(t, p, 0.0) for t, p in zip(top2, pos)])
    rows = [score[g:g + 1, :] for g in range(ngrp)]
    won = []
    for g in range(ngrp):
        lost = jnp.zeros(rows[g].shape, I32)
        for o in range(ngrp):
            if o != g:
                lost = lost + jnp.where((rows[o] >= rows[g]) if o < g else (rows[o] > rows[g]), 1, 0)
        won.append(jnp.where(lost == 0, 1.0, 0.0))
    group_id = functools.reduce(jnp.add, [won[g] * float(g) for g in range(ngrp)])
    won = jnp.concatenate(won, axis=0) > 0.0
    gate = [jnp.where(t & won, p / score, 0.0) for t, p in zip(top2, pos)]
    gates = jnp.concatenate([gate[a][g:g + 1, :] for g in range(ngrp) for a in range(npos)], axis=0)
    return gates, group_id


def _oproj_kernel(o_ref, x_ref, w_ref, gt_ref, g_ref, sc_ref, sh_ref, wr_ref, br_ref,
                  x1_ref, h2_ref, gates_ref, *, transposed):
    x1 = x_ref[...] + gt_ref[...] * _dot(o_ref[...], w_ref[...])
    x1_ref[...] = x1
    h2 = _rms_mod(x1, g_ref[...], sc_ref[...], sh_ref[...])
    hi, lo = _split2(h2)
    h2_ref[...] = hi
    if transposed:
        both = _dot_nt(wr_ref[...], hi)
        lg = both[:N_EXPERTS] + both[N_EXPERTS:] + _dot_nt(wr_ref[0:N_EXPERTS, :], lo) + br_ref[:, 0:1]
        gates, group_id = _route_t(lg)
        pad = jnp.zeros((LANES - N_EXPERTS - 1, gates.shape[1]), F32)
        gates_ref[...] = jnp.concatenate([gates, group_id, pad], axis=0).T
    else:
        both = _dot(hi, wr_ref[...])
        logits = both[:, :LANES] + both[:, LANES:] + _dot(lo, wr_ref[:, 0:LANES]) + br_ref[...]
        gates_ref[...] = _route(logits)


def router_operands(w_router, b_router, transposed):
    if transposed:
        hi, lo = _split2(w_router.T)
        return jnp.concatenate([hi, lo], axis=0), jnp.broadcast_to(b_router[:, None], (N_EXPERTS, LANES))
    wr, br = _pad_lanes(w_router, b_router)
    hi, lo = _split2(wr)
    return jnp.concatenate([hi, lo], axis=1), br


def out_proj_router(o, x, w_out_b, gt, g2, sc, sh, wr, br, *, tm, rows_per_seq):
    m, d = x.shape
    transposed = wr.shape[1] == d
    row = lambda i: (i, 0)
    const = lambda i: (0, 0)
    ms = lambda a: _mod_spec(a, tm, rows_per_seq)
    return pl.pallas_call(
        functools.partial(_oproj_kernel, transposed=transposed),
        grid=(m // tm,),
        in_specs=[pl.BlockSpec((tm, Q_DIM), row), pl.BlockSpec((tm, d), row),
                  pl.BlockSpec((Q_DIM, d), const), ms(gt), pl.BlockSpec((1, d), const),
                  ms(sc), ms(sh), pl.BlockSpec(wr.shape, const), pl.BlockSpec(br.shape, const)],
        out_specs=[pl.BlockSpec((tm, d), row), pl.BlockSpec((tm, d), row),
                   pl.BlockSpec((tm, LANES), row)],
        out_shape=[jax.ShapeDtypeStruct((m, d), F32), jax.ShapeDtypeStruct((m, d), BF16),
                   jax.ShapeDtypeStruct((m, LANES), F32)],
        compiler_params=_params("parallel"),
        name="out_proj_router",
    )(o, x, w_out_b, gt, g2, sc, sh, wr, br)


def _moe_kernel(h_ref, gates_ref, x_ref, gt_ref, wg_ref, wu_ref, wd_ref, *rest, epc, final):
    fg_ref, o_ref, acc_ref = rest if final else (None, *rest)
    c = pl.program_id(1)

    @pl.when(c == 0)
    def _():
        acc_ref[...] = jnp.zeros(acc_ref.shape, F32)

    h = h_ref[...]
    gates = gates_ref[...]
    acts = []
    for j in range(epc):
        gcol = _lane_pick(gates, c * epc + j)
        acts.append((_silu(_dot(h, wg_ref[j])) * _dot(h, wu_ref[j]) * gcol).astype(BF16))
    wd = wd_ref[...]
    acc_ref[...] += _dot(jnp.concatenate(acts, axis=1), wd.reshape(wd.shape[0] * wd.shape[1], wd.shape[2]))

    @pl.when(c == pl.num_programs(1) - 1)
    def _():
        x = x_ref[...] + gt_ref[...] * acc_ref[...]
        if fg_ref is not None:
            x = x * lax.rsqrt(jnp.mean(x * x, axis=-1, keepdims=True) + RMS_EPS) * fg_ref[...]
        o_ref[...] = x


def moe_ffn(h2, gates, x1, gt, wg_b, wu_b, wd_b, final_g, *, tm, rows_per_seq):
    m, d = x1.shape
    n_e, _, de = wg_b.shape
    epc = 4
    row = lambda i, c: (i, 0)
    gt_spec = _mod_spec(gt, tm, rows_per_seq)
    gt_spec = pl.BlockSpec(gt_spec.block_shape, lambda i, c, f=gt_spec.index_map: f(i))
    in_specs = [pl.BlockSpec((tm, d), row), pl.BlockSpec((tm, LANES), row),
                pl.BlockSpec((tm, d), row), gt_spec,
                pl.BlockSpec((epc, d, de), lambda i, c: (c, 0, 0)),
                pl.BlockSpec((epc, d, de), lambda i, c: (c, 0, 0)),
                pl.BlockSpec((epc, de, d), lambda i, c: (c, 0, 0))]
    args = [h2, gates, x1, gt, wg_b, wu_b, wd_b]
    if final_g is not None:
        in_specs.append(pl.BlockSpec((1, d), lambda i, c: (0, 0)))
        args.append(final_g)
    return pl.pallas_call(
        functools.partial(_moe_kernel, epc=epc, final=final_g is not None),
        grid=(m // tm, n_e // epc),
        in_specs=in_specs,
        out_specs=pl.BlockSpec((tm, d), row),
        out_shape=jax.ShapeDtypeStruct((m, d), F32),
        scratch_shapes=[pltpu.VMEM((tm, d), F32)],
        compiler_params=_params("parallel", "arbitrary"),
        name="moe_ffn",
    )(*args)


def _moe_sorted_kernel(cnt_ref, h_ref, route_ref, x_ref, gt_ref, wg_ref, wu_ref, wd_ref, *rest, final):
    fg_ref, o_ref, acc_ref, posc_ref, posr_ref = rest if final else (None, *rest)
    i = pl.program_id(0)
    g = pl.program_id(1)
    tm = h_ref.shape[0]
    route = route_ref[...]

    @pl.when(g == 0)
    def _():
        acc_ref[...] = jnp.zeros(acc_ref.shape, F32)
        lane = lax.broadcasted_iota(I32, (tm, LANES), 1)
        gid = route[:, GROUP_LANE:GROUP_LANE + 1]
        onehot = jnp.where(lane.astype(F32) == gid, 1.0, 0.0)
        r = lax.broadcasted_iota(I32, (tm, tm), 0)
        c = lax.broadcasted_iota(I32, (tm, tm), 1)
        earlier = jnp.where(c < r, 1.0, 0.0).astype(BF16)
        rank = jnp.sum(onehot * _dot(earlier, onehot.astype(BF16)), axis=1, keepdims=True)
        posc = jnp.where(lane == 0, gid, jnp.where(lane == 1, rank, 0.0))
        posc_ref[...] = posc
        posr_ref[...] = posc.T

    n_rows = cnt_ref[i * N_GROUPS + g]
    gf = g.astype(F32)
    pos_col = jnp.where(posc_ref[:, 0:1] == gf, posc_ref[:, 1:2], -1.0)
    pos_row = jnp.where(posr_ref[0:1, :] == gf, posr_ref[1:2, :], -1.0)
    r_hi, r_lo = _split2(route)
    wd = wd_ref[...]
    wd = wd.reshape(wd.shape[0] * wd.shape[1], wd.shape[2])
    for chunk in range(tm // SORTED_ROWS):
        @pl.when(n_rows > chunk * SORTED_ROWS)
        def _(base=float(chunk * SORTED_ROWS)):
            rid = lax.broadcasted_iota(I32, (SORTED_ROWS, tm), 0).astype(F32) + base
            pack = jnp.where(pos_row == rid, 1.0, 0.0).astype(BF16)
            xs = _dot(pack, h_ref[...]).astype(BF16)
            gs = _dot(pack, r_hi) + _dot(pack, r_lo)
            acts = []
            for j in range(EXPERTS_PER_GROUP):
                gcol = _lane_pick(gs, g * EXPERTS_PER_GROUP + j)
                acts.append((_silu(_dot(xs, wg_ref[j])) * _dot(xs, wu_ref[j]) * gcol).astype(BF16))
            ys = _dot(jnp.concatenate(acts, axis=1), wd)
            cid = lax.broadcasted_iota(I32, (tm, SORTED_ROWS), 1).astype(F32) + base
            unpack = jnp.where(pos_col == cid, 1.0, 0.0).astype(BF16)
            acc_ref[...] += _dot(unpack, ys.astype(BF16))

    @pl.when(g == pl.num_programs(1) - 1)
    def _():
        x = x_ref[...] + gt_ref[...] * acc_ref[...]
        if fg_ref is not None:
            x = x * lax.rsqrt(jnp.mean(x * x, axis=-1, keepdims=True) + RMS_EPS) * fg_ref[...]
        o_ref[...] = x


def moe_ffn_sorted(h2, route, x1, gt, wg_b, wu_b, wd_b, final_g, *, tm, rows_per_seq):
    m, d = x1.shape
    _, _, de = wg_b.shape
    npos = EXPERTS_PER_GROUP
    n_tiles = m // tm
    group = route[:, GROUP_LANE].reshape(n_tiles, 1, tm)
    counts = jnp.sum(group == jnp.arange(N_GROUPS, dtype=F32)[None, :, None], axis=2, dtype=I32)
    row = lambda i, g, cnt: (i, 0)
    gt_spec = _mod_spec(gt, tm, rows_per_seq)
    gt_spec = pl.BlockSpec(gt_spec.block_shape, lambda i, g, cnt, f=gt_spec.index_map: f(i))
    in_specs = [pl.BlockSpec((tm, d), row), pl.BlockSpec((tm, LANES), row),
                pl.BlockSpec((tm, d), row), gt_spec,
                pl.BlockSpec((npos, d, de), lambda i, g, cnt: (g, 0, 0)),
                pl.BlockSpec((npos, d, de), lambda i, g, cnt: (g, 0, 0)),
                pl.BlockSpec((npos, de, d), lambda i, g, cnt: (g, 0, 0))]
    args = [h2, route, x1, gt, wg_b, wu_b, wd_b]
    if final_g is not None:
        in_specs.append(pl.BlockSpec((1, d), lambda i, g, cnt: (0, 0)))
        args.append(final_g)
    grid_spec = pltpu.PrefetchScalarGridSpec(
        num_scalar_prefetch=1,
        grid=(n_tiles, N_GROUPS),
        in_specs=in_specs,
        out_specs=pl.BlockSpec((tm, d), row),
        scratch_shapes=[pltpu.VMEM((tm, d), F32), pltpu.VMEM((tm, LANES), F32), pltpu.VMEM((LANES, tm), F32)],
    )
    return pl.pallas_call(
        functools.partial(_moe_sorted_kernel, final=final_g is not None),
        grid_spec=grid_spec,
        out_shape=jax.ShapeDtypeStruct((m, d), F32),
        compiler_params=_params("parallel", "arbitrary"),
        name="moe_ffn_sorted",
    )(counts.reshape(-1), *args)


def _head_rows(mats):
    rowi = lax.broadcasted_iota(I32, mats[0].shape, 0)
    out = mats[0]
    for kvh in range(1, N_KV_HEADS):
        out = jnp.where((rowi >> 1) == kvh, mats[kvh], out)
    return out


def _kv_rows(page_ref):
    return [page_ref[pl.ds(kvh, PAGE_SIZE, stride=N_KV_HEADS), :] for kvh in range(N_KV_HEADS)]


def _new_token_logits(q, kn):
    qf = q.astype(F32)
    prods = [qf * kn[kvh:kvh + 1, :].astype(BF16).astype(F32) for kvh in range(N_KV_HEADS)]
    return jnp.sum(_head_rows(prods), axis=1, keepdims=True)


def _new_token_values(vn):
    return _head_rows([jnp.broadcast_to(vn[kvh:kvh + 1, :].astype(BF16).astype(F32), (HEAD_ROWS, HEAD_DIM))
                       for kvh in range(N_KV_HEADS)])


def _page_spec(layer, n_pages, page_of):
    return pl.BlockSpec((None, None, PAGE_ROWS, HEAD_DIM),
                        lambda b, c, pt: (layer, pt[b * n_pages + page_of(c)], 0, 0))


def _seq_spec(rows, width):
    return pl.BlockSpec((None, rows, width), lambda b, c, pt: (b, 0, 0))


def _suffix_total_matrix(n):
    upper, ones = _suffix_matrices(n)
    return jnp.concatenate([upper, ones], axis=1)


def _suffix_and_total(xs, suffix_total):
    sums = []
    for x in xs:
        hi, lo = _split2(x)
        sums.append(_dot(hi, suffix_total) + _dot(lo, suffix_total))
    return [s[:, :PAGE_SIZE] for s in sums], [s[:, PAGE_SIZE:] for s in sums]


def _page_logits(q, k_refs):
    return [_head_rows([_dot_nt(q, ks.astype(BF16)) for ks in _kv_rows(ref)]) for ref in k_refs]


def _page_values(w_pages, v_refs):
    pv = [jnp.zeros((HEAD_ROWS, HEAD_DIM), F32)] * N_KV_HEADS
    for w, ref in zip(w_pages, v_refs):
        wb = w.astype(BF16)
        pv = [a + _dot(wb, vs.astype(BF16)) for a, vs in zip(pv, _kv_rows(ref))]
    return _head_rows(pv)


def _carry_back(car, sufs, tots):
    after = [None] * len(sufs)
    for r in reversed(range(len(sufs))):
        after[r] = car + sufs[r]
        car = car + tots[r]
    return after, car


def _stream_pages(pt_ref, sources, bufs, sem, *, pages, n_pages, nc):
    step = pl.program_id(0) * nc + pl.program_id(1)
    total = pl.num_programs(0) * nc

    def copies(s, slot):
        first = (s // nc) * n_pages + (nc - 1 - s % nc) * pages
        return [pltpu.make_async_copy(src.at[layer, pt_ref[first + r]], buf.at[slot, r], sem.at[slot, i, r])
                for r in range(pages) for i, ((src, layer), buf) in enumerate(zip(sources, bufs))]

    slot = step % 2

    @pl.when(step == 0)
    def _():
        for cp in copies(step, slot):
            cp.start()

    @pl.when(step + 1 < total)
    def _():
        for cp in copies(step + 1, 1 - slot):
            cp.start()

    for cp in copies(step, slot):
        cp.wait()
    return slot


def _stream_scratch(pages, shapes):
    return ([pltpu.VMEM((2, pages) + shape, F32) for shape in shapes]
            + [pltpu.SemaphoreType.DMA((2, len(shapes), pages))])


def _fox_dec_kernel(pt_ref, q_ref, kn_ref, vn_ref, lfn_ref, ck_ref, cv_ref, clf_ref, o_ref,
                    m_s, l_s, acc_s, car_s, kbuf, vbuf, lfbuf, sem, *, pages, n_pages, nc, layer, fox_layer):
    slot = _stream_pages(pt_ref, [(ck_ref, layer), (cv_ref, layer), (clf_ref, fox_layer)],
                         [kbuf, vbuf, lfbuf], sem, pages=pages, n_pages=n_pages, nc=nc)
    k_refs = [kbuf.at[slot, r] for r in range(pages)]
    v_refs = [vbuf.at[slot, r] for r in range(pages)]
    lf_refs = [lfbuf.at[slot, r] for r in range(pages)]
    c = pl.program_id(1)
    q = q_ref[...]
    suffix_total = _suffix_total_matrix(PAGE_SIZE)

    @pl.when(c == 0)
    def _():
        m_s[...] = jnp.broadcast_to(_new_token_logits(q, kn_ref[...]), m_s.shape)
        l_s[...] = jnp.ones(l_s.shape, F32)
        acc_s[...] = _new_token_values(vn_ref[...])
        car_s[...] = lfn_ref[...]

    pad = jnp.zeros((HEAD_ROWS - N_HEADS, PAGE_SIZE), F32)
    sufs, tots = _suffix_and_total([jnp.concatenate([ref[...], pad], axis=0) for ref in lf_refs], suffix_total)
    qk = _page_logits(q, k_refs)
    decay, car_s[...] = _carry_back(car_s[...], sufs, tots)
    s_pages = [a + b for a, b in zip(qk, decay)]

    m_old = m_s[...]
    mx = functools.reduce(jnp.maximum, s_pages)
    m_new = jnp.maximum(m_old, jnp.max(mx, axis=1, keepdims=True))
    alpha = jnp.exp(m_old - m_new)
    p_pages = [jnp.exp(s - m_new) for s in s_pages]
    l_s[...] = l_s[...] * alpha + jnp.sum(functools.reduce(jnp.add, p_pages), axis=1, keepdims=True)
    acc_s[...] = acc_s[...] * alpha + _page_values(p_pages, v_refs)
    m_s[...] = m_new

    @pl.when(c == pl.num_programs(1) - 1)
    def _():
        o_ref[...] = acc_s[...] / l_s[...]


def fox_attention_decode(q16, kn, vn, lfn, cache_k, cache_v, cache_lft, page_table, layer, fox_layer):
    b = q16.shape[0]
    n_pages = page_table.shape[1]
    pages = min(8, n_pages)
    nc = n_pages // pages
    hbm = pl.BlockSpec(memory_space=pl.ANY)
    grid_spec = pltpu.PrefetchScalarGridSpec(
        num_scalar_prefetch=1,
        grid=(b, nc),
        in_specs=[_seq_spec(HEAD_ROWS, HEAD_DIM), _seq_spec(N_KV_HEADS, HEAD_DIM),
                  _seq_spec(N_KV_HEADS, HEAD_DIM), _seq_spec(HEAD_ROWS, PAGE_SIZE), hbm, hbm, hbm],
        out_specs=_seq_spec(HEAD_ROWS, HEAD_DIM),
        scratch_shapes=[pltpu.VMEM((HEAD_ROWS, LANES), F32)] * 4
        + _stream_scratch(pages, [(PAGE_ROWS, HEAD_DIM), (PAGE_ROWS, HEAD_DIM), (N_HEADS, PAGE_SIZE)]),
    )
    return pl.pallas_call(
        functools.partial(_fox_dec_kernel, pages=pages, n_pages=n_pages, nc=nc, layer=layer,
                          fox_layer=fox_layer),
        grid_spec=grid_spec,
        out_shape=jax.ShapeDtypeStruct((b, HEAD_ROWS, HEAD_DIM), F32),
        compiler_params=_params("arbitrary", "arbitrary"),
        name="fox_attention_decode",
    )(page_table.reshape(-1), q16, kn, vn, lfn, cache_k, cache_v, cache_lft)


def _sb_dec_kernel(pt_ref, q_ref, ck_ref, cv_ref, o_ref, acc_s, car_s, kbuf, vbuf, sem,
                   *, pages, n_pages, nc, layer):
    slot = _stream_pages(pt_ref, [(ck_ref, layer), (cv_ref, layer)], [kbuf, vbuf], sem,
                         pages=pages, n_pages=n_pages, nc=nc)
    k_refs = [kbuf.at[slot, r] for r in range(pages)]
    v_refs = [vbuf.at[slot, r] for r in range(pages)]
    c = pl.program_id(1)
    q = q_ref[...]
    suffix_total = _suffix_total_matrix(PAGE_SIZE)

    @pl.when(c == 0)
    def _():
        acc_s[...] = jnp.zeros(acc_s.shape, F32)
        car_s[...] = jnp.zeros(car_s.shape, F32)

    zs = _page_logits(q, k_refs)
    lks = [_neg_softplus(z) for z in zs]
    sufs, tots = _suffix_and_total(lks, suffix_total)
    after, car_s[...] = _carry_back(car_s[...], sufs, tots)
    a_pages = [jnp.exp(z + lk + af) for z, lk, af in zip(zs, lks, after)]
    acc_s[...] += _page_values(a_pages, v_refs)

    @pl.when(c == pl.num_programs(1) - 1)
    def _():
        o_ref[...] = acc_s[...]


def sb_attention_decode(q16, cache_k, cache_v, page_table, layer):
    b = q16.shape[0]
    n_pages = page_table.shape[1]
    pages = min(8, n_pages)
    nc = n_pages // pages
    hbm = pl.BlockSpec(memory_space=pl.ANY)
    grid_spec = pltpu.PrefetchScalarGridSpec(
        num_scalar_prefetch=1,
        grid=(b, nc),
        in_specs=[_seq_spec(HEAD_ROWS, HEAD_DIM), hbm, hbm],
        out_specs=_seq_spec(HEAD_ROWS, HEAD_DIM),
        scratch_shapes=[pltpu.VMEM((HEAD_ROWS, LANES), F32)] * 2
        + _stream_scratch(pages, [(PAGE_ROWS, HEAD_DIM), (PAGE_ROWS, HEAD_DIM)]),
    )
    return pl.pallas_call(
        functools.partial(_sb_dec_kernel, pages=pages, n_pages=n_pages, nc=nc, layer=layer),
        grid_spec=grid_spec,
        out_shape=jax.ShapeDtypeStruct((b, HEAD_ROWS, HEAD_DIM), F32),
        compiler_params=_params("arbitrary", "arbitrary"),
        name="sb_attention_decode",
    )(page_table.reshape(-1), q16, cache_k, cache_v)


def _page_head_sums(page_ref):
    x = page_ref[...]
    s8 = jnp.sum(x.reshape(PAGE_ROWS // 8, 8, HEAD_DIM), axis=0)
    return s8[:N_KV_HEADS] + s8[N_KV_HEADS:]


def _moba_sel_kernel(pt_ref, q_ref, *refs, pages, nb):
    k_refs = refs[:pages]
    km_ref, sel_ref = refs[pages:]
    c = pl.program_id(1)
    bps = pages // 2
    sums = [_page_head_sums(k_refs[r]) for r in range(pages)]
    means = [(sums[2 * j] + sums[2 * j + 1]) * (1.0 / MOBA_BLOCK) for j in range(bps)]
    row0 = pl.multiple_of(c * bps, bps)
    for kvh in range(N_KV_HEADS):
        km_ref[kvh, pl.ds(row0, bps), :] = jnp.concatenate([mj[kvh:kvh + 1] for mj in means], axis=0)

    @pl.when(c == pl.num_programs(1) - 1)
    def _():
        q = q_ref[...]
        lane = lax.broadcasted_iota(I32, (nb, LANES), 1)
        gate = jnp.zeros((nb, LANES), F32)
        for kvh in range(N_KV_HEADS):
            hi, lo = _split2(km_ref[kvh])
            gate = jnp.where((lane >> 1) == kvh, _dot_nt(hi, q) + _dot_nt(lo, q), gate)
        sel = _top_mask(gate, nb, nb, 0)
        blk = lax.broadcasted_iota(I32, (nb, LANES), 0).astype(F32)
        rows = []
        for _ in range(MOBA_TOPK):
            first = jnp.min(jnp.where(sel, blk, float(nb)), axis=0, keepdims=True)
            rows.append(first.astype(I32))
            sel = sel & (blk != first)
        rows.append(jnp.zeros((8 - MOBA_TOPK, LANES), I32))
        sel_ref[...] = jnp.concatenate(rows, axis=0)


def moba_select_decode(q128, cache_k, page_table, layer):
    b = q128.shape[0]
    n_pages = page_table.shape[1]
    nb = n_pages * PAGE_SIZE // MOBA_BLOCK
    pages = min(16, n_pages)
    nc = n_pages // pages
    k_specs = [_page_spec(layer, n_pages, lambda c, r=r: c * pages + r) for r in range(pages)]
    grid_spec = pltpu.PrefetchScalarGridSpec(
        num_scalar_prefetch=1,
        grid=(b, nc),
        in_specs=[_seq_spec(LANES, HEAD_DIM)] + k_specs,
        out_specs=[pl.BlockSpec((None, N_KV_HEADS, nb, HEAD_DIM), lambda b, c, pt: (b, 0, 0, 0)),
                   _seq_spec(8, LANES)],
    )
    _, sel = pl.pallas_call(
        functools.partial(_moba_sel_kernel, pages=pages, nb=nb),
        grid_spec=grid_spec,
        out_shape=[jax.ShapeDtypeStruct((b, N_KV_HEADS, nb, HEAD_DIM), F32),
                   jax.ShapeDtypeStruct((b, 8, LANES), I32)],
        compiler_params=_params("parallel", "arbitrary"),
        name="moba_select_decode",
    )(page_table.reshape(-1), q128, *([cache_k] * pages))
    return sel


def _moba_dec_kernel(pt_ref, sel_ref, q_ref, kn_ref, vn_ref, bt_ref, *refs, nb):
    n = MOBA_TOPK * 2
    k_refs = refs[:n]
    v_refs = refs[n:2 * n]
    o_ref = refs[2 * n]
    b = pl.program_id(0)
    h = pl.program_id(1)
    head_rows = pl.ds(h // GROUP, PAGE_SIZE, stride=N_KV_HEADS)
    q = q_ref[...]
    bt = bt_ref[...]
    far = bt[:, 2 * PAGE_SIZE:3 * PAGE_SIZE]
    s_new = _new_token_logits(q, kn_ref[...]) + bt[:, 3 * PAGE_SIZE:3 * PAGE_SIZE + 1]

    s_piece = []
    for slot in range(MOBA_TOPK):
        last = sel_ref[(b * MOBA_TOPK + slot) * N_HEADS + h] == nb - 1
        for r in range(2):
            kp = k_refs[slot * 2 + r][head_rows, :].astype(BF16)
            bias = jnp.where(last, bt[:, r * PAGE_SIZE:(r + 1) * PAGE_SIZE], far)
            s_piece.append(_dot_nt(q, kp) + bias)
    mx = functools.reduce(jnp.maximum, s_piece)
    m = jnp.maximum(jnp.max(mx, axis=1, keepdims=True), s_new)
    p_piece = [jnp.exp(s - m) for s in s_piece]
    p_new = jnp.exp(s_new - m)
    l = jnp.sum(functools.reduce(jnp.add, p_piece), axis=1, keepdims=True) + p_new
    acc = p_new.astype(BF16).astype(F32) * _new_token_values(vn_ref[...])
    for i in range(n):
        acc = acc + _dot(p_piece[i].astype(BF16), v_refs[i][head_rows, :].astype(BF16))
    o = acc / l

    @pl.when(h == 0)
    def _():
        o_ref[...] = jnp.zeros(o_ref.shape, F32)

    rowi = lax.broadcasted_iota(I32, o.shape, 0)
    o_ref[pl.ds(h, 1), :] = jnp.sum(jnp.where(rowi == h, o, 0.0), axis=0, keepdims=True)


def moba_attention_decode(q16, kn, vn, bias_tab, sel, cache_k, cache_v, page_table, layer):
    b = q16.shape[0]
    n_pages = page_table.shape[1]
    nb = n_pages * PAGE_SIZE // MOBA_BLOCK

    def page_spec(slot, r):
        def index(b, h, pt, sel):
            blk = sel[(b * MOBA_TOPK + slot) * N_HEADS + h]
            return (layer, pt[b * n_pages + 2 * blk + r], 0, 0)
        return pl.BlockSpec((None, None, PAGE_ROWS, HEAD_DIM), index)

    kv_specs = [page_spec(slot, r) for slot in range(MOBA_TOPK) for r in range(2)]
    seq = lambda rows, width: pl.BlockSpec((None, rows, width), lambda b, h, pt, sel: (b, 0, 0))
    grid_spec = pltpu.PrefetchScalarGridSpec(
        num_scalar_prefetch=2,
        grid=(b, N_HEADS),
        in_specs=[seq(HEAD_ROWS, HEAD_DIM), seq(N_KV_HEADS, HEAD_DIM), seq(N_KV_HEADS, HEAD_DIM),
                  pl.BlockSpec((HEAD_ROWS, 4 * PAGE_SIZE), lambda b, h, pt, sel: (0, 0))] + kv_specs + kv_specs,
        out_specs=seq(HEAD_ROWS, HEAD_DIM),
    )
    n = len(kv_specs)
    return pl.pallas_call(
        functools.partial(_moba_dec_kernel, nb=nb),
        grid_spec=grid_spec,
        out_shape=jax.ShapeDtypeStruct((b, HEAD_ROWS, HEAD_DIM), F32),
        compiler_params=_params("parallel", "arbitrary"),
        name="moba_attention_decode",
    )(page_table.reshape(-1), sel.reshape(-1), q16, kn, vn, bias_tab, *([cache_k] * n), *([cache_v] * n))


def _rel_bias_by_distance(rel_bias, n):
    dist = jnp.arange(n, dtype=I32)
    max_exact = N_BUCKETS // 2
    large = max_exact + (jnp.log(jnp.maximum(dist, 1).astype(F32) / max_exact)
                         / math.log(MAX_DISTANCE / max_exact)
                         * (N_BUCKETS - max_exact)).astype(I32)
    large = jnp.minimum(large, N_BUCKETS - 1)
    return rel_bias[jnp.where(dist < max_exact, dist, large)]


def _pad_lanes(w, b):
    n = w.shape[1]
    return jnp.pad(w, ((0, 0), (0, LANES - n))), jnp.pad(b, (0, LANES - n)).reshape(1, LANES)


def _trunk(x, mod, cache, rel_bias, weights):
    (norm_g, final_g, w_in_b, w_out_b, fgates, w_router, b_router, wg_b, wu_b, wd_b) = weights
    b, t, d = x.shape
    m = b * t
    depth = w_in_b.shape[0]
    paged = cache is not None
    if paged:
        assert t == 1
        cache_k, cache_v, cache_lft, page_table = cache
        n_pages = page_table.shape[1]
        nb_past = n_pages * PAGE_SIZE // MOBA_BLOCK
        assert (n_pages * PAGE_SIZE) % MOBA_BLOCK == 0 and nb_past >= MOBA_TOPK
        tm_tok = tm_moe = m
        bias_d = _rel_bias_by_distance(rel_bias, 2 * MOBA_BLOCK + 1)
        assert MAX_DISTANCE <= MOBA_BLOCK
        near = bias_d[MOBA_BLOCK - jnp.arange(MOBA_BLOCK)].T
        tab = jnp.concatenate([near, jnp.broadcast_to(bias_d[2 * MOBA_BLOCK][:, None], (N_HEADS, PAGE_SIZE)),
                               jnp.broadcast_to(bias_d[0][:, None], (N_HEADS, PAGE_SIZE))], axis=1)
        bias_tab = jnp.pad(tab, ((0, HEAD_ROWS - N_HEADS), (0, 0)))
    else:
        assert t % MOBA_BLOCK == 0 and t // MOBA_BLOCK <= LANES
        tm_tok = min(512, t)
        tm_moe = min(1024, t)
        assert MAX_DISTANCE <= MOBA_BLOCK
        bias0, bias1 = rel_bias_tiles(rel_bias)

    q_scale = HEAD_DIM ** -0.5 * (1.0 if paged else LOG2E)
    wr, br = router_operands(w_router, b_router, transposed=tm_tok >= LANES)

    def mod_part(l, j):
        part = mod[l, :, j * d:(j + 1) * d]
        return part if paged else part.reshape(b, 1, d)

    xf = x.reshape(m, d)
    new_k, new_v, new_logf = [], [], []
    for l in range(depth):
        kind = l % N_MIXERS
        sh1, sc1, gt1, sh2, sc2, gt2 = [mod_part(l, j) for j in range(6)]
        fg = fgates[l // N_MIXERS] if kind == 0 else None
        outs = norm_qkv(xf, norm_g[l, 0].reshape(1, d), sc1, sh1, w_in_b[l], fg,
                        tm=tm_tok, rows_per_seq=t, q_scale=q_scale)
        q, k, v, kb, vb = outs[:5]
        new_k.append(k.reshape(b, t, N_KV_HEADS, HEAD_DIM))
        new_v.append(v.reshape(b, t, N_KV_HEADS, HEAD_DIM))
        if kind == 0:
            lf = outs[5]
            new_logf.append(lf[:, :N_HEADS].reshape(b, t, N_HEADS))
        if paged:
            q16 = jnp.pad(q.reshape(b, N_HEADS, HEAD_DIM), ((0, 0), (0, HEAD_ROWS - N_HEADS), (0, 0)))
            kn = k.reshape(b, N_KV_HEADS, HEAD_DIM)
            vn = v.reshape(b, N_KV_HEADS, HEAD_DIM)
            if kind == 0:
                lfn = jnp.broadcast_to(
                    jnp.pad(lf[:, :N_HEADS], ((0, 0), (0, HEAD_ROWS - N_HEADS)))[:, :, None],
                    (b, HEAD_ROWS, PAGE_SIZE))
                o16 = fox_attention_decode(q16, kn, vn, lfn, cache_k, cache_v, cache_lft, page_table,
                                           l, l // N_MIXERS)
            elif kind == 1:
                q128 = jnp.pad(q.reshape(b, N_HEADS, HEAD_DIM), ((0, 0), (0, LANES - N_HEADS), (0, 0)))
                sel = moba_select_decode(q128, cache_k, page_table, l)
                sel = sel[:, :MOBA_TOPK, :N_HEADS]
                o16 = moba_attention_decode(q16, kn, vn, bias_tab, sel, cache_k, cache_v, page_table, l)
            else:
                o16 = sb_attention_decode(q16, cache_k, cache_v, page_table, l)
            o = o16[:, :N_HEADS, :].reshape(m, Q_DIM).astype(BF16)
        else:
            q3 = q.reshape(b, t, Q_DIM)
            kb3 = kb.reshape(b, t, KV_DIM)
            vb3 = vb.reshape(b, t, KV_DIM)
            if kind == 0:
                o = fox_attention_prompt(q3, kb3, vb3, cumsum_time(lf.reshape(b, t, LANES)))
            elif kind == 1:
                kmean = block_mean(k.reshape(b, t * N_KV_HEADS, HEAD_DIM))
                o = moba_attention_prompt(q3, kb3, vb3, kmean, bias0, bias1)
            else:
                o = sb_attention_prompt(q3, kb3, vb3)
            o = o.reshape(m, Q_DIM)
        x1, h2, gates = out_proj_router(o, xf, w_out_b[l], gt1, norm_g[l, 1].reshape(1, d), sc2, sh2,
                                        wr, br, tm=tm_tok, rows_per_seq=t)
        fin = final_g.reshape(1, d) if l == depth - 1 else None
        moe = moe_ffn if paged else moe_ffn_sorted
        xf = moe(h2, gates, x1, gt2, wg_b[l], wu_b[l], wd_b[l], fin, tm=tm_moe, rows_per_seq=t)
    return xf.reshape(b, t, d), jnp.stack(new_k), jnp.stack(new_v), jnp.stack(new_logf)


def kernel(x_prompt, x_sample, cache_k, cache_v, cache_logf, page_table, c_prompt, c_sample, rel_bias,
           w_ada, b_ada, norm_g, final_g, w_in, w_out, w_fgate, b_fgate, w_router, b_router,
           w_gate, w_up, w_down):
    n_prompt = c_prompt.shape[0]
    mod = ada_modulation(jnp.concatenate([c_prompt, c_sample], axis=0), w_ada, b_ada)

    fgates = []
    for a in range(w_fgate.shape[0]):
        wf, bf = _pad_lanes(w_fgate[a], b_fgate[a])
        fgates.append((wf.astype(BF16), bf))
    weights = (norm_g, final_g, w_in.astype(BF16), w_out.astype(BF16), fgates, w_router, b_router,
               w_gate.astype(BF16), w_up.astype(BF16), w_down.astype(BF16))

    depth, n_pool = cache_k.shape[:2]
    cache = (cache_k.reshape(depth, n_pool, PAGE_ROWS, HEAD_DIM),
             cache_v.reshape(depth, n_pool, PAGE_ROWS, HEAD_DIM),
             jnp.swapaxes(cache_logf, 2, 3), page_table)

    y_p, k_p, v_p, lf_p = _trunk(x_prompt, mod[:, :n_prompt], None, rel_bias, weights)
    y_s, k_s, v_s, lf_s = _trunk(x_sample, mod[:, n_prompt:], cache, rel_bias, weights)
    return (y_p, y_s, k_p, v_p, lf_p, k_s, v_s, lf_s)
```

```python
import functools
import math

import jax
import jax.numpy as jnp
from jax import lax
from jax.experimental import pallas as pl
from jax.experimental.pallas import tpu as pltpu

F32 = jnp.float32
BF16 = jnp.bfloat16
I32 = jnp.int32

N_MIXERS = 3
N_HEADS = 8
N_KV_HEADS = 4
GROUP = N_HEADS // N_KV_HEADS
HEAD_DIM = 128
Q_DIM = N_HEADS * HEAD_DIM
KV_DIM = N_KV_HEADS * HEAD_DIM
PAGE_SIZE = 128
MOBA_BLOCK = 256
MOBA_TOPK = 3
N_BUCKETS = 32
MAX_DISTANCE = 128
N_EXPERTS = 16
N_GROUPS = 4
EXPERTS_PER_GROUP = N_EXPERTS // N_GROUPS
TOP_K = 2
GROUP_LANE = N_EXPERTS
SORTED_ROWS = 256
RMS_EPS = 1e-6

PAGE_ROWS = PAGE_SIZE * N_KV_HEADS
LANES = 128
HEAD_ROWS = 16
MASKED = -1e30
LOG2E = math.log2(math.e)
VMEM_LIMIT = 56 * 1024 * 1024


def _params(*sem):
    return pltpu.CompilerParams(dimension_semantics=sem, vmem_limit_bytes=VMEM_LIMIT)


def _dot(a, b):
    return jnp.dot(a, b, preferred_element_type=F32)


def _dot_nt(a, b):
    return lax.dot_general(a, b, (((1,), (1,)), ((), ())), preferred_element_type=F32)


def _split2(x):
    hi = x.astype(BF16)
    lo = (x - hi.astype(F32)).astype(BF16)
    return hi, lo


def _split3(x):
    hi = x.astype(BF16)
    r = x - hi.astype(F32)
    mid = r.astype(BF16)
    lo = (r - mid.astype(F32)).astype(BF16)
    return hi, mid, lo


def _neg_softplus(z):
    return -(jnp.maximum(z, 0.0) + jnp.log1p(jnp.exp(-jnp.abs(z))))


def _silu(x):
    return x / (1.0 + jnp.exp(-x))


def _ada_kernel(c_ref, w_ref, b_ref, o_ref):
    s = _silu(c_ref[...])
    o_ref[...] = _dot(s.astype(BF16), w_ref[...].astype(BF16)) + b_ref[...]


def ada_modulation(c_all, w_ada, b_ada):
    depth, d, n = w_ada.shape
    mc = c_all.shape[0]
    tn = 1024
    return pl.pallas_call(
        _ada_kernel,
        grid=(depth, n // tn),
        in_specs=[
            pl.BlockSpec((mc, d), lambda l, j: (0, 0)),
            pl.BlockSpec((None, d, tn), lambda l, j: (l, 0, j)),
            pl.BlockSpec((None, 1, tn), lambda l, j: (l, 0, j)),
        ],
        out_specs=pl.BlockSpec((None, mc, tn), lambda l, j: (l, 0, j)),
        out_shape=jax.ShapeDtypeStruct((depth, mc, n), F32),
        compiler_params=_params("parallel", "parallel"),
        name="ada_modulation",
    )(c_all, w_ada, b_ada.reshape(depth, 1, n))


def _rms_mod(x, g, sc, sh):
    r = lax.rsqrt(jnp.mean(x * x, axis=-1, keepdims=True) + RMS_EPS)
    return (x * r * g) * (1.0 + sc) + sh


def _qkv_kernel(*refs, has_fgate, q_scale):
    if has_fgate:
        (x_ref, g_ref, sc_ref, sh_ref, w_ref, wf_ref, bf_ref,
         q_ref, k_ref, v_ref, kb_ref, vb_ref, lf_ref) = refs
    else:
        x_ref, g_ref, sc_ref, sh_ref, w_ref, q_ref, k_ref, v_ref, kb_ref, vb_ref = refs
    hb = _rms_mod(x_ref[...], g_ref[...], sc_ref[...], sh_ref[...]).astype(BF16)
    qkv = _dot(hb, w_ref[...])
    q_ref[...] = (qkv[:, :Q_DIM] * q_scale).astype(BF16)
    k = qkv[:, Q_DIM:Q_DIM + KV_DIM]
    v = qkv[:, Q_DIM + KV_DIM:]
    tm = k.shape[0]
    for kvh in range(N_KV_HEADS):
        rows = pl.ds(kvh, tm, stride=N_KV_HEADS)
        k_ref[rows, :] = k[:, kvh * HEAD_DIM:(kvh + 1) * HEAD_DIM]
        v_ref[rows, :] = v[:, kvh * HEAD_DIM:(kvh + 1) * HEAD_DIM]
    kb_ref[...] = k.astype(BF16)
    vb_ref[...] = v.astype(BF16)
    if has_fgate:
        z = _dot(hb, wf_ref[...]) + bf_ref[...]
        lf_ref[...] = jnp.minimum(z, 0.0) - jnp.log1p(jnp.exp(-jnp.abs(z)))


def _mod_spec(mod, tm, rows_per_seq):
    d = mod.shape[-1]
    if mod.ndim == 3:
        tiles = rows_per_seq // tm
        return pl.BlockSpec((None, 1, d), lambda i: (i // tiles, 0, 0))
    return pl.BlockSpec((tm, d), lambda i: (i, 0))


def norm_qkv(x, g, sc, sh, w_in_b, fgate, *, tm, rows_per_seq, q_scale):
    m, d = x.shape
    n = w_in_b.shape[1]
    has_fgate = fgate is not None
    row = lambda i: (i, 0)
    const = lambda i: (0, 0)
    in_specs = [pl.BlockSpec((tm, d), row), pl.BlockSpec((1, d), const),
                _mod_spec(sc, tm, rows_per_seq), _mod_spec(sh, tm, rows_per_seq),
                pl.BlockSpec((d, n), const)]
    args = [x, g, sc, sh, w_in_b]
    out_specs = [pl.BlockSpec((tm, Q_DIM), row), pl.BlockSpec((tm * N_KV_HEADS, HEAD_DIM), row),
                 pl.BlockSpec((tm * N_KV_HEADS, HEAD_DIM), row), pl.BlockSpec((tm, KV_DIM), row),
                 pl.BlockSpec((tm, KV_DIM), row)]
    out_shape = [jax.ShapeDtypeStruct((m, Q_DIM), BF16),
                 jax.ShapeDtypeStruct((m * N_KV_HEADS, HEAD_DIM), F32),
                 jax.ShapeDtypeStruct((m * N_KV_HEADS, HEAD_DIM), F32),
                 jax.ShapeDtypeStruct((m, KV_DIM), BF16), jax.ShapeDtypeStruct((m, KV_DIM), BF16)]
    if has_fgate:
        wf, bf = fgate
        in_specs += [pl.BlockSpec((d, LANES), const), pl.BlockSpec((1, LANES), const)]
        args += [wf, bf]
        out_specs.append(pl.BlockSpec((tm, LANES), row))
        out_shape.append(jax.ShapeDtypeStruct((m, LANES), F32))
    return pl.pallas_call(
        functools.partial(_qkv_kernel, has_fgate=has_fgate, q_scale=q_scale),
        grid=(m // tm,),
        in_specs=in_specs, out_specs=out_specs, out_shape=out_shape,
        compiler_params=_params("parallel"),
        name="norm_qkv",
    )(*args)


def _cumsum_kernel(lf_ref, dt_ref, *, chunk):
    t = lf_ref.shape[0]
    row = lax.broadcasted_iota(I32, (chunk, chunk), 0)
    col = lax.broadcasted_iota(I32, (chunk, chunk), 1)
    tri = jnp.where(col <= row, 1.0, 0.0).astype(BF16)
    carry = jnp.zeros((1, LANES), F32)
    for c in range(t // chunk):
        sl = slice(c * chunk, (c + 1) * chunk)
        hi, mid, lo = _split3(lf_ref[sl, :])
        cs = _dot(tri, hi) + _dot(tri, mid) + _dot(tri, lo) + carry
        dt_ref[:, sl] = cs.T[:N_HEADS, :]
        carry = cs[chunk - 1:chunk, :]


def cumsum_time(lf):
    b, t, _ = lf.shape
    chunk = min(256, t)
    return pl.pallas_call(
        functools.partial(_cumsum_kernel, chunk=chunk),
        grid=(b,),
        in_specs=[pl.BlockSpec((None, t, LANES), lambda i: (i, 0, 0))],
        out_specs=pl.BlockSpec((None, N_HEADS, t), lambda i: (i, 0, 0)),
        out_shape=jax.ShapeDtypeStruct((b, N_HEADS, t), F32),
        compiler_params=_params("parallel"),
        name="cumsum_time",
    )(lf)


def _lane_pick(x, idx):
    lane = lax.broadcasted_iota(I32, x.shape, 1)
    return jnp.sum(jnp.where(lane == idx, x, 0.0), axis=1, keepdims=True)


def _with_ones(v):
    return jnp.concatenate([v, jnp.ones(v.shape, v.dtype)], axis=1)


def _softmax2_first(s, v1):
    m = jnp.max(s, axis=1, keepdims=True)
    return m, _dot(jnp.exp2(s - m).astype(BF16), v1)


def _softmax2_step(s, v1, m, accl, keep=None):
    m_new = jnp.maximum(m, jnp.max(s, axis=1, keepdims=True))
    new = accl * jnp.exp2(m - m_new) + _dot(jnp.exp2(s - m_new).astype(BF16), v1)
    if keep is None:
        return m_new, new
    return jnp.where(keep, m_new, m), jnp.where(keep, new, accl)


def _softmax2_out(accl):
    return (accl[:, :HEAD_DIM] / accl[:, HEAD_DIM:HEAD_DIM + 1]).astype(BF16)


def _fox_kernel(q_ref, k_ref, v_ref, dt_ref, o_ref, *, tq):
    kvh = pl.program_id(1)
    i = pl.program_id(2)
    q0 = pl.multiple_of(i * tq, tq)
    row = lax.broadcasted_iota(I32, (tq, tq), 0)
    col = lax.broadcasted_iota(I32, (tq, tq), 1)
    causal = col <= row
    heads = [kvh * GROUP + g for g in range(GROUP)]
    qs = [q_ref[:, g * HEAD_DIM:(g + 1) * HEAD_DIM] for g in range(GROUP)]
    d0s = [dt_ref[pl.ds(h, 1), pl.ds(q0, tq)][:, 0:1] for h in heads]

    def logits(g, start, kt):
        dk = dt_ref[pl.ds(heads[g], 1), pl.ds(start, tq)]
        return _dot_nt(qs[g], kt) + (d0s[g] - dk) * LOG2E

    def step(s, vt, m, l, acc):
        m_new = jnp.maximum(m, jnp.max(s, axis=1, keepdims=True))
        alpha = jnp.exp2(m - m_new)
        p = jnp.exp2(s - m_new)
        return m_new, l * alpha + jnp.sum(p, axis=1, keepdims=True), acc * alpha + _dot(p.astype(BF16), vt)

    kt, vt = k_ref[pl.ds(q0, tq), :], v_ref[pl.ds(q0, tq), :]
    zero = (jnp.full((tq, 1), MASKED, F32), jnp.zeros((tq, 1), F32), jnp.zeros((tq, HEAD_DIM), F32))
    state = tuple(step(jnp.where(causal, logits(g, q0, kt), MASKED), vt, *zero) for g in range(GROUP))

    def body(j, state):
        start = pl.multiple_of(j * tq, tq)
        kt, vt = k_ref[pl.ds(start, tq), :], v_ref[pl.ds(start, tq), :]
        return tuple(step(logits(g, start, kt), vt, *state[g]) for g in range(GROUP))

    state = lax.fori_loop(0, i, body, state)
    for g, (_, l, acc) in enumerate(state):
        o_ref[:, g * HEAD_DIM:(g + 1) * HEAD_DIM] = (acc / l).astype(BF16)


def _attn_specs(t, tq):
    qo = pl.BlockSpec((None, tq, GROUP * HEAD_DIM), lambda b, kvh, i: (b, i, kvh))
    kv = pl.BlockSpec((None, t, HEAD_DIM), lambda b, kvh, i: (b, 0, kvh))
    return qo, kv


def fox_attention_prompt(q, kb, vb, dt):
    b, t, _ = q.shape
    tq = min(512, t)
    qo, kv = _attn_specs(t, tq)
    return pl.pallas_call(
        functools.partial(_fox_kernel, tq=tq),
        grid=(b, N_KV_HEADS, t // tq),
        in_specs=[qo, kv, kv, pl.BlockSpec((None, N_HEADS, t), lambda b, kvh, i: (b, 0, 0))],
        out_specs=qo,
        out_shape=jax.ShapeDtypeStruct((b, t, Q_DIM), BF16),
        compiler_params=_params("parallel", "parallel", "parallel"),
        name="fox_attention_prompt",
    )(q, kb, vb, dt)


def _suffix_matrices(n):
    row = lax.broadcasted_iota(I32, (n, n), 0)
    col = lax.broadcasted_iota(I32, (n, n), 1)
    return jnp.where(row > col, 1.0, 0.0).astype(BF16), jnp.ones((n, n), BF16)


def _sb_kernel(q_ref, k_ref, v_ref, o_ref, *, tq, tk):
    i = pl.program_id(2)
    q0 = pl.multiple_of(i * tq, tq)
    upper, _ = _suffix_matrices(tk)
    row = lax.broadcasted_iota(I32, (tq, tk), 0)
    col = lax.broadcasted_iota(I32, (tq, tk), 1)
    qs = [q_ref[:, g * HEAD_DIM:(g + 1) * HEAD_DIM] for g in range(GROUP)]

    def chunk(start, state, mask, r0=0):
        kt, vt = k_ref[pl.ds(start, tk), :], v_ref[pl.ds(start, tk), :]
        out = []
        for g in range(GROUP):
            c, acc = state[g]
            z = _dot_nt(qs[g][r0:], kt)
            ls = jnp.minimum(z, 0.0) - jnp.log2(1.0 + jnp.exp2(jnp.minimum(z, -z)))
            lk = ls - z
            if mask is not None:
                lk = jnp.where(mask[r0:], lk, 0.0)
            hi, lo = _split2(lk)
            after = _dot(hi, upper) + _dot(lo, upper) + c[r0:]
            a = jnp.exp2(ls + after)
            if mask is not None:
                a = jnp.where(mask[r0:], a, 0.0)
            c_new = c[r0:] + jnp.sum(lk, axis=1, keepdims=True)
            acc_new = acc[r0:] + _dot(a.astype(BF16), vt)
            if r0:
                c_new = jnp.concatenate([c[:r0], c_new], axis=0)
                acc_new = jnp.concatenate([acc[:r0], acc_new], axis=0)
            out.append((c_new, acc_new))
        return tuple(out)

    state = tuple((jnp.zeros((tq, 1), F32), jnp.zeros((tq, HEAD_DIM), F32)) for _ in range(GROUP))
    for mth in reversed(range(tq // tk)):
        start = pl.multiple_of(q0 + mth * tk, tk)
        state = chunk(start, state, (col + mth * tk) < row, r0=mth * tk)

    n_past = i * (tq // tk)

    def body(it, state):
        return chunk(pl.multiple_of((n_past - 1 - it) * tk, tk), state, None)

    state = lax.fori_loop(0, n_past, body, state)
    for g, (_, acc) in enumerate(state):
        o_ref[:, g * HEAD_DIM:(g + 1) * HEAD_DIM] = acc.astype(BF16)


def sb_attention_prompt(q, kb, vb):
    b, t, _ = q.shape
    tq = min(1024, t)
    tk = min(256, t)
    qo, kv = _attn_specs(t, tq)
    return pl.pallas_call(
        functools.partial(_sb_kernel, tq=tq, tk=tk),
        grid=(b, N_KV_HEADS, t // tq),
        in_specs=[qo, kv, kv],
        out_specs=qo,
        out_shape=jax.ShapeDtypeStruct((b, t, Q_DIM), BF16),
        compiler_params=_params("parallel", "parallel", "parallel"),
        name="sb_attention_prompt",
    )(q, kb, vb)


def _block_mean_kernel(k_ref, o_ref, *, nb):
    t = k_ref.shape[0] // N_KV_HEADS
    o_ref[...] = jnp.zeros(o_ref.shape, F32)
    for kvh in range(N_KV_HEADS):
        x = k_ref[pl.ds(kvh, t, stride=N_KV_HEADS), :]
        o_ref[kvh, 0:nb, :] = jnp.sum(x.reshape(nb, t // nb, HEAD_DIM), axis=1) * (1.0 / (t // nb))


def block_mean(k4):
    b, rows, _ = k4.shape
    nb = rows // N_KV_HEADS // MOBA_BLOCK
    return pl.pallas_call(
        functools.partial(_block_mean_kernel, nb=nb),
        grid=(b,),
        in_specs=[pl.BlockSpec((None, rows, HEAD_DIM), lambda i: (i, 0, 0))],
        out_specs=pl.BlockSpec((None, N_KV_HEADS, LANES, HEAD_DIM), lambda i: (i, 0, 0, 0)),
        out_shape=jax.ShapeDtypeStruct((b, N_KV_HEADS, LANES, HEAD_DIM), F32),
        compiler_params=_params("parallel"),
        name="block_mean",
    )(k4)


def _top_mask(gate, n_valid, n_cand, axis):
    idx = lax.broadcasted_iota(I32, gate.shape, axis)
    cnt = jnp.zeros(gate.shape, I32)
    for jp in range(n_cand):
        other = gate[:, jp:jp + 1] if axis == 1 else gate[jp:jp + 1, :]
        beats = (other > gate) | ((other == gate) & (jp < idx))
        cnt = cnt + jnp.where(beats, jnp.where(jp < n_valid, 1, 0), 0)
    return (idx < n_valid) & (cnt < MOBA_TOPK)


def _moba_kernel(q_ref, k_ref, v_ref, km_ref, b0_ref, b1_ref, o_ref, *, tq, nb):
    i = pl.program_id(2)
    q0 = pl.multiple_of(i * tq, tq)
    row = lax.broadcasted_iota(I32, (tq, tq), 0)
    col = lax.broadcasted_iota(I32, (tq, tq), 1)
    causal = col <= row
    nbp = -(-nb // 16) * 16
    km_hi, km_lo = _split2(km_ref[0:nbp, :])
    qs = [q_ref[:, g * HEAD_DIM:(g + 1) * HEAD_DIM] for g in range(GROUP)]

    def select(qh):
        gate = _dot_nt(km_hi, qh) + _dot_nt(km_lo, qh)
        sel = jnp.where(_top_mask(gate, i, nb, 0), 1.0, 0.0)
        return jnp.concatenate([sel, jnp.zeros((LANES - nbp, tq), F32)], axis=0).T

    sels = [select(qh) for qh in qs]

    def tile(j):
        start = pl.multiple_of(j * tq, tq)
        return k_ref[pl.ds(start, tq), :], _with_ones(v_ref[pl.ds(start, tq), :])

    def picked(g, j):
        return _lane_pick(sels[g], j) > 0.0

    kt, v1 = tile(i)
    state = tuple(_softmax2_first(jnp.where(causal, _dot_nt(qs[g], kt) + b0_ref[g], MASKED), v1)
                  for g in range(GROUP))
    prev = jnp.maximum(i - 1, 0)
    kt, v1 = tile(prev)
    state = tuple(_softmax2_step(_dot_nt(qs[g], kt) + b1_ref[g], v1, *state[g], keep=picked(g, i - 1))
                  for g in range(GROUP))

    def body(j, state):
        kt, v1 = tile(j)
        return tuple(_softmax2_step(_dot_nt(qs[g], kt), v1, *state[g], keep=picked(g, j))
                     for g in range(GROUP))

    state = lax.fori_loop(0, i - 1, body, state)
    for g, (_, accl) in enumerate(state):
        o_ref[:, g * HEAD_DIM:(g + 1) * HEAD_DIM] = _softmax2_out(accl)


def _bias_tile_kernel(rb_ref, b0_ref, b1_ref, *, n):
    h = pl.program_id(0)
    row = lax.broadcasted_iota(I32, (n, n), 0)
    col = lax.broadcasted_iota(I32, (n, n), 1)
    max_exact = N_BUCKETS // 2
    for out_ref, shift in ((b0_ref, 0), (b1_ref, n)):
        dist = jnp.maximum(row - col + shift, 0)
        large = max_exact + (jnp.log(jnp.maximum(dist, 1).astype(F32) / max_exact)
                             / math.log(MAX_DISTANCE / max_exact) * (N_BUCKETS - max_exact)).astype(I32)
        bucket = jnp.where(dist < max_exact, dist, jnp.minimum(large, N_BUCKETS - 1))
        acc = jnp.zeros((n, n), F32)
        for bk in range(N_BUCKETS):
            acc = jnp.where(bucket == bk, rb_ref[bk, h], acc)
        out_ref[...] = (acc - rb_ref[N_BUCKETS - 1, h]) * LOG2E


def rel_bias_tiles(rel_bias):
    n = MOBA_BLOCK
    shape = jax.ShapeDtypeStruct((N_HEADS, n, n), F32)
    spec = pl.BlockSpec((None, n, n), lambda h: (h, 0, 0))
    return pl.pallas_call(
        functools.partial(_bias_tile_kernel, n=n),
        grid=(N_HEADS,),
        in_specs=[pl.BlockSpec(memory_space=pltpu.SMEM)],
        out_specs=[spec, spec],
        out_shape=[shape, shape],
        compiler_params=_params("parallel"),
        name="rel_bias_tiles",
    )(rel_bias)


def moba_attention_prompt(q, kb, vb, kmean, bias0, bias1):
    b, t, _ = q.shape
    tq = MOBA_BLOCK
    nb = t // tq
    qo, kv = _attn_specs(t, tq)
    head_pair = lambda b, kvh, i: (kvh, 0, 0)
    return pl.pallas_call(
        functools.partial(_moba_kernel, tq=tq, nb=nb),
        grid=(b, N_KV_HEADS, nb),
        in_specs=[qo, kv, kv,
                  pl.BlockSpec((None, None, LANES, HEAD_DIM), lambda b, kvh, i: (b, kvh, 0, 0)),
                  pl.BlockSpec((GROUP, tq, tq), head_pair),
                  pl.BlockSpec((GROUP, tq, tq), head_pair)],
        out_specs=qo,
        out_shape=jax.ShapeDtypeStruct((b, t, Q_DIM), BF16),
        compiler_params=_params("parallel", "parallel", "parallel"),
        name="moba_attention_prompt",
    )(q, kb, vb, kmean, bias0, bias1)


def _route(logits):
    idx = lax.broadcasted_iota(I32, logits.shape, 1)
    valid = idx < N_EXPERTS
    lg = jnp.where(valid, logits, MASKED)
    e = jnp.where(valid, jnp.exp(lg - jnp.max(lg, axis=1, keepdims=True)), 0.0)
    grp = idx >> 2

    def peers(x, shifts):
        for s in shifts:
            for sh in (s, LANES - s):
                oi = pltpu.roll(idx, sh, 1)
                yield pltpu.roll(x, sh, 1), oi, oi < N_EXPERTS

    cnt = jnp.zeros(logits.shape, I32)
    for oe, oi, ok in peers(e, (1, 2, 3)):
        beats = ok & ((oi >> 2) == grp) & ((oe > e) | ((oe == e) & (oi < idx)))
        cnt = cnt + jnp.where(beats, 1, 0)
    top2 = valid & (cnt < TOP_K)
    t2e = jnp.where(top2, e, 0.0)
    score = t2e
    for ot, oi, ok in peers(t2e, (1, 2, 3)):
        score = score + jnp.where(ok & ((oi >> 2) == grp), ot, 0.0)
    lost = jnp.zeros(logits.shape, I32)
    for osc, oi, ok in peers(score, (4, 8, 12)):
        beats = ok & ((osc > score) | ((osc == score) & ((oi >> 2) < grp)))
        lost = lost + jnp.where(beats, 1, 0)
    return jnp.where(top2 & valid & (lost == 0), e / score, 0.0)


def _route_t(lg):
    npos, ngrp = EXPERTS_PER_GROUP, N_GROUPS
    e = jnp.exp(lg - jnp.max(lg, axis=0, keepdims=True))
    pos = [jnp.concatenate([e[g * npos + a:g * npos + a + 1, :] for g in range(ngrp)], axis=0)
           for a in range(npos)]
    top2 = []
    for a in range(npos):
        cnt = jnp.zeros(pos[a].shape, I32)
        for b in range(npos):
            if b != a:
                cnt = cnt + jnp.where((pos[b] >= pos[a]) if b < a else (pos[b] > pos[a]), 1, 0)
        top2.append(cnt < TOP_K)
    score = functools.reduce(jnp.add, [jnp.where(t, p, 0.0) for t, p in zip(top2, pos)])
    rows = [score[g:g + 1, :] for g in range(ngrp)]
    won = []
    for g in range(ngrp):
        lost = jnp.zeros(rows[g].shape, I32)
        for o in range(ngrp):
            if o != g:
                lost = lost + jnp.where((rows[o] >= rows[g]) if o < g else (rows[o] > rows[g]), 1, 0)
        won.append(jnp.where(lost == 0, 1.0, 0.0))
    group_id = functools.reduce(jnp.add, [won[g] * float(g) for g in range(ngrp)])
    won = jnp.concatenate(won, axis=0) > 0.0
    gate = [jnp.where(t & won, p / score, 0.0) for t, p in zip(top2, pos)]
    gates = jnp.concatenate([gate[a][g:g + 1, :] for g in range(ngrp) for a in range(npos)], axis=0)
    return gates, group_id


def _oproj_kernel(o_ref, x_ref, w_ref, gt_ref, g_ref, sc_ref, sh_ref, wr_ref, br_ref,
                  x1_ref, h2_ref, gates_ref, *, transposed):
    x1 = x_ref[...] + gt_ref[...] * _dot(o_ref[...], w_ref[...])
    x1_ref[...] = x1
    h2 = _rms_mod(x1, g_ref[...], sc_ref[...], sh_ref[...])
    hi, lo = _split2(h2)
    h2_ref[...] = hi
    if transposed:
        both = _dot_nt(wr_ref[...], hi)
        lg = both[:N_EXPERTS] + both[N_EXPERTS:] + _dot_nt(wr_ref[0:N_EXPERTS, :], lo) + br_ref[:, 0:1]
        gates, group_id = _route_t(lg)
        pad = jnp.zeros((LANES - N_EXPERTS - 1, gates.shape[1]), F32)
        gates_ref[...] = jnp.concatenate([gates, group_id, pad], axis=0).T
    else:
        both = _dot(hi, wr_ref[...])
        logits = both[:, :LANES] + both[:, LANES:] + _dot(lo, wr_ref[:, 0:LANES]) + br_ref[...]
        gates_ref[...] = _route(logits)


def router_operands(w_router, b_router, transposed):
    if transposed:
        hi, lo = _split2(w_router.T)
        return jnp.concatenate([hi, lo], axis=0), jnp.broadcast_to(b_router[:, None], (N_EXPERTS, LANES))
    wr, br = _pad_lanes(w_router, b_router)
    hi, lo = _split2(wr)
    return jnp.concatenate([hi, lo], axis=1), br


def out_proj_router(o, x, w_out_b, gt, g2, sc, sh, wr, br, *, tm, rows_per_seq):
    m, d = x.shape
    transposed = wr.shape[1] == d
    row = lambda i: (i, 0)
    const = lambda i: (0, 0)
    ms = lambda a: _mod_spec(a, tm, rows_per_seq)
    return pl.pallas_call(
        functools.partial(_oproj_kernel, transposed=transposed),
        grid=(m // tm,),
        in_specs=[pl.BlockSpec((tm, Q_DIM), row), pl.BlockSpec((tm, d), row),
                  pl.BlockSpec((Q_DIM, d), const), ms(gt), pl.BlockSpec((1, d), const),
                  ms(sc), ms(sh), pl.BlockSpec(wr.shape, const), pl.BlockSpec(br.shape, const)],
        out_specs=[pl.BlockSpec((tm, d), row), pl.BlockSpec((tm, d), row),
                   pl.BlockSpec((tm, LANES), row)],
        out_shape=[jax.ShapeDtypeStruct((m, d), F32), jax.ShapeDtypeStruct((m, d), BF16),
                   jax.ShapeDtypeStruct((m, LANES), F32)],
        compiler_params=_params("parallel"),
        name="out_proj_router",
    )(o, x, w_out_b, gt, g2, sc, sh, wr, br)


def _moe_kernel(h_ref, gates_ref, x_ref, gt_ref, wg_ref, wu_ref, wd_ref, *rest, epc, final):
    fg_ref, o_ref, acc_ref = rest if final else (None, *rest)
    c = pl.program_id(1)

    @pl.when(c == 0)
    def _():
        acc_ref[...] = jnp.zeros(acc_ref.shape, F32)

    h = h_ref[...]
    gates = gates_ref[...]
    acts = []
    for j in range(epc):
        gcol = _lane_pick(gates, c * epc + j)
        acts.append((_silu(_dot(h, wg_ref[j])) * _dot(h, wu_ref[j]) * gcol).astype(BF16))
    wd = wd_ref[...]
    acc_ref[...] += _dot(jnp.concatenate(acts, axis=1), wd.reshape(wd.shape[0] * wd.shape[1], wd.shape[2]))

    @pl.when(c == pl.num_programs(1) - 1)
    def _():
        x = x_ref[...] + gt_ref[...] * acc_ref[...]
        if fg_ref is not None:
            x = x * lax.rsqrt(jnp.mean(x * x, axis=-1, keepdims=True) + RMS_EPS) * fg_ref[...]
        o_ref[...] = x


def moe_ffn(h2, gates, x1, gt, wg_b, wu_b, wd_b, final_g, *, tm, rows_per_seq):
    m, d = x1.shape
    n_e, _, de = wg_b.shape
    epc = 4
    row = lambda i, c: (i, 0)
    gt_spec = _mod_spec(gt, tm, rows_per_seq)
    gt_spec = pl.BlockSpec(gt_spec.block_shape, lambda i, c, f=gt_spec.index_map: f(i))
    in_specs = [pl.BlockSpec((tm, d), row), pl.BlockSpec((tm, LANES), row),
                pl.BlockSpec((tm, d), row), gt_spec,
                pl.BlockSpec((epc, d, de), lambda i, c: (c, 0, 0)),
                pl.BlockSpec((epc, d, de), lambda i, c: (c, 0, 0)),
                pl.BlockSpec((epc, de, d), lambda i, c: (c, 0, 0))]
    args = [h2, gates, x1, gt, wg_b, wu_b, wd_b]
    if final_g is not None:
        in_specs.append(pl.BlockSpec((1, d), lambda i, c: (0, 0)))
        args.append(final_g)
    return pl.pallas_call(
        functools.partial(_moe_kernel, epc=epc, final=final_g is not None),
        grid=(m // tm, n_e // epc),
        in_specs=in_specs,
        out_specs=pl.BlockSpec((tm, d), row),
        out_shape=jax.ShapeDtypeStruct((m, d), F32),
        scratch_shapes=[pltpu.VMEM((tm, d), F32)],
        compiler_params=_params("parallel", "arbitrary"),
        name="moe_ffn",
    )(*args)


def _moe_sorted_kernel(cnt_ref, h_ref, route_ref, x_ref, gt_ref, wg_ref, wu_ref, wd_ref, *rest, final):
    fg_ref, o_ref, acc_ref, posc_ref, posr_ref = rest if final else (None, *rest)
    i = pl.program_id(0)
    g = pl.program_id(1)
    tm = h_ref.shape[0]
    route = route_ref[...]

    @pl.when(g == 0)
    def _():
        acc_ref[...] = jnp.zeros(acc_ref.shape, F32)
        lane = lax.broadcasted_iota(I32, (tm, LANES), 1)
        gid = route[:, GROUP_LANE:GROUP_LANE + 1]
        onehot = jnp.where(lane.astype(F32) == gid, 1.0, 0.0)
        r = lax.broadcasted_iota(I32, (tm, tm), 0)
        c = lax.broadcasted_iota(I32, (tm, tm), 1)
        earlier = jnp.where(c < r, 1.0, 0.0).astype(BF16)
        rank = jnp.sum(onehot * _dot(earlier, onehot.astype(BF16)), axis=1, keepdims=True)
        posc = jnp.where(lane == 0, gid, jnp.where(lane == 1, rank, 0.0))
        posc_ref[...] = posc
        posr_ref[...] = posc.T

    n_rows = cnt_ref[i * N_GROUPS + g]
    gf = g.astype(F32)
    pos_col = jnp.where(posc_ref[:, 0:1] == gf, posc_ref[:, 1:2], -1.0)
    pos_row = jnp.where(posr_ref[0:1, :] == gf, posr_ref[1:2, :], -1.0)
    r_hi, r_lo = _split2(route)
    wd = wd_ref[...]
    wd = wd.reshape(wd.shape[0] * wd.shape[1], wd.shape[2])
    sizes = [SORTED_ROWS] + [SORTED_ROWS // 2] * 2 + [SORTED_ROWS] * (tm // SORTED_ROWS - 2)
    sizes = sizes if tm >= 2 * SORTED_ROWS else [tm]
    for start, rows in zip([sum(sizes[:k]) for k in range(len(sizes))], sizes):
        @pl.when(n_rows > start)
        def _(base=float(start), rows=rows):
            rid = lax.broadcasted_iota(I32, (rows, tm), 0).astype(F32) + base
            pack = jnp.where(pos_row == rid, 1.0, 0.0).astype(BF16)
            xs = _dot(pack, h_ref[...]).astype(BF16)
            gs = _dot(pack, r_hi) + _dot(pack, r_lo)
            acts = []
            for j in range(EXPERTS_PER_GROUP):
                gcol = _lane_pick(gs, g * EXPERTS_PER_GROUP + j)
                acts.append((_silu(_dot(xs, wg_ref[j])) * _dot(xs, wu_ref[j]) * gcol).astype(BF16))
            ys = _dot(jnp.concatenate(acts, axis=1), wd)
            cid = lax.broadcasted_iota(I32, (tm, rows), 1).astype(F32) + base
            unpack = jnp.where(pos_col == cid, 1.0, 0.0).astype(BF16)
            acc_ref[...] += _dot(unpack, ys.astype(BF16))

    @pl.when(g == pl.num_programs(1) - 1)
    def _():
        x = x_ref[...] + gt_ref[...] * acc_ref[...]
        if fg_ref is not None:
            x = x * lax.rsqrt(jnp.mean(x * x, axis=-1, keepdims=True) + RMS_EPS) * fg_ref[...]
        o_ref[...] = x


def moe_ffn_sorted(h2, route, x1, gt, wg_b, wu_b, wd_b, final_g, *, tm, rows_per_seq):
    m, d = x1.shape
    _, _, de = wg_b.shape
    npos = EXPERTS_PER_GROUP
    n_tiles = m // tm
    group = route[:, GROUP_LANE].reshape(n_tiles, 1, tm)
    counts = jnp.sum(group == jnp.arange(N_GROUPS, dtype=F32)[None, :, None], axis=2, dtype=I32)
    row = lambda i, g, cnt: (i, 0)
    gt_spec = _mod_spec(gt, tm, rows_per_seq)
    gt_spec = pl.BlockSpec(gt_spec.block_shape, lambda i, g, cnt, f=gt_spec.index_map: f(i))
    in_specs = [pl.BlockSpec((tm, d), row), pl.BlockSpec((tm, LANES), row),
                pl.BlockSpec((tm, d), row), gt_spec,
                pl.BlockSpec((npos, d, de), lambda i, g, cnt: (g, 0, 0)),
                pl.BlockSpec((npos, d, de), lambda i, g, cnt: (g, 0, 0)),
                pl.BlockSpec((npos, de, d), lambda i, g, cnt: (g, 0, 0))]
    args = [h2, route, x1, gt, wg_b, wu_b, wd_b]
    if final_g is not None:
        in_specs.append(pl.BlockSpec((1, d), lambda i, g, cnt: (0, 0)))
        args.append(final_g)
    grid_spec = pltpu.PrefetchScalarGridSpec(
        num_scalar_prefetch=1,
        grid=(n_tiles, N_GROUPS),
        in_specs=in_specs,
        out_specs=pl.BlockSpec((tm, d), row),
        scratch_shapes=[pltpu.VMEM((tm, d), F32), pltpu.VMEM((tm, LANES), F32), pltpu.VMEM((LANES, tm), F32)],
    )
    return pl.pallas_call(
        functools.partial(_moe_sorted_kernel, final=final_g is not None),
        grid_spec=grid_spec,
        out_shape=jax.ShapeDtypeStruct((m, d), F32),
        compiler_params=_params("parallel", "arbitrary"),
        name="moe_ffn_sorted",
    )(counts.reshape(-1), *args)


def _head_rows(mats):
    rowi = lax.broadcasted_iota(I32, mats[0].shape, 0)
    out = mats[0]
    for kvh in range(1, N_KV_HEADS):
        out = jnp.where((rowi >> 1) == kvh, mats[kvh], out)
    return out


def _kv_rows(page_ref):
    return [page_ref[pl.ds(kvh, PAGE_SIZE, stride=N_KV_HEADS), :] for kvh in range(N_KV_HEADS)]


def _new_token_logits(q, kn):
    qf = q.astype(F32)
    prods = [qf * kn[kvh:kvh + 1, :].astype(BF16).astype(F32) for kvh in range(N_KV_HEADS)]
    return jnp.sum(_head_rows(prods), axis=1, keepdims=True)


def _new_token_values(vn):
    return _head_rows([jnp.broadcast_to(vn[kvh:kvh + 1, :].astype(BF16).astype(F32), (HEAD_ROWS, HEAD_DIM))
                       for kvh in range(N_KV_HEADS)])


def _page_spec(layer, n_pages, page_of):
    return pl.BlockSpec((None, None, PAGE_ROWS, HEAD_DIM),
                        lambda b, c, pt: (layer, pt[b * n_pages + page_of(c)], 0, 0))


def _seq_spec(rows, width):
    return pl.BlockSpec((None, rows, width), lambda b, c, pt: (b, 0, 0))


def _suffix_total_matrix(n):
    upper, ones = _suffix_matrices(n)
    return jnp.concatenate([upper, ones], axis=1)


def _suffix_and_total(xs, suffix_total):
    sums = []
    for x in xs:
        hi, lo = _split2(x)
        sums.append(_dot(hi, suffix_total) + _dot(lo, suffix_total))
    return [s[:, :PAGE_SIZE] for s in sums], [s[:, PAGE_SIZE:] for s in sums]


def _page_logits(q, k_refs):
    return [_head_rows([_dot_nt(q, ks.astype(BF16)) for ks in _kv_rows(ref)]) for ref in k_refs]


def _page_values(w_pages, v_refs):
    pv = [jnp.zeros((HEAD_ROWS, HEAD_DIM), F32)] * N_KV_HEADS
    for w, ref in zip(w_pages, v_refs):
        wb = w.astype(BF16)
        pv = [a + _dot(wb, vs.astype(BF16)) for a, vs in zip(pv, _kv_rows(ref))]
    return _head_rows(pv)


def _carry_back(car, sufs, tots):
    after = [None] * len(sufs)
    for r in reversed(range(len(sufs))):
        after[r] = car + sufs[r]
        car = car + tots[r]
    return after, car


def _stream_pages(pt_ref, sources, bufs, sem, *, pages, n_pages, nc):
    step = pl.program_id(0) * nc + pl.program_id(1)
    total = pl.num_programs(0) * nc

    def copies(s, slot):
        first = (s // nc) * n_pages + (nc - 1 - s % nc) * pages
        return [pltpu.make_async_copy(src.at[layer, pt_ref[first + r]], buf.at[slot, r], sem.at[slot, i, r])
                for r in range(pages) for i, ((src, layer), buf) in enumerate(zip(sources, bufs))]

    slot = step % 2

    @pl.when(step == 0)
    def _():
        for cp in copies(step, slot):
            cp.start()

    @pl.when(step + 1 < total)
    def _():
        for cp in copies(step + 1, 1 - slot):
            cp.start()

    for cp in copies(step, slot):
        cp.wait()
    return slot


def _stream_scratch(pages, shapes):
    return ([pltpu.VMEM((2, pages) + shape, F32) for shape in shapes]
            + [pltpu.SemaphoreType.DMA((2, len(shapes), pages))])


def _fox_dec_kernel(pt_ref, q_ref, kn_ref, vn_ref, lfn_ref, ck_ref, cv_ref, clf_ref, o_ref,
                    m_s, l_s, acc_s, car_s, kbuf, vbuf, lfbuf, sem, *, pages, n_pages, nc, layer, fox_layer):
    slot = _stream_pages(pt_ref, [(ck_ref, layer), (cv_ref, layer), (clf_ref, fox_layer)],
                         [kbuf, vbuf, lfbuf], sem, pages=pages, n_pages=n_pages, nc=nc)
    k_refs = [kbuf.at[slot, r] for r in range(pages)]
    v_refs = [vbuf.at[slot, r] for r in range(pages)]
    lf_refs = [lfbuf.at[slot, r] for r in range(pages)]
    c = pl.program_id(1)
    q = q_ref[...]
    suffix_total = _suffix_total_matrix(PAGE_SIZE)

    @pl.when(c == 0)
    def _():
        m_s[...] = jnp.broadcast_to(_new_token_logits(q, kn_ref[...]), m_s.shape)
        l_s[...] = jnp.ones(l_s.shape, F32)
        acc_s[...] = _new_token_values(vn_ref[...])
        car_s[...] = lfn_ref[...]

    pad = jnp.zeros((HEAD_ROWS - N_HEADS, PAGE_SIZE), F32)
    sufs, tots = _suffix_and_total([jnp.concatenate([ref[...], pad], axis=0) for ref in lf_refs], suffix_total)
    qk = _page_logits(q, k_refs)
    decay, car_s[...] = _carry_back(car_s[...], sufs, tots)
    s_pages = [a + b for a, b in zip(qk, decay)]

    m_old = m_s[...]
    mx = functools.reduce(jnp.maximum, s_pages)
    m_new = jnp.maximum(m_old, jnp.max(mx, axis=1, keepdims=True))
    alpha = jnp.exp(m_old - m_new)
    p_pages = [jnp.exp(s - m_new) for s in s_pages]
    l_s[...] = l_s[...] * alpha + jnp.sum(functools.reduce(jnp.add, p_pages), axis=1, keepdims=True)
    acc_s[...] = acc_s[...] * alpha + _page_values(p_pages, v_refs)
    m_s[...] = m_new

    @pl.when(c == pl.num_programs(1) - 1)
    def _():
        o_ref[...] = acc_s[...] / l_s[...]


def fox_attention_decode(q16, kn, vn, lfn, cache_k, cache_v, cache_lft, page_table, layer, fox_layer):
    b = q16.shape[0]
    n_pages = page_table.shape[1]
    pages = min(16, n_pages)
    nc = n_pages // pages
    hbm = pl.BlockSpec(memory_space=pl.ANY)
    grid_spec = pltpu.PrefetchScalarGridSpec(
        num_scalar_prefetch=1,
        grid=(b, nc),
        in_specs=[_seq_spec(HEAD_ROWS, HEAD_DIM), _seq_spec(N_KV_HEADS, HEAD_DIM),
                  _seq_spec(N_KV_HEADS, HEAD_DIM), _seq_spec(HEAD_ROWS, PAGE_SIZE), hbm, hbm, hbm],
        out_specs=_seq_spec(HEAD_ROWS, HEAD_DIM),
        scratch_shapes=[pltpu.VMEM((HEAD_ROWS, LANES), F32)] * 4
        + _stream_scratch(pages, [(PAGE_ROWS, HEAD_DIM), (PAGE_ROWS, HEAD_DIM), (N_HEADS, PAGE_SIZE)]),
    )
    return pl.pallas_call(
        functools.partial(_fox_dec_kernel, pages=pages, n_pages=n_pages, nc=nc, layer=layer,
                          fox_layer=fox_layer),
        grid_spec=grid_spec,
        out_shape=jax.ShapeDtypeStruct((b, HEAD_ROWS, HEAD_DIM), F32),
        compiler_params=_params("arbitrary", "arbitrary"),
        name="fox_attention_decode",
    )(page_table.reshape(-1), q16, kn, vn, lfn, cache_k, cache_v, cache_lft)


def _sb_dec_kernel(pt_ref, q_ref, ck_ref, cv_ref, o_ref, acc_s, car_s, kbuf, vbuf, sem,
                   *, pages, n_pages, nc, layer):
    slot = _stream_pages(pt_ref, [(ck_ref, layer), (cv_ref, layer)], [kbuf, vbuf], sem,
                         pages=pages, n_pages=n_pages, nc=nc)
    k_refs = [kbuf.at[slot, r] for r in range(pages)]
    v_refs = [vbuf.at[slot, r] for r in range(pages)]
    c = pl.program_id(1)
    q = q_ref[...]
    suffix_total = _suffix_total_matrix(PAGE_SIZE)

    @pl.when(c == 0)
    def _():
        acc_s[...] = jnp.zeros(acc_s.shape, F32)
        car_s[...] = jnp.zeros(car_s.shape, F32)

    zs = _page_logits(q, k_refs)
    lks = [_neg_softplus(z) for z in zs]
    sufs, tots = _suffix_and_total(lks, suffix_total)
    after, car_s[...] = _carry_back(car_s[...], sufs, tots)
    a_pages = [jnp.exp(z + lk + af) for z, lk, af in zip(zs, lks, after)]
    acc_s[...] += _page_values(a_pages, v_refs)

    @pl.when(c == pl.num_programs(1) - 1)
    def _():
        o_ref[...] = acc_s[...]


def sb_attention_decode(q16, cache_k, cache_v, page_table, layer):
    b = q16.shape[0]
    n_pages = page_table.shape[1]
    pages = min(16, n_pages)
    nc = n_pages // pages
    hbm = pl.BlockSpec(memory_space=pl.ANY)
    grid_spec = pltpu.PrefetchScalarGridSpec(
        num_scalar_prefetch=1,
        grid=(b, nc),
        in_specs=[_seq_spec(HEAD_ROWS, HEAD_DIM), hbm, hbm],
        out_specs=_seq_spec(HEAD_ROWS, HEAD_DIM),
        scratch_shapes=[pltpu.VMEM((HEAD_ROWS, LANES), F32)] * 2
        + _stream_scratch(pages, [(PAGE_ROWS, HEAD_DIM), (PAGE_ROWS, HEAD_DIM)]),
    )
    return pl.pallas_call(
        functools.partial(_sb_dec_kernel, pages=pages, n_pages=n_pages, nc=nc, layer=layer),
        grid_spec=grid_spec,
        out_shape=jax.ShapeDtypeStruct((b, HEAD_ROWS, HEAD_DIM), F32),
        compiler_params=_params("arbitrary", "arbitrary"),
        name="sb_attention_decode",
    )(page_table.reshape(-1), q16, cache_k, cache_v)


def _page_head_sums(page_ref):
    x = page_ref[...]
    s8 = jnp.sum(x.reshape(PAGE_ROWS // 8, 8, HEAD_DIM), axis=0)
    return s8[:N_KV_HEADS] + s8[N_KV_HEADS:]


def _moba_sel_kernel(pt_ref, q_ref, *refs, pages, nb):
    k_refs = refs[:pages]
    km_ref, sel_ref = refs[pages:]
    c = pl.program_id(1)
    bps = pages // 2
    sums = [_page_head_sums(k_refs[r]) for r in range(pages)]
    means = [(sums[2 * j] + sums[2 * j + 1]) * (1.0 / MOBA_BLOCK) for j in range(bps)]
    row0 = pl.multiple_of(c * bps, bps)
    for kvh in range(N_KV_HEADS):
        km_ref[kvh, pl.ds(row0, bps), :] = jnp.concatenate([mj[kvh:kvh + 1] for mj in means], axis=0)

    @pl.when(c == pl.num_programs(1) - 1)
    def _():
        q = q_ref[...]
        lane = lax.broadcasted_iota(I32, (nb, LANES), 1)
        gate = jnp.zeros((nb, LANES), F32)
        for kvh in range(N_KV_HEADS):
            hi, lo = _split2(km_ref[kvh])
            gate = jnp.where((lane >> 1) == kvh, _dot_nt(hi, q) + _dot_nt(lo, q), gate)
        sel = _top_mask(gate, nb, nb, 0)
        blk = lax.broadcasted_iota(I32, (nb, LANES), 0).astype(F32)
        rows = []
        for _ in range(MOBA_TOPK):
            first = jnp.min(jnp.where(sel, blk, float(nb)), axis=0, keepdims=True)
            rows.append(first.astype(I32))
            sel = sel & (blk != first)
        rows.append(jnp.zeros((8 - MOBA_TOPK, LANES), I32))
        sel_ref[...] = jnp.concatenate(rows, axis=0)


def moba_select_decode(q128, cache_k, page_table, layer):
    b = q128.shape[0]
    n_pages = page_table.shape[1]
    nb = n_pages * PAGE_SIZE // MOBA_BLOCK
    pages = min(16, n_pages)
    nc = n_pages // pages
    k_specs = [_page_spec(layer, n_pages, lambda c, r=r: c * pages + r) for r in range(pages)]
    grid_spec = pltpu.PrefetchScalarGridSpec(
        num_scalar_prefetch=1,
        grid=(b, nc),
        in_specs=[_seq_spec(LANES, HEAD_DIM)] + k_specs,
        out_specs=[pl.BlockSpec((None, N_KV_HEADS, nb, HEAD_DIM), lambda b, c, pt: (b, 0, 0, 0)),
                   _seq_spec(8, LANES)],
    )
    _, sel = pl.pallas_call(
        functools.partial(_moba_sel_kernel, pages=pages, nb=nb),
        grid_spec=grid_spec,
        out_shape=[jax.ShapeDtypeStruct((b, N_KV_HEADS, nb, HEAD_DIM), F32),
                   jax.ShapeDtypeStruct((b, 8, LANES), I32)],
        compiler_params=_params("parallel", "arbitrary"),
        name="moba_select_decode",
    )(page_table.reshape(-1), q128, *([cache_k] * pages))
    return sel


def _moba_dec_kernel(pt_ref, sel_ref, q_ref, kn_ref, vn_ref, bt_ref, *refs, nb):
    n = MOBA_TOPK * 2
    k_refs = refs[:n]
    v_refs = refs[n:2 * n]
    o_ref = refs[2 * n]
    b = pl.program_id(0)
    h = pl.program_id(1)
    head_rows = pl.ds(h // GROUP, PAGE_SIZE, stride=N_KV_HEADS)
    q = q_ref[...]
    bt = bt_ref[...]
    far = bt[:, 2 * PAGE_SIZE:3 * PAGE_SIZE]
    s_new = _new_token_logits(q, kn_ref[...]) + bt[:, 3 * PAGE_SIZE:3 * PAGE_SIZE + 1]

    s_piece = []
    for slot in range(MOBA_TOPK):
        last = sel_ref[(b * MOBA_TOPK + slot) * N_HEADS + h] == nb - 1
        for r in range(2):
            kp = k_refs[slot * 2 + r][head_rows, :].astype(BF16)
            bias = jnp.where(last, bt[:, r * PAGE_SIZE:(r + 1) * PAGE_SIZE], far)
            s_piece.append(_dot_nt(q, kp) + bias)
    mx = functools.reduce(jnp.maximum, s_piece)
    m = jnp.maximum(jnp.max(mx, axis=1, keepdims=True), s_new)
    p_piece = [jnp.exp(s - m) for s in s_piece]
    p_new = jnp.exp(s_new - m)
    l = jnp.sum(functools.reduce(jnp.add, p_piece), axis=1, keepdims=True) + p_new
    acc = p_new.astype(BF16).astype(F32) * _new_token_values(vn_ref[...])
    for i in range(n):
        acc = acc + _dot(p_piece[i].astype(BF16), v_refs[i][head_rows, :].astype(BF16))
    o = acc / l

    @pl.when(h == 0)
    def _():
        o_ref[...] = jnp.zeros(o_ref.shape, F32)

    rowi = lax.broadcasted_iota(I32, o.shape, 0)
    o_ref[pl.ds(h, 1), :] = jnp.sum(jnp.where(rowi == h, o, 0.0), axis=0, keepdims=True)


def moba_attention_decode(q16, kn, vn, bias_tab, sel, cache_k, cache_v, page_table, layer):
    b = q16.shape[0]
    n_pages = page_table.shape[1]
    nb = n_pages * PAGE_SIZE // MOBA_BLOCK

    def page_spec(slot, r):
        def index(b, h, pt, sel):
            blk = sel[(b * MOBA_TOPK + slot) * N_HEADS + h]
            return (layer, pt[b * n_pages + 2 * blk + r], 0, 0)
        return pl.BlockSpec((None, None, PAGE_ROWS, HEAD_DIM), index)

    kv_specs = [page_spec(slot, r) for slot in range(MOBA_TOPK) for r in range(2)]
    seq = lambda rows, width: pl.BlockSpec((None, rows, width), lambda b, h, pt, sel: (b, 0, 0))
    grid_spec = pltpu.PrefetchScalarGridSpec(
        num_scalar_prefetch=2,
        grid=(b, N_HEADS),
        in_specs=[seq(HEAD_ROWS, HEAD_DIM), seq(N_KV_HEADS, HEAD_DIM), seq(N_KV_HEADS, HEAD_DIM),
                  pl.BlockSpec((HEAD_ROWS, 4 * PAGE_SIZE), lambda b, h, pt, sel: (0, 0))] + kv_specs + kv_specs,
        out_specs=seq(HEAD_ROWS, HEAD_DIM),
    )
    n = len(kv_specs)
    return pl.pallas_call(
        functools.partial(_moba_dec_kernel, nb=nb),
        grid_spec=grid_spec,
        out_shape=jax.ShapeDtypeStruct((b, HEAD_ROWS, HEAD_DIM), F32),
        compiler_params=_params("parallel", "arbitrary"),
        name="moba_attention_decode",
    )(page_table.reshape(-1), sel.reshape(-1), q16, kn, vn, bias_tab, *([cache_k] * n), *([cache_v] * n))


def _rel_bias_by_distance(rel_bias, n):
    dist = jnp.arange(n, dtype=I32)
    max_exact = N_BUCKETS // 2
    large = max_exact + (jnp.log(jnp.maximum(dist, 1).astype(F32) / max_exact)
                         / math.log(MAX_DISTANCE / max_exact)
                         * (N_BUCKETS - max_exact)).astype(I32)
    large = jnp.minimum(large, N_BUCKETS - 1)
    return rel_bias[jnp.where(dist < max_exact, dist, large)]


def _pad_lanes(w, b):
    n = w.shape[1]
    return jnp.pad(w, ((0, 0), (0, LANES - n))), jnp.pad(b, (0, LANES - n)).reshape(1, LANES)


def _trunk(x, mod, cache, rel_bias, weights):
    (norm_g, final_g, w_in_b, w_out_b, fgates, w_router, b_router, wg_b, wu_b, wd_b) = weights
    b, t, d = x.shape
    m = b * t
    depth = w_in_b.shape[0]
    paged = cache is not None
    if paged:
        assert t == 1
        cache_k, cache_v, cache_lft, page_table = cache
        n_pages = page_table.shape[1]
        nb_past = n_pages * PAGE_SIZE // MOBA_BLOCK
        assert (n_pages * PAGE_SIZE) % MOBA_BLOCK == 0 and nb_past >= MOBA_TOPK
        tm_tok = tm_moe = m
        bias_d = _rel_bias_by_distance(rel_bias, 2 * MOBA_BLOCK + 1)
        assert MAX_DISTANCE <= MOBA_BLOCK
        near = bias_d[MOBA_BLOCK - jnp.arange(MOBA_BLOCK)].T
        tab = jnp.concatenate([near, jnp.broadcast_to(bias_d[2 * MOBA_BLOCK][:, None], (N_HEADS, PAGE_SIZE)),
                               jnp.broadcast_to(bias_d[0][:, None], (N_HEADS, PAGE_SIZE))], axis=1)
        bias_tab = jnp.pad(tab, ((0, HEAD_ROWS - N_HEADS), (0, 0)))
    else:
        assert t % MOBA_BLOCK == 0 and t // MOBA_BLOCK <= LANES
        tm_tok = min(512, t)
        tm_moe = min(1024, t)
        assert MAX_DISTANCE <= MOBA_BLOCK
        bias0, bias1 = rel_bias_tiles(rel_bias)

    q_scale = HEAD_DIM ** -0.5 * (1.0 if paged else LOG2E)
    wr, br = router_operands(w_router, b_router, transposed=tm_tok >= LANES)

    def mod_part(l, j):
        part = mod[l, :, j * d:(j + 1) * d]
        return part if paged else part.reshape(b, 1, d)

    xf = x.reshape(m, d)
    new_k, new_v, new_logf = [], [], []
    for l in range(depth):
        kind = l % N_MIXERS
        sh1, sc1, gt1, sh2, sc2, gt2 = [mod_part(l, j) for j in range(6)]
        fg = fgates[l // N_MIXERS] if kind == 0 else None
        outs = norm_qkv(xf, norm_g[l, 0].reshape(1, d), sc1, sh1, w_in_b[l], fg,
                        tm=tm_tok, rows_per_seq=t, q_scale=q_scale)
        q, k, v, kb, vb = outs[:5]
        new_k.append(k.reshape(b, t, N_KV_HEADS, HEAD_DIM))
        new_v.append(v.reshape(b, t, N_KV_HEADS, HEAD_DIM))
        if kind == 0:
            lf = outs[5]
            new_logf.append(lf[:, :N_HEADS].reshape(b, t, N_HEADS))
        if paged:
            q16 = jnp.pad(q.reshape(b, N_HEADS, HEAD_DIM), ((0, 0), (0, HEAD_ROWS - N_HEADS), (0, 0)))
            kn = k.reshape(b, N_KV_HEADS, HEAD_DIM)
            vn = v.reshape(b, N_KV_HEADS, HEAD_DIM)
            if kind == 0:
                lfn = jnp.broadcast_to(
                    jnp.pad(lf[:, :N_HEADS], ((0, 0), (0, HEAD_ROWS - N_HEADS)))[:, :, None],
                    (b, HEAD_ROWS, PAGE_SIZE))
                o16 = fox_attention_decode(q16, kn, vn, lfn, cache_k, cache_v, cache_lft, page_table,
                                           l, l // N_MIXERS)
            elif kind == 1:
                q128 = jnp.pad(q.reshape(b, N_HEADS, HEAD_DIM), ((0, 0), (0, LANES - N_HEADS), (0, 0)))
                sel = moba_select_decode(q128, cache_k, page_table, l)
                sel = sel[:, :MOBA_TOPK, :N_HEADS]
                o16 = moba_attention_decode(q16, kn, vn, bias_tab, sel, cache_k, cache_v, page_table, l)
            else:
                o16 = sb_attention_decode(q16, cache_k, cache_v, page_table, l)
            o = o16[:, :N_HEADS, :].reshape(m, Q_DIM).astype(BF16)
        else:
            q3 = q.reshape(b, t, Q_DIM)
            kb3 = kb.reshape(b, t, KV_DIM)
            vb3 = vb.reshape(b, t, KV_DIM)
            if kind == 0:
                o = fox_attention_prompt(q3, kb3, vb3, cumsum_time(lf.reshape(b, t, LANES)))
            elif kind == 1:
                kmean = block_mean(k.reshape(b, t * N_KV_HEADS, HEAD_DIM))
                o = moba_attention_prompt(q3, kb3, vb3, kmean, bias0, bias1)
            else:
                o = sb_attention_prompt(q3, kb3, vb3)
            o = o.reshape(m, Q_DIM)
        x1, h2, gates = out_proj_router(o, xf, w_out_b[l], gt1, norm_g[l, 1].reshape(1, d), sc2, sh2,
                                        wr, br, tm=tm_tok, rows_per_seq=t)
        fin = final_g.reshape(1, d) if l == depth - 1 else None
        moe = moe_ffn if paged else moe_ffn_sorted
        xf = moe(h2, gates, x1, gt2, wg_b[l], wu_b[l], wd_b[l], fin, tm=tm_moe, rows_per_seq=t)
    return xf.reshape(b, t, d), jnp.stack(new_k), jnp.stack(new_v), jnp.stack(new_logf)


def kernel(x_prompt, x_sample, cache_k, cache_v, cache_logf, page_table, c_prompt, c_sample, rel_bias,
           w_ada, b_ada, norm_g, final_g, w_in, w_out, w_fgate, b_fgate, w_router, b_router,
           w_gate, w_up, w_down):
    n_prompt = c_prompt.shape[0]
    mod = ada_modulation(jnp.concatenate([c_prompt, c_sample], axis=0), w_ada, b_ada)

    fgates = []
    for a in range(w_fgate.shape[0]):
        wf, bf = _pad_lanes(w_fgate[a], b_fgate[a])
        fgates.append((wf.astype(BF16), bf))
    weights = (norm_g, final_g, w_in.astype(BF16), w_out.astype(BF16), fgates, w_router, b_router,
               w_gate.astype(BF16), w_up.astype(BF16), w_down.astype(BF16))

    depth, n_pool = cache_k.shape[:2]
    cache = (cache_k.reshape(depth, n_pool, PAGE_ROWS, HEAD_DIM),
             cache_v.reshape(depth, n_pool, PAGE_ROWS, HEAD_DIM),
             jnp.swapaxes(cache_logf, 2, 3), page_table)

    y_p, k_p, v_p, lf_p = _trunk(x_prompt, mod[:, :n_prompt], None, rel_bias, weights)
    y_s, k_s, v_s, lf_s = _trunk(x_sample, mod[:, n_prompt:], cache, rel_bias, weights)
    return (y_p, y_s, k_p, v_p, lf_p, k_s, v_s, lf_s)
```

```python
import functools
import math

import jax
import jax.numpy as jnp
from jax import lax
from jax.experimental import pallas as pl
from jax.experimental.pallas import tpu as pltpu

F32 = jnp.float32
BF16 = jnp.bfloat16
I32 = jnp.int32

N_MIXERS = 3
N_HEADS = 8
N_KV_HEADS = 4
GROUP = N_HEADS // N_KV_HEADS
HEAD_DIM = 128
Q_DIM = N_HEADS * HEAD_DIM
KV_DIM = N_KV_HEADS * HEAD_DIM
PAGE_SIZE = 128
MOBA_BLOCK = 256
MOBA_TOPK = 3
N_BUCKETS = 32
MAX_DISTANCE = 128
N_EXPERTS = 16
N_GROUPS = 4
EXPERTS_PER_GROUP = N_EXPERTS // N_GROUPS
TOP_K = 2
GROUP_LANE = N_EXPERTS
SORTED_ROWS = 256
RMS_EPS = 1e-6

PAGE_ROWS = PAGE_SIZE * N_KV_HEADS
LANES = 128
HEAD_ROWS = 16
MASKED = -1e30
LOG2E = math.log2(math.e)
VMEM_LIMIT = 56 * 1024 * 1024


def _params(*sem):
    return pltpu.CompilerParams(dimension_semantics=sem, vmem_limit_bytes=VMEM_LIMIT)


def _dot(a, b):
    return jnp.dot(a, b, preferred_element_type=F32)


def _dot_nt(a, b):
    return lax.dot_general(a, b, (((1,), (1,)), ((), ())), preferred_element_type=F32)


def _split2(x):
    hi = x.astype(BF16)
    lo = (x - hi.astype(F32)).astype(BF16)
    return hi, lo


def _split3(x):
    hi = x.astype(BF16)
    r = x - hi.astype(F32)
    mid = r.astype(BF16)
    lo = (r - mid.astype(F32)).astype(BF16)
    return hi, mid, lo


def _neg_softplus(z):
    return -(jnp.maximum(z, 0.0) + jnp.log1p(jnp.exp(-jnp.abs(z))))


def _silu(x):
    return x / (1.0 + jnp.exp(-x))


def _ada_kernel(c_ref, w_ref, b_ref, o_ref):
    s = _silu(c_ref[...])
    o_ref[...] = _dot(s.astype(BF16), w_ref[...].astype(BF16)) + b_ref[...]


def ada_modulation(c_all, w_ada, b_ada):
    depth, d, n = w_ada.shape
    mc = c_all.shape[0]
    tn = 1024
    return pl.pallas_call(
        _ada_kernel,
        grid=(depth, n // tn),
        in_specs=[
            pl.BlockSpec((mc, d), lambda l, j: (0, 0)),
            pl.BlockSpec((None, d, tn), lambda l, j: (l, 0, j)),
            pl.BlockSpec((None, 1, tn), lambda l, j: (l, 0, j)),
        ],
        out_specs=pl.BlockSpec((None, mc, tn), lambda l, j: (l, 0, j)),
        out_shape=jax.ShapeDtypeStruct((depth, mc, n), F32),
        compiler_params=_params("parallel", "parallel"),
        name="ada_modulation",
    )(c_all, w_ada, b_ada.reshape(depth, 1, n))


def _rms_mod(x, g, sc, sh):
    r = lax.rsqrt(jnp.mean(x * x, axis=-1, keepdims=True) + RMS_EPS)
    return (x * r * g) * (1.0 + sc) + sh


def _qkv_kernel(*refs, has_fgate, q_scale):
    if has_fgate:
        (x_ref, g_ref, sc_ref, sh_ref, w_ref, wf_ref, bf_ref,
         q_ref, k_ref, v_ref, kb_ref, vb_ref, lf_ref) = refs
    else:
        x_ref, g_ref, sc_ref, sh_ref, w_ref, q_ref, k_ref, v_ref, kb_ref, vb_ref = refs
    hb = _rms_mod(x_ref[...], g_ref[...], sc_ref[...], sh_ref[...]).astype(BF16)
    qkv = _dot(hb, w_ref[...])
    q_ref[...] = (qkv[:, :Q_DIM] * q_scale).astype(BF16)
    k = qkv[:, Q_DIM:Q_DIM + KV_DIM]
    v = qkv[:, Q_DIM + KV_DIM:]
    tm = k.shape[0]
    for kvh in range(N_KV_HEADS):
        rows = pl.ds(kvh, tm, stride=N_KV_HEADS)
        k_ref[rows, :] = k[:, kvh * HEAD_DIM:(kvh + 1) * HEAD_DIM]
        v_ref[rows, :] = v[:, kvh * HEAD_DIM:(kvh + 1) * HEAD_DIM]
    kb_ref[...] = k.astype(BF16)
    vb_ref[...] = v.astype(BF16)
    if has_fgate:
        z = _dot(hb, wf_ref[...]) + bf_ref[...]
        lf_ref[...] = jnp.minimum(z, 0.0) - jnp.log1p(jnp.exp(-jnp.abs(z)))


def _mod_spec(mod, tm, rows_per_seq):
    d = mod.shape[-1]
    if mod.ndim == 3:
        tiles = rows_per_seq // tm
        return pl.BlockSpec((None, 1, d), lambda i: (i // tiles, 0, 0))
    return pl.BlockSpec((tm, d), lambda i: (i, 0))


def norm_qkv(x, g, sc, sh, w_in_b, fgate, *, tm, rows_per_seq, q_scale):
    m, d = x.shape
    n = w_in_b.shape[1]
    has_fgate = fgate is not None
    row = lambda i: (i, 0)
    const = lambda i: (0, 0)
    in_specs = [pl.BlockSpec((tm, d), row), pl.BlockSpec((1, d), const),
                _mod_spec(sc, tm, rows_per_seq), _mod_spec(sh, tm, rows_per_seq),
                pl.BlockSpec((d, n), const)]
    args = [x, g, sc, sh, w_in_b]
    out_specs = [pl.BlockSpec((tm, Q_DIM), row), pl.BlockSpec((tm * N_KV_HEADS, HEAD_DIM), row),
                 pl.BlockSpec((tm * N_KV_HEADS, HEAD_DIM), row), pl.BlockSpec((tm, KV_DIM), row),
                 pl.BlockSpec((tm, KV_DIM), row)]
    out_shape = [jax.ShapeDtypeStruct((m, Q_DIM), BF16),
                 jax.ShapeDtypeStruct((m * N_KV_HEADS, HEAD_DIM), F32),
                 jax.ShapeDtypeStruct((m * N_KV_HEADS, HEAD_DIM), F32),
                 jax.ShapeDtypeStruct((m, KV_DIM), BF16), jax.ShapeDtypeStruct((m, KV_DIM), BF16)]
    if has_fgate:
        wf, bf = fgate
        in_specs += [pl.BlockSpec((d, LANES), const), pl.BlockSpec((1, LANES), const)]
        args += [wf, bf]
        out_specs.append(pl.BlockSpec((tm, LANES), row))
        out_shape.append(jax.ShapeDtypeStruct((m, LANES), F32))
    return pl.pallas_call(
        functools.partial(_qkv_kernel, has_fgate=has_fgate, q_scale=q_scale),
        grid=(m // tm,),
        in_specs=in_specs, out_specs=out_specs, out_shape=out_shape,
        compiler_params=_params("parallel"),
        name="norm_qkv",
    )(*args)


def _cumsum_kernel(lf_ref, dt_ref, *, chunk):
    t = lf_ref.shape[0]
    row = lax.broadcasted_iota(I32, (chunk, chunk), 0)
    col = lax.broadcasted_iota(I32, (chunk, chunk), 1)
    tri = jnp.where(col <= row, 1.0, 0.0).astype(BF16)
    carry = jnp.zeros((1, LANES), F32)
    for c in range(t // chunk):
        sl = slice(c * chunk, (c + 1) * chunk)
        hi, mid, lo = _split3(lf_ref[sl, :])
        cs = _dot(tri, hi) + _dot(tri, mid) + _dot(tri, lo) + carry
        dt_ref[:, sl] = cs.T[:N_HEADS, :]
        carry = cs[chunk - 1:chunk, :]


def cumsum_time(lf):
    b, t, _ = lf.shape
    chunk = min(256, t)
    return pl.pallas_call(
        functools.partial(_cumsum_kernel, chunk=chunk),
        grid=(b,),
        in_specs=[pl.BlockSpec((None, t, LANES), lambda i: (i, 0, 0))],
        out_specs=pl.BlockSpec((None, N_HEADS, t), lambda i: (i, 0, 0)),
        out_shape=jax.ShapeDtypeStruct((b, N_HEADS, t), F32),
        compiler_params=_params("parallel"),
        name="cumsum_time",
    )(lf)


def _lane_pick(x, idx):
    lane = lax.broadcasted_iota(I32, x.shape, 1)
    return jnp.sum(jnp.where(lane == idx, x, 0.0), axis=1, keepdims=True)


def _with_ones(v):
    return jnp.concatenate([v, jnp.ones(v.shape, v.dtype)], axis=1)


def _softmax2_first(s, v1):
    m = jnp.max(s, axis=1, keepdims=True)
    return m, _dot(jnp.exp2(s - m).astype(BF16), v1)


def _softmax2_step(s, v1, m, accl, keep=None):
    m_new = jnp.maximum(m, jnp.max(s, axis=1, keepdims=True))
    new = accl * jnp.exp2(m - m_new) + _dot(jnp.exp2(s - m_new).astype(BF16), v1)
    if keep is None:
        return m_new, new
    return jnp.where(keep, m_new, m), jnp.where(keep, new, accl)


def _softmax2_out(accl):
    return (accl[:, :HEAD_DIM] / accl[:, HEAD_DIM:HEAD_DIM + 1]).astype(BF16)


def _fox_kernel(q_ref, k_ref, v_ref, dt_ref, o_ref, *, tq):
    kvh = pl.program_id(1)
    i = pl.program_id(2)
    q0 = pl.multiple_of(i * tq, tq)
    row = lax.broadcasted_iota(I32, (tq, tq), 0)
    col = lax.broadcasted_iota(I32, (tq, tq), 1)
    causal = col <= row
    heads = [kvh * GROUP + g for g in range(GROUP)]
    qs = [q_ref[:, g * HEAD_DIM:(g + 1) * HEAD_DIM] for g in range(GROUP)]
    d0s = [dt_ref[pl.ds(h, 1), pl.ds(q0, tq)][:, 0:1] for h in heads]

    def logits(g, start, kt):
        dk = dt_ref[pl.ds(heads[g], 1), pl.ds(start, tq)]
        return _dot_nt(qs[g], kt) + (d0s[g] - dk) * LOG2E

    def step(s, vt, m, l, acc):
        m_new = jnp.maximum(m, jnp.max(s, axis=1, keepdims=True))
        alpha = jnp.exp2(m - m_new)
        p = jnp.exp2(s - m_new)
        return m_new, l * alpha + jnp.sum(p, axis=1, keepdims=True), acc * alpha + _dot(p.astype(BF16), vt)

    kt, vt = k_ref[pl.ds(q0, tq), :], v_ref[pl.ds(q0, tq), :]
    zero = (jnp.full((tq, 1), MASKED, F32), jnp.zeros((tq, 1), F32), jnp.zeros((tq, HEAD_DIM), F32))
    state = tuple(step(jnp.where(causal, logits(g, q0, kt), MASKED), vt, *zero) for g in range(GROUP))

    def body(j, state):
        start = pl.multiple_of(j * tq, tq)
        kt, vt = k_ref[pl.ds(start, tq), :], v_ref[pl.ds(start, tq), :]
        return tuple(step(logits(g, start, kt), vt, *state[g]) for g in range(GROUP))

    state = lax.fori_loop(0, i, body, state)
    for g, (_, l, acc) in enumerate(state):
        o_ref[:, g * HEAD_DIM:(g + 1) * HEAD_DIM] = (acc / l).astype(BF16)


def _attn_specs(t, tq):
    qo = pl.BlockSpec((None, tq, GROUP * HEAD_DIM), lambda b, kvh, i: (b, i, kvh))
    kv = pl.BlockSpec((None, t, HEAD_DIM), lambda b, kvh, i: (b, 0, kvh))
    return qo, kv


def fox_attention_prompt(q, kb, vb, dt):
    b, t, _ = q.shape
    tq = min(512, t)
    qo, kv = _attn_specs(t, tq)
    return pl.pallas_call(
        functools.partial(_fox_kernel, tq=tq),
        grid=(b, N_KV_HEADS, t // tq),
        in_specs=[qo, kv, kv, pl.BlockSpec((None, N_HEADS, t), lambda b, kvh, i: (b, 0, 0))],
        out_specs=qo,
        out_shape=jax.ShapeDtypeStruct((b, t, Q_DIM), BF16),
        compiler_params=_params("parallel", "parallel", "parallel"),
        name="fox_attention_prompt",
    )(q, kb, vb, dt)


def _suffix_matrices(n):
    row = lax.broadcasted_iota(I32, (n, n), 0)
    col = lax.broadcasted_iota(I32, (n, n), 1)
    return jnp.where(row > col, 1.0, 0.0).astype(BF16), jnp.ones((n, n), BF16)


def _sb_kernel(q_ref, k_ref, v_ref, o_ref, *, tq, tk):
    i = pl.program_id(2)
    q0 = pl.multiple_of(i * tq, tq)
    upper, _ = _suffix_matrices(tk)
    row = lax.broadcasted_iota(I32, (tq, tk), 0)
    col = lax.broadcasted_iota(I32, (tq, tk), 1)
    qs = [q_ref[:, g * HEAD_DIM:(g + 1) * HEAD_DIM] for g in range(GROUP)]

    def chunk(start, state, mask, r0=0):
        kt, vt = k_ref[pl.ds(start, tk), :], v_ref[pl.ds(start, tk), :]
        out = []
        for g in range(GROUP):
            c, acc = state[g]
            z = _dot_nt(qs[g][r0:], kt)
            ls = jnp.minimum(z, 0.0) - jnp.log2(1.0 + jnp.exp2(jnp.minimum(z, -z)))
            lk = ls - z
            if mask is not None:
                lk = jnp.where(mask[r0:], lk, 0.0)
            hi, lo = _split2(lk)
            after = _dot(hi, upper) + _dot(lo, upper) + c[r0:]
            a = jnp.exp2(ls + after)
            if mask is not None:
                a = jnp.where(mask[r0:], a, 0.0)
            c_new = c[r0:] + jnp.sum(lk, axis=1, keepdims=True)
            acc_new = acc[r0:] + _dot(a.astype(BF16), vt)
            if r0:
                c_new = jnp.concatenate([c[:r0], c_new], axis=0)
                acc_new = jnp.concatenate([acc[:r0], acc_new], axis=0)
            out.append((c_new, acc_new))
        return tuple(out)

    state = tuple((jnp.zeros((tq, 1), F32), jnp.zeros((tq, HEAD_DIM), F32)) for _ in range(GROUP))
    for mth in reversed(range(tq // tk)):
        start = pl.multiple_of(q0 + mth * tk, tk)
        state = chunk(start, state, (col + mth * tk) < row, r0=mth * tk)

    n_past = i * (tq // tk)

    def body(it, state):
        return chunk(pl.multiple_of((n_past - 1 - it) * tk, tk), state, None)

    state = lax.fori_loop(0, n_past, body, state)
    for g, (_, acc) in enumerate(state):
        o_ref[:, g * HEAD_DIM:(g + 1) * HEAD_DIM] = acc.astype(BF16)


def sb_attention_prompt(q, kb, vb):
    b, t, _ = q.shape
    tq = min(1024, t)
    tk = min(256, t)
    qo, kv = _attn_specs(t, tq)
    return pl.pallas_call(
        functools.partial(_sb_kernel, tq=tq, tk=tk),
        grid=(b, N_KV_HEADS, t // tq),
        in_specs=[qo, kv, kv],
        out_specs=qo,
        out_shape=jax.ShapeDtypeStruct((b, t, Q_DIM), BF16),
        compiler_params=_params("parallel", "parallel", "parallel"),
        name="sb_attention_prompt",
    )(q, kb, vb)


def _block_mean_kernel(k_ref, o_ref, *, nb):
    t = k_ref.shape[0] // N_KV_HEADS
    o_ref[...] = jnp.zeros(o_ref.shape, F32)
    for kvh in range(N_KV_HEADS):
        x = k_ref[pl.ds(kvh, t, stride=N_KV_HEADS), :]
        o_ref[kvh, 0:nb, :] = jnp.sum(x.reshape(nb, t // nb, HEAD_DIM), axis=1) * (1.0 / (t // nb))


def block_mean(k4):
    b, rows, _ = k4.shape
    nb = rows // N_KV_HEADS // MOBA_BLOCK
    return pl.pallas_call(
        functools.partial(_block_mean_kernel, nb=nb),
        grid=(b,),
        in_specs=[pl.BlockSpec((None, rows, HEAD_DIM), lambda i: (i, 0, 0))],
        out_specs=pl.BlockSpec((None, N_KV_HEADS, LANES, HEAD_DIM), lambda i: (i, 0, 0, 0)),
        out_shape=jax.ShapeDtypeStruct((b, N_KV_HEADS, LANES, HEAD_DIM), F32),
        compiler_params=_params("parallel"),
        name="block_mean",
    )(k4)


def _top_mask(gate, n_valid, n_cand, axis):
    idx = lax.broadcasted_iota(I32, gate.shape, axis)
    cnt = jnp.zeros(gate.shape, I32)
    for jp in range(n_cand):
        other = gate[:, jp:jp + 1] if axis == 1 else gate[jp:jp + 1, :]
        beats = (other > gate) | ((other == gate) & (jp < idx))
        cnt = cnt + jnp.where(beats, jnp.where(jp < n_valid, 1, 0), 0)
    return (idx < n_valid) & (cnt < MOBA_TOPK)


def _moba_kernel(q_ref, k_ref, v_ref, km_ref, b0_ref, b1_ref, o_ref, *, tq, nb):
    i = pl.program_id(2)
    q0 = pl.multiple_of(i * tq, tq)
    row = lax.broadcasted_iota(I32, (tq, tq), 0)
    col = lax.broadcasted_iota(I32, (tq, tq), 1)
    causal = col <= row
    nbp = -(-nb // 16) * 16
    km = km_ref[0:nbp, :].astype(BF16)
    qs = [q_ref[:, g * HEAD_DIM:(g + 1) * HEAD_DIM] for g in range(GROUP)]

    def select(qh):
        gate = _dot_nt(km, qh)
        sel = jnp.where(_top_mask(gate, i, nb, 0), 1.0, 0.0)
        return jnp.concatenate([sel, jnp.zeros((LANES - nbp, tq), F32)], axis=0).T

    sels = [select(qh) for qh in qs]

    def tile(j):
        start = pl.multiple_of(j * tq, tq)
        return k_ref[pl.ds(start, tq), :], _with_ones(v_ref[pl.ds(start, tq), :])

    def picked(g, j):
        return _lane_pick(sels[g], j) > 0.0

    kt, v1 = tile(i)
    state = tuple(_softmax2_first(jnp.where(causal, _dot_nt(qs[g], kt) + b0_ref[g], MASKED), v1)
                  for g in range(GROUP))
    prev = jnp.maximum(i - 1, 0)
    kt, v1 = tile(prev)
    state = tuple(_softmax2_step(_dot_nt(qs[g], kt) + b1_ref[g], v1, *state[g], keep=picked(g, i - 1))
                  for g in range(GROUP))

    def body(j, state):
        kt, v1 = tile(j)
        return tuple(_softmax2_step(_dot_nt(qs[g], kt), v1, *state[g], keep=picked(g, j))
                     for g in range(GROUP))

    state = lax.fori_loop(0, i - 1, body, state)
    for g, (_, accl) in enumerate(state):
        o_ref[:, g * HEAD_DIM:(g + 1) * HEAD_DIM] = _softmax2_out(accl)


def _bias_tile_kernel(rb_ref, b0_ref, b1_ref, *, n):
    h = pl.program_id(0)
    row = lax.broadcasted_iota(I32, (n, n), 0)
    col = lax.broadcasted_iota(I32, (n, n), 1)
    max_exact = N_BUCKETS // 2
    for out_ref, shift in ((b0_ref, 0), (b1_ref, n)):
        dist = jnp.maximum(row - col + shift, 0)
        large = max_exact + (jnp.log(jnp.maximum(dist, 1).astype(F32) / max_exact)
                             / math.log(MAX_DISTANCE / max_exact) * (N_BUCKETS - max_exact)).astype(I32)
        bucket = jnp.where(dist < max_exact, dist, jnp.minimum(large, N_BUCKETS - 1))
        acc = jnp.zeros((n, n), F32)
        for bk in range(N_BUCKETS):
            acc = jnp.where(bucket == bk, rb_ref[bk, h], acc)
        out_ref[...] = (acc - rb_ref[N_BUCKETS - 1, h]) * LOG2E


def rel_bias_tiles(rel_bias):
    n = MOBA_BLOCK
    shape = jax.ShapeDtypeStruct((N_HEADS, n, n), F32)
    spec = pl.BlockSpec((None, n, n), lambda h: (h, 0, 0))
    return pl.pallas_call(
        functools.partial(_bias_tile_kernel, n=n),
        grid=(N_HEADS,),
        in_specs=[pl.BlockSpec(memory_space=pltpu.SMEM)],
        out_specs=[spec, spec],
        out_shape=[shape, shape],
        compiler_params=_params("parallel"),
        name="rel_bias_tiles",
    )(rel_bias)


def moba_attention_prompt(q, kb, vb, kmean, bias0, bias1):
    b, t, _ = q.shape
    tq = MOBA_BLOCK
    nb = t // tq
    qo, kv = _attn_specs(t, tq)
    head_pair = lambda b, kvh, i: (kvh, 0, 0)
    return pl.pallas_call(
        functools.partial(_moba_kernel, tq=tq, nb=nb),
        grid=(b, N_KV_HEADS, nb),
        in_specs=[qo, kv, kv,
                  pl.BlockSpec((None, None, LANES, HEAD_DIM), lambda b, kvh, i: (b, kvh, 0, 0)),
                  pl.BlockSpec((GROUP, tq, tq), head_pair),
                  pl.BlockSpec((GROUP, tq, tq), head_pair)],
        out_specs=qo,
        out_shape=jax.ShapeDtypeStruct((b, t, Q_DIM), BF16),
        compiler_params=_params("parallel", "parallel", "parallel"),
        name="moba_attention_prompt",
    )(q, kb, vb, kmean, bias0, bias1)


def _route(logits):
    idx = lax.broadcasted_iota(I32, logits.shape, 1)
    valid = idx < N_EXPERTS
    lg = jnp.where(valid, logits, MASKED)
    e = jnp.where(valid, jnp.exp(lg - jnp.max(lg, axis=1, keepdims=True)), 0.0)
    grp = idx >> 2

    def peers(x, shifts):
        for s in shifts:
            for sh in (s, LANES - s):
                oi = pltpu.roll(idx, sh, 1)
                yield pltpu.roll(x, sh, 1), oi, oi < N_EXPERTS

    cnt = jnp.zeros(logits.shape, I32)
    for oe, oi, ok in peers(e, (1, 2, 3)):
        beats = ok & ((oi >> 2) == grp) & ((oe > e) | ((oe == e) & (oi < idx)))
        cnt = cnt + jnp.where(beats, 1, 0)
    top2 = valid & (cnt < TOP_K)
    t2e = jnp.where(top2, e, 0.0)
    score = t2e
    for ot, oi, ok in peers(t2e, (1, 2, 3)):
        score = score + jnp.where(ok & ((oi >> 2) == grp), ot, 0.0)
    lost = jnp.zeros(logits.shape, I32)
    for osc, oi, ok in peers(score, (4, 8, 12)):
        beats = ok & ((osc > score) | ((osc == score) & ((oi >> 2) < grp)))
        lost = lost + jnp.where(beats, 1, 0)
    return jnp.where(top2 & valid & (lost == 0), e / score, 0.0)


def _route_t(lg):
    npos, ngrp = EXPERTS_PER_GROUP, N_GROUPS
    e = jnp.exp(lg - jnp.max(lg, axis=0, keepdims=True))
    pos = [jnp.concatenate([e[g * npos + a:g * npos + a + 1, :] for g in range(ngrp)], axis=0)
           for a in range(npos)]
    top2 = []
    for a in range(npos):
        cnt = jnp.zeros(pos[a].shape, I32)
        for b in range(npos):
            if b != a:
                cnt = cnt + jnp.where((pos[b] >= pos[a]) if b < a else (pos[b] > pos[a]), 1, 0)
        top2.append(cnt < TOP_K)
    score = functools.reduce(jnp.add, [jnp.where(t, p, 0.0) for t, p in zip(top2, pos)])
    rows = [score[g:g + 1, :] for g in range(ngrp)]
    won = []
    for g in range(ngrp):
        lost = jnp.zeros(rows[g].shape, I32)
        for o in range(ngrp):
            if o != g:
                lost = lost + jnp.where((rows[o] >= rows[g]) if o < g else (rows[o] > rows[g]), 1, 0)
        won.append(jnp.where(lost == 0, 1.0, 0.0))
    group_id = functools.reduce(jnp.add, [won[g] * float(g) for g in range(ngrp)])
    won = jnp.concatenate(won, axis=0) > 0.0
    gate = [jnp.where(t & won, p / score, 0.0) for t, p in zip(top2, pos)]
    gates = jnp.concatenate([gate[a][g:g + 1, :] for g in range(ngrp) for a in range(npos)], axis=0)
    return gates, group_id


def _oproj_kernel(o_ref, x_ref, w_ref, gt_ref, g_ref, sc_ref, sh_ref, wr_ref, br_ref,
                  x1_ref, h2_ref, gates_ref, *, transposed):
    x1 = x_ref[...] + gt_ref[...] * _dot(o_ref[...], w_ref[...])
    x1_ref[...] = x1
    h2 = _rms_mod(x1, g_ref[...], sc_ref[...], sh_ref[...])
    hi = h2.astype(BF16)
    h2_ref[...] = hi
    if transposed:
        lg = _dot_nt(wr_ref[...], hi) + br_ref[:, 0:1]
        gates, group_id = _route_t(lg)
        pad = jnp.zeros((LANES - N_EXPERTS - 1, gates.shape[1]), F32)
        gates_ref[...] = jnp.concatenate([gates, group_id, pad], axis=0).T
    else:
        gates_ref[...] = _route(_dot(hi, wr_ref[...]) + br_ref[...])


def router_operands(w_router, b_router, transposed):
    if transposed:
        return w_router.T.astype(BF16), jnp.broadcast_to(b_router[:, None], (N_EXPERTS, LANES))
    wr, br = _pad_lanes(w_router, b_router)
    return wr.astype(BF16), br


def out_proj_router(o, x, w_out_b, gt, g2, sc, sh, wr, br, *, tm, rows_per_seq):
    m, d = x.shape
    transposed = wr.shape[1] == d
    row = lambda i: (i, 0)
    const = lambda i: (0, 0)
    ms = lambda a: _mod_spec(a, tm, rows_per_seq)
    return pl.pallas_call(
        functools.partial(_oproj_kernel, transposed=transposed),
        grid=(m // tm,),
        in_specs=[pl.BlockSpec((tm, Q_DIM), row), pl.BlockSpec((tm, d), row),
                  pl.BlockSpec((Q_DIM, d), const), ms(gt), pl.BlockSpec((1, d), const),
                  ms(sc), ms(sh), pl.BlockSpec(wr.shape, const), pl.BlockSpec(br.shape, const)],
        out_specs=[pl.BlockSpec((tm, d), row), pl.BlockSpec((tm, d), row),
                   pl.BlockSpec((tm, LANES), row)],
        out_shape=[jax.ShapeDtypeStruct((m, d), F32), jax.ShapeDtypeStruct((m, d), BF16),
                   jax.ShapeDtypeStruct((m, LANES), F32)],
        compiler_params=_params("parallel"),
        name="out_proj_router",
    )(o, x, w_out_b, gt, g2, sc, sh, wr, br)


def _moe_kernel(h_ref, gates_ref, x_ref, gt_ref, wg_ref, wu_ref, wd_ref, *rest, epc, final):
    fg_ref, o_ref, acc_ref = rest if final else (None, *rest)
    c = pl.program_id(1)

    @pl.when(c == 0)
    def _():
        acc_ref[...] = jnp.zeros(acc_ref.shape, F32)

    h = h_ref[...]
    gates = gates_ref[...]
    acts = []
    for j in range(epc):
        gcol = _lane_pick(gates, c * epc + j)
        acts.append((_silu(_dot(h, wg_ref[j])) * _dot(h, wu_ref[j]) * gcol).astype(BF16))
    wd = wd_ref[...]
    acc_ref[...] += _dot(jnp.concatenate(acts, axis=1), wd.reshape(wd.shape[0] * wd.shape[1], wd.shape[2]))

    @pl.when(c == pl.num_programs(1) - 1)
    def _():
        x = x_ref[...] + gt_ref[...] * acc_ref[...]
        if fg_ref is not None:
            x = x * lax.rsqrt(jnp.mean(x * x, axis=-1, keepdims=True) + RMS_EPS) * fg_ref[...]
        o_ref[...] = x


def moe_ffn(h2, gates, x1, gt, wg_b, wu_b, wd_b, final_g, *, tm, rows_per_seq):
    m, d = x1.shape
    n_e, _, de = wg_b.shape
    epc = 4
    row = lambda i, c: (i, 0)
    gt_spec = _mod_spec(gt, tm, rows_per_seq)
    gt_spec = pl.BlockSpec(gt_spec.block_shape, lambda i, c, f=gt_spec.index_map: f(i))
    in_specs = [pl.BlockSpec((tm, d), row), pl.BlockSpec((tm, LANES), row),
                pl.BlockSpec((tm, d), row), gt_spec,
                pl.BlockSpec((epc, d, de), lambda i, c: (c, 0, 0)),
                pl.BlockSpec((epc, d, de), lambda i, c: (c, 0, 0)),
                pl.BlockSpec((epc, de, d), lambda i, c: (c, 0, 0))]
    args = [h2, gates, x1, gt, wg_b, wu_b, wd_b]
    if final_g is not None:
        in_specs.append(pl.BlockSpec((1, d), lambda i, c: (0, 0)))
        args.append(final_g)
    return pl.pallas_call(
        functools.partial(_moe_kernel, epc=epc, final=final_g is not None),
        grid=(m // tm, n_e // epc),
        in_specs=in_specs,
        out_specs=pl.BlockSpec((tm, d), row),
        out_shape=jax.ShapeDtypeStruct((m, d), F32),
        scratch_shapes=[pltpu.VMEM((tm, d), F32)],
        compiler_params=_params("parallel", "arbitrary"),
        name="moe_ffn",
    )(*args)


def _moe_sorted_kernel(cnt_ref, h_ref, route_ref, x_ref, gt_ref, wg_ref, wu_ref, wd_ref, *rest, final):
    fg_ref, o_ref, acc_ref, posc_ref, posr_ref = rest if final else (None, *rest)
    i = pl.program_id(0)
    g = pl.program_id(1)
    tm = h_ref.shape[0]
    route = route_ref[...]

    @pl.when(g == 0)
    def _():
        acc_ref[...] = jnp.zeros(acc_ref.shape, F32)
        lane = lax.broadcasted_iota(I32, (tm, LANES), 1)
        gid = route[:, GROUP_LANE:GROUP_LANE + 1]
        onehot = jnp.where(lane.astype(F32) == gid, 1.0, 0.0)
        r = lax.broadcasted_iota(I32, (tm, tm), 0)
        c = lax.broadcasted_iota(I32, (tm, tm), 1)
        earlier = jnp.where(c < r, 1.0, 0.0).astype(BF16)
        rank = jnp.sum(onehot * _dot(earlier, onehot.astype(BF16)), axis=1, keepdims=True)
        posc = jnp.where(lane == 0, gid, jnp.where(lane == 1, rank, 0.0))
        posc_ref[...] = posc
        posr_ref[...] = posc.T

    n_rows = cnt_ref[i * N_GROUPS + g]
    gf = g.astype(F32)
    pos_col = jnp.where(posc_ref[:, 0:1] == gf, posc_ref[:, 1:2], -1.0)
    pos_row = jnp.where(posr_ref[0:1, :] == gf, posr_ref[1:2, :], -1.0)
    r_hi, r_lo = _split2(route)
    wd = wd_ref[...]
    wd = wd.reshape(wd.shape[0] * wd.shape[1], wd.shape[2])
    sizes = [SORTED_ROWS] + [SORTED_ROWS // 2] * 2 + [SORTED_ROWS] * (tm // SORTED_ROWS - 2)
    sizes = sizes if tm >= 2 * SORTED_ROWS else [tm]
    for start, rows in zip([sum(sizes[:k]) for k in range(len(sizes))], sizes):
        @pl.when(n_rows > start)
        def _(base=float(start), rows=rows):
            rid = lax.broadcasted_iota(I32, (rows, tm), 0).astype(F32) + base
            pack = jnp.where(pos_row == rid, 1.0, 0.0).astype(BF16)
            xs = _dot(pack, h_ref[...]).astype(BF16)
            gs = _dot(pack, r_hi) + _dot(pack, r_lo)
            acts = []
            for j in range(EXPERTS_PER_GROUP):
                gcol = _lane_pick(gs, g * EXPERTS_PER_GROUP + j)
                acts.append((_silu(_dot(xs, wg_ref[j])) * _dot(xs, wu_ref[j]) * gcol).astype(BF16))
            ys = _dot(jnp.concatenate(acts, axis=1), wd)
            cid = lax.broadcasted_iota(I32, (tm, rows), 1).astype(F32) + base
            unpack = jnp.where(pos_col == cid, 1.0, 0.0).astype(BF16)
            acc_ref[...] += _dot(unpack, ys.astype(BF16))

    @pl.when(g == pl.num_programs(1) - 1)
    def _():
        x = x_ref[...] + gt_ref[...] * acc_ref[...]
        if fg_ref is not None:
            x = x * lax.rsqrt(jnp.mean(x * x, axis=-1, keepdims=True) + RMS_EPS) * fg_ref[...]
        o_ref[...] = x


def moe_ffn_sorted(h2, route, x1, gt, wg_b, wu_b, wd_b, final_g, *, tm, rows_per_seq):
    m, d = x1.shape
    _, _, de = wg_b.shape
    npos = EXPERTS_PER_GROUP
    n_tiles = m // tm
    group = route[:, GROUP_LANE].reshape(n_tiles, 1, tm)
    counts = jnp.sum(group == jnp.arange(N_GROUPS, dtype=F32)[None, :, None], axis=2, dtype=I32)
    row = lambda i, g, cnt: (i, 0)
    gt_spec = _mod_spec(gt, tm, rows_per_seq)
    gt_spec = pl.BlockSpec(gt_spec.block_shape, lambda i, g, cnt, f=gt_spec.index_map: f(i))
    in_specs = [pl.BlockSpec((tm, d), row), pl.BlockSpec((tm, LANES), row),
                pl.BlockSpec((tm, d), row), gt_spec,
                pl.BlockSpec((npos, d, de), lambda i, g, cnt: (g, 0, 0)),
                pl.BlockSpec((npos, d, de), lambda i, g, cnt: (g, 0, 0)),
                pl.BlockSpec((npos, de, d), lambda i, g, cnt: (g, 0, 0))]
    args = [h2, route, x1, gt, wg_b, wu_b, wd_b]
    if final_g is not None:
        in_specs.append(pl.BlockSpec((1, d), lambda i, g, cnt: (0, 0)))
        args.append(final_g)
    grid_spec = pltpu.PrefetchScalarGridSpec(
        num_scalar_prefetch=1,
        grid=(n_tiles, N_GROUPS),
        in_specs=in_specs,
        out_specs=pl.BlockSpec((tm, d), row),
        scratch_shapes=[pltpu.VMEM((tm, d), F32), pltpu.VMEM((tm, LANES), F32), pltpu.VMEM((LANES, tm), F32)],
    )
    return pl.pallas_call(
        functools.partial(_moe_sorted_kernel, final=final_g is not None),
        grid_spec=grid_spec,
        out_shape=jax.ShapeDtypeStruct((m, d), F32),
        compiler_params=_params("parallel", "arbitrary"),
        name="moe_ffn_sorted",
    )(counts.reshape(-1), *args)


def _head_rows(mats):
    rowi = lax.broadcasted_iota(I32, mats[0].shape, 0)
    out = mats[0]
    for kvh in range(1, N_KV_HEADS):
        out = jnp.where((rowi >> 1) == kvh, mats[kvh], out)
    return out


def _kv_rows(page_ref):
    return [page_ref[pl.ds(kvh, PAGE_SIZE, stride=N_KV_HEADS), :] for kvh in range(N_KV_HEADS)]


def _new_token_logits(q, kn):
    qf = q.astype(F32)
    prods = [qf * kn[kvh:kvh + 1, :].astype(BF16).astype(F32) for kvh in range(N_KV_HEADS)]
    return jnp.sum(_head_rows(prods), axis=1, keepdims=True)


def _new_token_values(vn):
    return _head_rows([jnp.broadcast_to(vn[kvh:kvh + 1, :].astype(BF16).astype(F32), (HEAD_ROWS, HEAD_DIM))
                       for kvh in range(N_KV_HEADS)])


def _page_spec(layer, n_pages, page_of):
    return pl.BlockSpec((None, None, PAGE_ROWS, HEAD_DIM),
                        lambda b, c, pt: (layer, pt[b * n_pages + page_of(c)], 0, 0))


def _seq_spec(rows, width):
    return pl.BlockSpec((None, rows, width), lambda b, c, pt: (b, 0, 0))


def _suffix_total_matrix(n):
    upper, ones = _suffix_matrices(n)
    return jnp.concatenate([upper, ones], axis=1)


def _suffix_and_total(xs, suffix_total):
    sums = []
    for x in xs:
        hi, lo = _split2(x)
        sums.append(_dot(hi, suffix_total) + _dot(lo, suffix_total))
    return [s[:, :PAGE_SIZE] for s in sums], [s[:, PAGE_SIZE:] for s in sums]


def _page_logits(q, k_refs):
    return [_head_rows([_dot_nt(q, ks.astype(BF16)) for ks in _kv_rows(ref)]) for ref in k_refs]


def _page_values(w_pages, v_refs):
    pv = [jnp.zeros((HEAD_ROWS, HEAD_DIM), F32)] * N_KV_HEADS
    for w, ref in zip(w_pages, v_refs):
        wb = w.astype(BF16)
        pv = [a + _dot(wb, vs.astype(BF16)) for a, vs in zip(pv, _kv_rows(ref))]
    return _head_rows(pv)


def _carry_back(car, sufs, tots):
    after = [None] * len(sufs)
    for r in reversed(range(len(sufs))):
        after[r] = car + sufs[r]
        car = car + tots[r]
    return after, car


def _stream_pages(pt_ref, sources, bufs, sem, *, pages, n_pages, nc):
    step = pl.program_id(0) * nc + pl.program_id(1)
    total = pl.num_programs(0) * nc

    def copies(s, slot):
        first = (s // nc) * n_pages + (nc - 1 - s % nc) * pages
        return [pltpu.make_async_copy(src.at[layer, pt_ref[first + r]], buf.at[slot, r], sem.at[slot, i, r])
                for r in range(pages) for i, ((src, layer), buf) in enumerate(zip(sources, bufs))]

    slot = step % 2

    @pl.when(step == 0)
    def _():
        for cp in copies(step, slot):
            cp.start()

    @pl.when(step + 1 < total)
    def _():
        for cp in copies(step + 1, 1 - slot):
            cp.start()

    for cp in copies(step, slot):
        cp.wait()
    return slot


def _stream_scratch(pages, shapes):
    return ([pltpu.VMEM((2, pages) + shape, F32) for shape in shapes]
            + [pltpu.SemaphoreType.DMA((2, len(shapes), pages))])


def _fox_dec_kernel(pt_ref, q_ref, kn_ref, vn_ref, lfn_ref, ck_ref, cv_ref, clf_ref, o_ref,
                    m_s, l_s, acc_s, car_s, kbuf, vbuf, lfbuf, sem, *, pages, n_pages, nc, layer, fox_layer):
    slot = _stream_pages(pt_ref, [(ck_ref, layer), (cv_ref, layer), (clf_ref, fox_layer)],
                         [kbuf, vbuf, lfbuf], sem, pages=pages, n_pages=n_pages, nc=nc)
    k_refs = [kbuf.at[slot, r] for r in range(pages)]
    v_refs = [vbuf.at[slot, r] for r in range(pages)]
    lf_refs = [lfbuf.at[slot, r] for r in range(pages)]
    c = pl.program_id(1)
    q = q_ref[...]
    suffix_total = _suffix_total_matrix(PAGE_SIZE)

    @pl.when(c == 0)
    def _():
        m_s[...] = jnp.broadcast_to(_new_token_logits(q, kn_ref[...]), m_s.shape)
        l_s[...] = jnp.ones(l_s.shape, F32)
        acc_s[...] = _new_token_values(vn_ref[...])
        car_s[...] = lfn_ref[...]

    pad = jnp.zeros((HEAD_ROWS - N_HEADS, PAGE_SIZE), F32)
    sufs, tots = _suffix_and_total([jnp.concatenate([ref[...], pad], axis=0) for ref in lf_refs], suffix_total)
    qk = _page_logits(q, k_refs)
    decay, car_s[...] = _carry_back(car_s[...], sufs, tots)
    s_pages = [a + b for a, b in zip(qk, decay)]

    m_old = m_s[...]
    mx = functools.reduce(jnp.maximum, s_pages)
    m_new = jnp.maximum(m_old, jnp.max(mx, axis=1, keepdims=True))
    alpha = jnp.exp(m_old - m_new)
    p_pages = [jnp.exp(s - m_new) for s in s_pages]
    l_s[...] = l_s[...] * alpha + jnp.sum(functools.reduce(jnp.add, p_pages), axis=1, keepdims=True)
    acc_s[...] = acc_s[...] * alpha + _page_values(p_pages, v_refs)
    m_s[...] = m_new

    @pl.when(c == pl.num_programs(1) - 1)
    def _():
        o_ref[...] = acc_s[...] / l_s[...]


def fox_attention_decode(q16, kn, vn, lfn, cache_k, cache_v, cache_lft, page_table, layer, fox_layer):
    b = q16.shape[0]
    n_pages = page_table.shape[1]
    pages = min(32, n_pages)
    nc = n_pages // pages
    hbm = pl.BlockSpec(memory_space=pl.ANY)
    grid_spec = pltpu.PrefetchScalarGridSpec(
        num_scalar_prefetch=1,
        grid=(b, nc),
        in_specs=[_seq_spec(HEAD_ROWS, HEAD_DIM), _seq_spec(N_KV_HEADS, HEAD_DIM),
                  _seq_spec(N_KV_HEADS, HEAD_DIM), _seq_spec(HEAD_ROWS, PAGE_SIZE), hbm, hbm, hbm],
        out_specs=_seq_spec(HEAD_ROWS, HEAD_DIM),
        scratch_shapes=[pltpu.VMEM((HEAD_ROWS, LANES), F32)] * 4
        + _stream_scratch(pages, [(PAGE_ROWS, HEAD_DIM), (PAGE_ROWS, HEAD_DIM), (N_HEADS, PAGE_SIZE)]),
    )
    return pl.pallas_call(
        functools.partial(_fox_dec_kernel, pages=pages, n_pages=n_pages, nc=nc, layer=layer,
                          fox_layer=fox_layer),
        grid_spec=grid_spec,
        out_shape=jax.ShapeDtypeStruct((b, HEAD_ROWS, HEAD_DIM), F32),
        compiler_params=_params("arbitrary", "arbitrary"),
        name="fox_attention_decode",
    )(page_table.reshape(-1), q16, kn, vn, lfn, cache_k, cache_v, cache_lft)


def _sb_dec_kernel(pt_ref, q_ref, ck_ref, cv_ref, o_ref, acc_s, car_s, kbuf, vbuf, sem,
                   *, pages, n_pages, nc, layer):
    slot = _stream_pages(pt_ref, [(ck_ref, layer), (cv_ref, layer)], [kbuf, vbuf], sem,
                         pages=pages, n_pages=n_pages, nc=nc)
    k_refs = [kbuf.at[slot, r] for r in range(pages)]
    v_refs = [vbuf.at[slot, r] for r in range(pages)]
    c = pl.program_id(1)
    q = q_ref[...]
    suffix_total = _suffix_total_matrix(PAGE_SIZE)

    @pl.when(c == 0)
    def _():
        acc_s[...] = jnp.zeros(acc_s.shape, F32)
        car_s[...] = jnp.zeros(car_s.shape, F32)

    zs = _page_logits(q, k_refs)
    lks = [_neg_softplus(z) for z in zs]
    sufs, tots = _suffix_and_total(lks, suffix_total)
    after, car_s[...] = _carry_back(car_s[...], sufs, tots)
    a_pages = [jnp.exp(z + lk + af) for z, lk, af in zip(zs, lks, after)]
    acc_s[...] += _page_values(a_pages, v_refs)

    @pl.when(c == pl.num_programs(1) - 1)
    def _():
        o_ref[...] = acc_s[...]


def sb_attention_decode(q16, cache_k, cache_v, page_table, layer):
    b = q16.shape[0]
    n_pages = page_table.shape[1]
    pages = min(32, n_pages)
    nc = n_pages // pages
    hbm = pl.BlockSpec(memory_space=pl.ANY)
    grid_spec = pltpu.PrefetchScalarGridSpec(
        num_scalar_prefetch=1,
        grid=(b, nc),
        in_specs=[_seq_spec(HEAD_ROWS, HEAD_DIM), hbm, hbm],
        out_specs=_seq_spec(HEAD_ROWS, HEAD_DIM),
        scratch_shapes=[pltpu.VMEM((HEAD_ROWS, LANES), F32)] * 2
        + _stream_scratch(pages, [(PAGE_ROWS, HEAD_DIM), (PAGE_ROWS, HEAD_DIM)]),
    )
    return pl.pallas_call(
        functools.partial(_sb_dec_kernel, pages=pages, n_pages=n_pages, nc=nc, layer=layer),
        grid_spec=grid_spec,
        out_shape=jax.ShapeDtypeStruct((b, HEAD_ROWS, HEAD_DIM), F32),
        compiler_params=_params("arbitrary", "arbitrary"),
        name="sb_attention_decode",
    )(page_table.reshape(-1), q16, cache_k, cache_v)


def _page_head_sums(page_ref):
    x = page_ref[...]
    s8 = jnp.sum(x.reshape(PAGE_ROWS // 8, 8, HEAD_DIM), axis=0)
    return s8[:N_KV_HEADS] + s8[N_KV_HEADS:]


def _moba_sel_kernel(pt_ref, q_ref, *refs, pages, nb):
    k_refs = refs[:pages]
    km_ref, sel_ref = refs[pages:]
    c = pl.program_id(1)
    bps = pages // 2
    sums = [_page_head_sums(k_refs[r]) for r in range(pages)]
    means = [(sums[2 * j] + sums[2 * j + 1]) * (1.0 / MOBA_BLOCK) for j in range(bps)]
    row0 = pl.multiple_of(c * bps, bps)
    for kvh in range(N_KV_HEADS):
        km_ref[kvh, pl.ds(row0, bps), :] = jnp.concatenate([mj[kvh:kvh + 1] for mj in means], axis=0)

    @pl.when(c == pl.num_programs(1) - 1)
    def _():
        q = q_ref[...]
        lane = lax.broadcasted_iota(I32, (nb, LANES), 1)
        gate = jnp.zeros((nb, LANES), F32)
        for kvh in range(N_KV_HEADS):
            gate = jnp.where((lane >> 1) == kvh, _dot_nt(km_ref[kvh].astype(BF16), q), gate)
        sel = _top_mask(gate, nb, nb, 0)
        blk = lax.broadcasted_iota(I32, (nb, LANES), 0).astype(F32)
        rows = []
        for _ in range(MOBA_TOPK):
            first = jnp.min(jnp.where(sel, blk, float(nb)), axis=0, keepdims=True)
            rows.append(first.astype(I32))
            sel = sel & (blk != first)
        rows.append(jnp.zeros((8 - MOBA_TOPK, LANES), I32))
        sel_ref[...] = jnp.concatenate(rows, axis=0)


def moba_select_decode(q128, cache_k, page_table, layer):
    b = q128.shape[0]
    n_pages = page_table.shape[1]
    nb = n_pages * PAGE_SIZE // MOBA_BLOCK
    pages = min(16, n_pages)
    nc = n_pages // pages
    k_specs = [_page_spec(layer, n_pages, lambda c, r=r: c * pages + r) for r in range(pages)]
    grid_spec = pltpu.PrefetchScalarGridSpec(
        num_scalar_prefetch=1,
        grid=(b, nc),
        in_specs=[_seq_spec(LANES, HEAD_DIM)] + k_specs,
        out_specs=[pl.BlockSpec((None, N_KV_HEADS, nb, HEAD_DIM), lambda b, c, pt: (b, 0, 0, 0)),
                   _seq_spec(8, LANES)],
    )
    _, sel = pl.pallas_call(
        functools.partial(_moba_sel_kernel, pages=pages, nb=nb),
        grid_spec=grid_spec,
        out_shape=[jax.ShapeDtypeStruct((b, N_KV_HEADS, nb, HEAD_DIM), F32),
                   jax.ShapeDtypeStruct((b, 8, LANES), I32)],
        compiler_params=_params("parallel", "arbitrary"),
        name="moba_select_decode",
    )(page_table.reshape(-1), q128, *([cache_k] * pages))
    return sel


def _moba_dec_kernel(pt_ref, sel_ref, q_ref, kn_ref, vn_ref, bt_ref, *refs, nb):
    n = MOBA_TOPK * 2
    k_refs = refs[:n]
    v_refs = refs[n:2 * n]
    o_ref = refs[2 * n]
    b = pl.program_id(0)
    h = pl.program_id(1)
    head_rows = pl.ds(h // GROUP, PAGE_SIZE, stride=N_KV_HEADS)
    q = q_ref[...]
    bt = bt_ref[...]
    far = bt[:, 2 * PAGE_SIZE:3 * PAGE_SIZE]
    s_new = _new_token_logits(q, kn_ref[...]) + bt[:, 3 * PAGE_SIZE:3 * PAGE_SIZE + 1]

    s_piece = []
    for slot in range(MOBA_TOPK):
        last = sel_ref[(b * MOBA_TOPK + slot) * N_HEADS + h] == nb - 1
        for r in range(2):
            kp = k_refs[slot * 2 + r][head_rows, :].astype(BF16)
            bias = jnp.where(last, bt[:, r * PAGE_SIZE:(r + 1) * PAGE_SIZE], far)
            s_piece.append(_dot_nt(q, kp) + bias)
    mx = functools.reduce(jnp.maximum, s_piece)
    m = jnp.maximum(jnp.max(mx, axis=1, keepdims=True), s_new)
    p_piece = [jnp.exp(s - m) for s in s_piece]
    p_new = jnp.exp(s_new - m)
    l = jnp.sum(functools.reduce(jnp.add, p_piece), axis=1, keepdims=True) + p_new
    acc = p_new.astype(BF16).astype(F32) * _new_token_values(vn_ref[...])
    for i in range(n):
        acc = acc + _dot(p_piece[i].astype(BF16), v_refs[i][head_rows, :].astype(BF16))
    o = acc / l

    @pl.when(h == 0)
    def _():
        o_ref[...] = jnp.zeros(o_ref.shape, F32)

    rowi = lax.broadcasted_iota(I32, o.shape, 0)
    o_ref[pl.ds(h, 1), :] = jnp.sum(jnp.where(rowi == h, o, 0.0), axis=0, keepdims=True)


def moba_attention_decode(q16, kn, vn, bias_tab, sel, cache_k, cache_v, page_table, layer):
    b = q16.shape[0]
    n_pages = page_table.shape[1]
    nb = n_pages * PAGE_SIZE // MOBA_BLOCK

    def page_spec(slot, r):
        def index(b, h, pt, sel):
            blk = sel[(b * MOBA_TOPK + slot) * N_HEADS + h]
            return (layer, pt[b * n_pages + 2 * blk + r], 0, 0)
        return pl.BlockSpec((None, None, PAGE_ROWS, HEAD_DIM), index)

    kv_specs = [page_spec(slot, r) for slot in range(MOBA_TOPK) for r in range(2)]
    seq = lambda rows, width: pl.BlockSpec((None, rows, width), lambda b, h, pt, sel: (b, 0, 0))
    grid_spec = pltpu.PrefetchScalarGridSpec(
        num_scalar_prefetch=2,
        grid=(b, N_HEADS),
        in_specs=[seq(HEAD_ROWS, HEAD_DIM), seq(N_KV_HEADS, HEAD_DIM), seq(N_KV_HEADS, HEAD_DIM),
                  pl.BlockSpec((HEAD_ROWS, 4 * PAGE_SIZE), lambda b, h, pt, sel: (0, 0))] + kv_specs + kv_specs,
        out_specs=seq(HEAD_ROWS, HEAD_DIM),
    )
    n = len(kv_specs)
    return pl.pallas_call(
        functools.partial(_moba_dec_kernel, nb=nb),
        grid_spec=grid_spec,
        out_shape=jax.ShapeDtypeStruct((b, HEAD_ROWS, HEAD_DIM), F32),
        compiler_params=_params("parallel", "arbitrary"),
        name="moba_attention_decode",
    )(page_table.reshape(-1), sel.reshape(-1), q16, kn, vn, bias_tab, *([cache_k] * n), *([cache_v] * n))


def _rel_bias_by_distance(rel_bias, n):
    dist = jnp.arange(n, dtype=I32)
    max_exact = N_BUCKETS // 2
    large = max_exact + (jnp.log(jnp.maximum(dist, 1).astype(F32) / max_exact)
                         / math.log(MAX_DISTANCE / max_exact)
                         * (N_BUCKETS - max_exact)).astype(I32)
    large = jnp.minimum(large, N_BUCKETS - 1)
    return rel_bias[jnp.where(dist < max_exact, dist, large)]


def _pad_lanes(w, b):
    n = w.shape[1]
    return jnp.pad(w, ((0, 0), (0, LANES - n))), jnp.pad(b, (0, LANES - n)).reshape(1, LANES)


def _trunk(x, mod, cache, rel_bias, weights):
    (norm_g, final_g, w_in_b, w_out_b, fgates, w_router, b_router, wg_b, wu_b, wd_b) = weights
    b, t, d = x.shape
    m = b * t
    depth = w_in_b.shape[0]
    paged = cache is not None
    if paged:
        assert t == 1
        cache_k, cache_v, cache_lft, page_table = cache
        n_pages = page_table.shape[1]
        nb_past = n_pages * PAGE_SIZE // MOBA_BLOCK
        assert (n_pages * PAGE_SIZE) % MOBA_BLOCK == 0 and nb_past >= MOBA_TOPK
        tm_tok = tm_moe = m
        bias_d = _rel_bias_by_distance(rel_bias, 2 * MOBA_BLOCK + 1)
        assert MAX_DISTANCE <= MOBA_BLOCK
        near = bias_d[MOBA_BLOCK - jnp.arange(MOBA_BLOCK)].T
        tab = jnp.concatenate([near, jnp.broadcast_to(bias_d[2 * MOBA_BLOCK][:, None], (N_HEADS, PAGE_SIZE)),
                               jnp.broadcast_to(bias_d[0][:, None], (N_HEADS, PAGE_SIZE))], axis=1)
        bias_tab = jnp.pad(tab, ((0, HEAD_ROWS - N_HEADS), (0, 0)))
    else:
        assert t % MOBA_BLOCK == 0 and t // MOBA_BLOCK <= LANES
        tm_tok = min(512, t)
        tm_moe = min(1024, t)
        assert MAX_DISTANCE <= MOBA_BLOCK
        bias0, bias1 = rel_bias_tiles(rel_bias)

    q_scale = HEAD_DIM ** -0.5 * (1.0 if paged else LOG2E)
    wr, br = router_operands(w_router, b_router, transposed=tm_tok >= LANES)

    def mod_part(l, j):
        part = mod[l, :, j * d:(j + 1) * d]
        return part if paged else part.reshape(b, 1, d)

    xf = x.reshape(m, d)
    new_k, new_v, new_logf = [], [], []
    for l in range(depth):
        kind = l % N_MIXERS
        sh1, sc1, gt1, sh2, sc2, gt2 = [mod_part(l, j) for j in range(6)]
        fg = fgates[l // N_MIXERS] if kind == 0 else None
        outs = norm_qkv(xf, norm_g[l, 0].reshape(1, d), sc1, sh1, w_in_b[l], fg,
                        tm=tm_tok, rows_per_seq=t, q_scale=q_scale)
        q, k, v, kb, vb = outs[:5]
        new_k.append(k.reshape(b, t, N_KV_HEADS, HEAD_DIM))
        new_v.append(v.reshape(b, t, N_KV_HEADS, HEAD_DIM))
        if kind == 0:
            lf = outs[5]
            new_logf.append(lf[:, :N_HEADS].reshape(b, t, N_HEADS))
        if paged:
            q16 = jnp.pad(q.reshape(b, N_HEADS, HEAD_DIM), ((0, 0), (0, HEAD_ROWS - N_HEADS), (0, 0)))
            kn = k.reshape(b, N_KV_HEADS, HEAD_DIM)
            vn = v.reshape(b, N_KV_HEADS, HEAD_DIM)
            if kind == 0:
                lfn = jnp.broadcast_to(
                    jnp.pad(lf[:, :N_HEADS], ((0, 0), (0, HEAD_ROWS - N_HEADS)))[:, :, None],
                    (b, HEAD_ROWS, PAGE_SIZE))
                o16 = fox_attention_decode(q16, kn, vn, lfn, cache_k, cache_v, cache_lft, page_table,
                                           l, l // N_MIXERS)
            elif kind == 1:
                q128 = jnp.pad(q.reshape(b, N_HEADS, HEAD_DIM), ((0, 0), (0, LANES - N_HEADS), (0, 0)))
                sel = moba_select_decode(q128, cache_k, page_table, l)
                sel = sel[:, :MOBA_TOPK, :N_HEADS]
                o16 = moba_attention_decode(q16, kn, vn, bias_tab, sel, cache_k, cache_v, page_table, l)
            else:
                o16 = sb_attention_decode(q16, cache_k, cache_v, page_table, l)
            o = o16[:, :N_HEADS, :].reshape(m, Q_DIM).astype(BF16)
        else:
            q3 = q.reshape(b, t, Q_DIM)
            kb3 = kb.reshape(b, t, KV_DIM)
            vb3 = vb.reshape(b, t, KV_DIM)
            if kind == 0:
                o = fox_attention_prompt(q3, kb3, vb3, cumsum_time(lf.reshape(b, t, LANES)))
            elif kind == 1:
                kmean = block_mean(k.reshape(b, t * N_KV_HEADS, HEAD_DIM))
                o = moba_attention_prompt(q3, kb3, vb3, kmean, bias0, bias1)
            else:
                o = sb_attention_prompt(q3, kb3, vb3)
            o = o.reshape(m, Q_DIM)
        x1, h2, gates = out_proj_router(o, xf, w_out_b[l], gt1, norm_g[l, 1].reshape(1, d), sc2, sh2,
                                        wr, br, tm=tm_tok, rows_per_seq=t)
        fin = final_g.reshape(1, d) if l == depth - 1 else None
        moe = moe_ffn if paged else moe_ffn_sorted
        xf = moe(h2, gates, x1, gt2, wg_b[l], wu_b[l], wd_b[l], fin, tm=tm_moe, rows_per_seq=t)
    return xf.reshape(b, t, d), jnp.stack(new_k), jnp.stack(new_v), jnp.stack(new_logf)


def kernel(x_prompt, x_sample, cache_k, cache_v, cache_logf, page_table, c_prompt, c_sample, rel_bias,
           w_ada, b_ada, norm_g, final_g, w_in, w_out, w_fgate, b_fgate, w_router, b_router,
           w_gate, w_up, w_down):
    n_prompt = c_prompt.shape[0]
    mod = ada_modulation(jnp.concatenate([c_prompt, c_sample], axis=0), w_ada, b_ada)

    fgates = []
    for a in range(w_fgate.shape[0]):
        wf, bf = _pad_lanes(w_fgate[a], b_fgate[a])
        fgates.append((wf.astype(BF16), bf))
    weights = (norm_g, final_g, w_in.astype(BF16), w_out.astype(BF16), fgates, w_router, b_router,
               w_gate.astype(BF16), w_up.astype(BF16), w_down.astype(BF16))

    depth, n_pool = cache_k.shape[:2]
    cache = (cache_k.reshape(depth, n_pool, PAGE_ROWS, HEAD_DIM),
             cache_v.reshape(depth, n_pool, PAGE_ROWS, HEAD_DIM),
             jnp.swapaxes(cache_logf, 2, 3), page_table)

    y_p, k_p, v_p, lf_p = _trunk(x_prompt, mod[:, :n_prompt], None, rel_bias, weights)
    y_s, k_s, v_s, lf_s = _trunk(x_sample, mod[:, n_prompt:], cache, rel_bias, weights)
    return (y_p, y_s, k_p, v_p, lf_p, k_s, v_s, lf_s)
```
